```python
import math
import jax
import jax.numpy as jnp
from jax import lax
import numpy as np

D_MODEL = 1024
BATCH = 16
SEQ = 4096
DEPTH = 1
DEC_BATCH = 128
DEC_SEQ = 1
PAST_LEN = 8192
PAGE_SIZE = 128

GDN_HEADS = 8
GDN_DK = 128
GDN_DV = 128
GDN_QK = GDN_HEADS * GDN_DK
GDN_V = GDN_HEADS * GDN_DV
CONV_DIM = 2 * GDN_QK + GDN_V
CONV_WIDTH = 4
CHUNK = 64

DIL_GROUPS = ((128, 1), (512, 4), (2048, 16))
N_GROUPS = 3
DIL_HEADS = 4
DIL_HD = 128
DIL_W = DIL_HEADS * DIL_HD
DIL_QKV = N_GROUPS * 3 * DIL_W

REL_BUCKETS = 32
REL_MAX_DIST = 2048

IN_SIZES = (CONV_DIM, GDN_V, GDN_HEADS, GDN_HEADS, DIL_QKV, DIL_W, D_MODEL, D_MODEL)
IN_WIDTH = 11280
RMS_EPS = 1e-6
NEG = -1e30

kernel_name = 'hybrid_gdn_dilated_decoder_step'


def rms_norm(x, gain):
    xf = x.astype(jnp.float32)
    y = xf * lax.rsqrt(jnp.mean(xf * xf, axis=-1, keepdims=True) + RMS_EPS)
    return (y * gain.astype(jnp.float32)).astype(x.dtype)


def l2_normalize(x):
    return x * lax.rsqrt(jnp.sum(x * x, axis=-1, keepdims=True) + RMS_EPS)


def rel_bucket(dist):
    max_exact = REL_BUCKETS // 2
    d = jnp.maximum(dist, 1).astype(jnp.float32)
    large = max_exact + (jnp.log(d / max_exact) / math.log(REL_MAX_DIST / max_exact)
                         * (REL_BUCKETS - max_exact)).astype(jnp.int32)
    large = jnp.minimum(large, REL_BUCKETS - 1)
    return jnp.where(dist < max_exact, dist, large)


def group_biases(rel_table):
    biases = []
    for gi, (win, dil) in enumerate(DIL_GROUPS):
        dist = jnp.arange(win // dil + 1, dtype=jnp.int32) * dil
        b = rel_table[rel_bucket(dist)][:, gi * DIL_HEADS:(gi + 1) * DIL_HEADS]
        biases.append(b.T.astype(jnp.float32))
    return biases


def mixer_inputs(x, g_pre, w_in):
    xn = rms_norm(x, g_pre)
    proj = jnp.einsum('bld,de->ble', xn, w_in)
    offsets = np.cumsum(IN_SIZES)[:-1].tolist()
    return jnp.split(proj, offsets, axis=-1)


def short_conv_silu(x_ext, conv_w):
    rhs = conv_w.T[:, None, :].astype(x_ext.dtype)
    y = lax.conv_general_dilated(x_ext, rhs, window_strides=(1,), padding='VALID',
                                 dimension_numbers=('NWC', 'WIO', 'NWC'),
                                 feature_group_count=x_ext.shape[-1])
    return jax.nn.silu(y)


def gated_delta_rule(q, k, v, g, beta, s0):
    B, L, H, DK = q.shape
    DV = v.shape[-1]
    c = math.gcd(L, CHUNK)
    n = L // c

    def blocks(a):
        return jnp.moveaxis(a.reshape((B, n, c) + a.shape[2:]), 3, 2)

    q, k, v, g, beta = blocks(q), blocks(k), blocks(v), blocks(g), blocks(beta)
    G = jnp.cumsum(g, axis=-1)
    causal = jnp.tril(jnp.ones((c, c), dtype=bool))
    strict = jnp.tril(jnp.ones((c, c), dtype=bool), k=-1)
    decay = jnp.exp(jnp.where(causal, G[..., :, None] - G[..., None, :], NEG))
    kb = k * beta[..., None]
    a_mat = jnp.where(strict, jnp.einsum('bnhcd,bnhed->bnhce', kb, k) * decay, 0.0)

    def solve(rhs):
        return lax.linalg.triangular_solve(a_mat, rhs, left_side=True, lower=True,
                                           unit_diagonal=True)

    u = solve(v * beta[..., None])
    w = solve(kb * jnp.exp(G)[..., None])
    qk = jnp.einsum('bnhcd,bnhed->bnhce', q, k) * decay
    q_dec = q * jnp.exp(G)[..., None]
    k_dec = k * jnp.exp(G[..., -1:] - G)[..., None]
    g_last = jnp.exp(G[..., -1])
    xs = tuple(jnp.moveaxis(a, 1, 0) for a in (u, w, qk, q_dec, k_dec, g_last))

    def step(S, inp):
        u_c, w_c, qk_c, qd_c, kd_c, gl_c = inp
        v_new = u_c - jnp.einsum('bhcd,bhde->bhce', w_c, S)
        o_c = jnp.einsum('bhcd,bhde->bhce', qd_c, S) + jnp.einsum('bhce,bhef->bhcf', qk_c, v_new)
        S = S * gl_c[..., None, None] + jnp.einsum('bhcd,bhce->bhde', kd_c, v_new)
        return S, o_c

    s_fin, o = lax.scan(step, s0, xs)
    o = jnp.transpose(o, (1, 0, 3, 2, 4)).reshape(B, L, H, DV)
    return o, s_fin


def gdn_branch(x_ext, s0, z_a, a_a, b_a, conv_w, a_log, dt_bias, g_head_norm, w_proj_a):
    B, L, _ = z_a.shape
    f32 = jnp.float32
    qkv = short_conv_silu(x_ext, conv_w).astype(f32)
    q, k, v = jnp.split(qkv, [GDN_QK, 2 * GDN_QK], axis=-1)
    q = l2_normalize(q.reshape(B, L, GDN_HEADS, GDN_DK)) * (GDN_DK ** -0.5)
    k = l2_normalize(k.reshape(B, L, GDN_HEADS, GDN_DK))
    v = v.reshape(B, L, GDN_HEADS, GDN_DV)
    g = -jnp.exp(a_log.astype(f32)) * jax.nn.softplus(a_a.astype(f32) + dt_bias.astype(f32))
    beta = jax.nn.sigmoid(b_a.astype(f32))
    o, s_fin = gated_delta_rule(q, k, v, g, beta, s0.astype(f32))
    o = o * lax.rsqrt(jnp.mean(o * o, axis=-1, keepdims=True) + RMS_EPS) * g_head_norm.astype(f32)
    y = o.reshape(B, L, GDN_V) * jax.nn.silu(z_a.astype(f32))
    return jnp.einsum('ble,ed->bld', y.astype(z_a.dtype), w_proj_a), s_fin


def dilated_attn_prompt(q, k, v, bias_j, win, dil):
    f32 = jnp.float32
    B, S, H, D = q.shape
    blk = win // dil
    span = blk * dil
    s_pad = -(-S // span) * span
    nb = s_pad // span

    def to_blocks(a):
        a = jnp.pad(a.astype(f32), ((0, 0), (0, s_pad - S), (0, 0), (0, 0)))
        return a.reshape(B, nb, blk, dil, H, D).transpose(0, 3, 1, 2, 4, 5)

    def with_prev(a):
        prev = jnp.pad(a, ((0, 0), (0, 0), (1, 0), (0, 0), (0, 0), (0, 0)))[:, :, :-1]
        return jnp.concatenate([prev, a], axis=3)

    qb = to_blocks(q)
    kk = with_prev(to_blocks(k))
    vv = with_prev(to_blocks(v))
    logits = jnp.einsum('brnqhd,brnkhd->brnhqk', qb, kk) * (D ** -0.5)
    j = blk + jnp.arange(blk)[:, None] - jnp.arange(2 * blk)[None, :]
    in_band = (j >= 0) & (j <= blk)
    after_start = (jnp.arange(nb)[:, None, None] > 0) | (jnp.arange(2 * blk)[None, None, :] >= blk)
    mask = in_band[None] & after_start
    logits = logits + bias_j[:, jnp.clip(j, 0, blk)]
    logits = jnp.where(mask[:, None], logits, NEG)
    m = jnp.max(logits, axis=-1, keepdims=True)
    p = jnp.exp(logits - m)
    s = jnp.sum(p, axis=-1, keepdims=True)
    o = jnp.einsum('brnhqk,brnkhd->brnqhd', p, vv) / jnp.swapaxes(s, 3, 4)
    lse = (m + jnp.log(s))[..., 0]
    o = o.transpose(0, 2, 3, 1, 4, 5).reshape(B, s_pad, H, D)[:, :S]
    lse = lse.transpose(0, 2, 4, 1, 3).reshape(B, s_pad, H)[:, :S]
    return o, lse


def dilated_attn_sample(q, k_new, v_new, k_buf, v_buf, bias_j, win, dil):
    f32 = jnp.float32
    B, L, H, D = q.shape
    wb = k_buf.shape[1]
    nk = win // dil + 1
    kk = jnp.concatenate([k_buf.astype(f32), k_new.astype(f32)], axis=1)
    vv = jnp.concatenate([v_buf.astype(f32), v_new.astype(f32)], axis=1)
    idx = wb + jnp.arange(L)[:, None] - jnp.arange(nk)[None, :] * dil
    valid = idx >= 0
    idx = jnp.maximum(idx, 0)
    kg = kk[:, idx]
    vg = vv[:, idx]
    logits = jnp.einsum('blhd,blkhd->bhlk', q.astype(f32), kg) * (D ** -0.5) + bias_j[:, None, :]
    logits = jnp.where(valid, logits, NEG)
    m = jnp.max(logits, axis=-1, keepdims=True)
    p = jnp.exp(logits - m)
    s = jnp.sum(p, axis=-1, keepdims=True)
    o = jnp.einsum('bhlk,blkhd->blhd', p, vg) / jnp.swapaxes(s, 1, 2)
    lse = jnp.transpose((m + jnp.log(s))[..., 0], (0, 2, 1))
    return o, lse


def dilated_branch(outs, lses, z_b, w_proj_b):
    wts = jax.nn.softmax(jnp.stack(lses, axis=0), axis=0)
    o = jnp.einsum('gblh,gblhd->blhd', wts, jnp.stack(outs, axis=0))
    B, L = o.shape[:2]
    y = o.reshape(B, L, DIL_W) * jax.nn.silu(z_b.astype(jnp.float32))
    return jnp.einsum('ble,ed->bld', y.astype(z_b.dtype), w_proj_b)


def merge_residual(x, p_a, p_b, gate_a, gate_b, w_out, g_post):
    h = jax.nn.sigmoid(gate_a) * p_a + jax.nn.sigmoid(gate_b) * p_b
    out = jnp.einsum('bld,de->ble', h, w_out)
    return x + rms_norm(out, g_post)


def prompt_layer(x, biases, g_pre, w_in, conv_w, a_log, dt_bias, g_head_norm,
                 w_proj_a, w_proj_b, w_out, g_post):
    B, L, _ = x.shape
    qkv_a, z_a, a_a, b_a, qkv_b, z_b, gate_a, gate_b = mixer_inputs(x, g_pre, w_in)
    x_ext = jnp.pad(qkv_a, ((0, 0), (CONV_WIDTH - 1, 0), (0, 0)))
    s0 = jnp.zeros((B, GDN_HEADS, GDN_DK, GDN_DV), jnp.float32)
    p_a, s_fin = gdn_branch(x_ext, s0, z_a, a_a, b_a, conv_w, a_log, dt_bias, g_head_norm, w_proj_a)
    qkv_b = qkv_b.reshape(B, L, N_GROUPS, 3, DIL_HEADS, DIL_HD)
    outs, lses, rows = [], [], []
    for gi, (win, dil) in enumerate(DIL_GROUPS):
        q, k, v = qkv_b[:, :, gi, 0], qkv_b[:, :, gi, 1], qkv_b[:, :, gi, 2]
        o, lse = dilated_attn_prompt(q, k, v, biases[gi], win, dil)
        outs.append(o)
        lses.append(lse)
        keep = min(win, L)
        rows += [k[:, L - keep:], v[:, L - keep:]]
    p_b = dilated_branch(outs, lses, z_b, w_proj_b)
    y = merge_residual(x, p_a, p_b, gate_a, gate_b, w_out, g_post)
    return y, [s_fin, x_ext[:, L:]] + rows


def sample_layer(x, s_gdn, s_conv, k_bufs, v_bufs, biases, g_pre, w_in, conv_w, a_log, dt_bias,
                 g_head_norm, w_proj_a, w_proj_b, w_out, g_post):
    B, L, _ = x.shape
    qkv_a, z_a, a_a, b_a, qkv_b, z_b, gate_a, gate_b = mixer_inputs(x, g_pre, w_in)
    x_ext = jnp.concatenate([s_conv.astype(qkv_a.dtype), qkv_a], axis=1)
    p_a, s_fin = gdn_branch(x_ext, s_gdn, z_a, a_a, b_a, conv_w, a_log, dt_bias, g_head_norm, w_proj_a)
    qkv_b = qkv_b.reshape(B, L, N_GROUPS, 3, DIL_HEADS, DIL_HD)
    outs, lses, rows = [], [], []
    for gi, (win, dil) in enumerate(DIL_GROUPS):
        q, k, v = qkv_b[:, :, gi, 0], qkv_b[:, :, gi, 1], qkv_b[:, :, gi, 2]
        o, lse = dilated_attn_sample(q, k, v, k_bufs[gi], v_bufs[gi], biases[gi], win, dil)
        outs.append(o)
        lses.append(lse)
        rows += [k, v]
    p_b = dilated_branch(outs, lses, z_b, w_proj_b)
    y = merge_residual(x, p_a, p_b, gate_a, gate_b, w_out, g_post)
    return y, [s_fin, x_ext[:, L:]] + rows


def setup_inputs(seed: int = 0) -> dict:
    key = jax.random.key(seed)
    ks = jax.random.split(key, 24)
    f32 = jnp.float32

    def nrm(k, shape, scale):
        return jax.random.normal(k, shape, f32) * scale

    buf = [min(w, PAST_LEN) for w, _ in DIL_GROUPS]
    x_prompt = nrm(ks[0], (BATCH, SEQ, D_MODEL), 1.0)
    x_sample = nrm(ks[1], (DEC_BATCH, DEC_SEQ, D_MODEL), 1.0)
    state_gdn = nrm(ks[2], (DEPTH, DEC_BATCH, GDN_HEADS, GDN_DK, GDN_DV), GDN_DK ** -0.5)
    state_conv = nrm(ks[3], (DEPTH, DEC_BATCH, CONV_WIDTH - 1, CONV_DIM), 1.0)
    cache_k_w128 = nrm(ks[4], (DEPTH, DEC_BATCH, buf[0], DIL_HEADS, DIL_HD), 1.0)
    cache_v_w128 = nrm(ks[5], (DEPTH, DEC_BATCH, buf[0], DIL_HEADS, DIL_HD), 1.0)
    cache_k_w512 = nrm(ks[6], (DEPTH, DEC_BATCH, buf[1], DIL_HEADS, DIL_HD), 1.0)
    cache_v_w512 = nrm(ks[7], (DEPTH, DEC_BATCH, buf[1], DIL_HEADS, DIL_HD), 1.0)
    cache_k_w2048 = nrm(ks[8], (DEPTH, DEC_BATCH, buf[2], DIL_HEADS, DIL_HD), 1.0)
    cache_v_w2048 = nrm(ks[9], (DEPTH, DEC_BATCH, buf[2], DIL_HEADS, DIL_HD), 1.0)
    rel_table = nrm(ks[10], (REL_BUCKETS, N_GROUPS * DIL_HEADS), 0.5)
    g_pre = 1.0 + nrm(ks[11], (DEPTH, D_MODEL), 0.01)
    w_in = nrm(ks[12], (DEPTH, D_MODEL, IN_WIDTH), D_MODEL ** -0.5)
    conv_w = nrm(ks[13], (DEPTH, CONV_DIM, CONV_WIDTH), CONV_WIDTH ** -0.5)
    a_log = jnp.log(jax.random.uniform(ks[14], (DEPTH, GDN_HEADS), f32, 1.0, 16.0))
    dt = jnp.exp(jax.random.uniform(ks[15], (DEPTH, GDN_HEADS), f32, math.log(1e-3), math.log(1e-1)))
    dt_bias = dt + jnp.log(-jnp.expm1(-dt))
    g_head_norm = 1.0 + nrm(ks[16], (DEPTH, GDN_DV), 0.01)
    w_proj_a = nrm(ks[17], (DEPTH, GDN_V, D_MODEL), GDN_V ** -0.5)
    w_proj_b = nrm(ks[18], (DEPTH, DIL_W, D_MODEL), DIL_W ** -0.5)
    w_out = nrm(ks[19], (DEPTH, D_MODEL, D_MODEL), D_MODEL ** -0.5)
    g_post = 1.0 + nrm(ks[20], (DEPTH, D_MODEL), 0.01)
    return {'x_prompt': x_prompt, 'x_sample': x_sample, 'state_gdn': state_gdn,
            'state_conv': state_conv, 'cache_k_w128': cache_k_w128, 'cache_v_w128': cache_v_w128,
            'cache_k_w512': cache_k_w512, 'cache_v_w512': cache_v_w512,
            'cache_k_w2048': cache_k_w2048, 'cache_v_w2048': cache_v_w2048,
            'rel_table': rel_table, 'g_pre': g_pre, 'w_in': w_in, 'conv_w': conv_w,
            'a_log': a_log, 'dt_bias': dt_bias, 'g_head_norm': g_head_norm,
            'w_proj_a': w_proj_a, 'w_proj_b': w_proj_b, 'w_out': w_out, 'g_post': g_post}


def reference(x_prompt, x_sample, state_gdn, state_conv, cache_k_w128, cache_v_w128,
              cache_k_w512, cache_v_w512, cache_k_w2048, cache_v_w2048, rel_table,
              g_pre, w_in, conv_w, a_log, dt_bias, g_head_norm, w_proj_a, w_proj_b, w_out, g_post):
    biases = group_biases(rel_table)
    k_caches = (cache_k_w128, cache_k_w512, cache_k_w2048)
    v_caches = (cache_v_w128, cache_v_w512, cache_v_w2048)
    y_prompt, y_sample = x_prompt, x_sample
    p_lists = [[] for _ in range(2 + 2 * N_GROUPS)]
    s_lists = [[] for _ in range(2 + 2 * N_GROUPS)]
    for l in range(DEPTH):
        lw = (g_pre[l], w_in[l], conv_w[l], a_log[l], dt_bias[l], g_head_norm[l],
              w_proj_a[l], w_proj_b[l], w_out[l], g_post[l])
        y_prompt, p_st = prompt_layer(y_prompt, biases, *lw)
        y_sample, s_st = sample_layer(y_sample, state_gdn[l], state_conv[l],
                                      tuple(c[l] for c in k_caches), tuple(c[l] for c in v_caches),
                                      biases, *lw)
        for lst, st in zip(p_lists, p_st):
            lst.append(st)
        for lst, st in zip(s_lists, s_st):
            lst.append(st)
    p_gdn, p_conv, p_k128, p_v128, p_k512, p_v512, p_k2048, p_v2048 = [jnp.stack(a, 0) for a in p_lists]
    s_gdn, s_conv, s_k128, s_v128, s_k512, s_v512, s_k2048, s_v2048 = [jnp.stack(a, 0) for a in s_lists]
    return (y_prompt, y_sample, p_gdn, p_conv, p_k128, p_v128, p_k512, p_v512, p_k2048, p_v2048,
            s_gdn, s_conv, s_k128, s_v128, s_k512, s_v512, s_k2048, s_v2048)
```

```python
import functools
import math

import jax
import jax.numpy as jnp
from jax import lax
from jax.experimental import pallas as pl
from jax.experimental.pallas import tpu as pltpu

F32 = jnp.float32
BF16 = jnp.bfloat16

D_MODEL = 1024
GDN_HEADS = 8
HEAD_DIM = 128
GDN_QK = GDN_HEADS * HEAD_DIM
CONV_DIM = 3 * GDN_QK
CONV_WIDTH = 4
DIL_GROUPS = ((128, 1), (512, 4), (2048, 16))
N_GROUPS = 3
DIL_HEADS = 4
DIL_W = DIL_HEADS * HEAD_DIM
WIN_BLK = 128
REL_BUCKETS = 32
REL_MAX_DIST = 2048
RMS_EPS = 1e-6
NEG = -1e30

COL_QKV_A = 0
COL_Z_A = 3072
COL_QKV_B = 4096
COL_Z_B = 8704
COL_GATE_A = 9216
COL_GATE_B = 10240
MAIN_WIDTH = 11264
AB_OFFSET = 4096
LANES = 128

VMEM_LIMIT = 56 * 1024 * 1024

GDN_CHUNK = 64
GDN_TILE = 256
ATTN_TILE = 2048


def _silu(x):
    return x * (1.0 / (1.0 + jnp.exp(-x)))


def _sigmoid(x):
    return 1.0 / (1.0 + jnp.exp(-x))


def _softplus(x):
    return jnp.maximum(x, 0.0) + jnp.log(1.0 + jnp.exp(-jnp.abs(x)))


def _dot(a, b):
    return jnp.dot(a.astype(BF16), b.astype(BF16), preferred_element_type=F32)


def _dot_nt(a, b):
    return lax.dot_general(a.astype(BF16), b.astype(BF16), (((1,), (1,)), ((), ())),
                           preferred_element_type=F32)


def _dot_tn(a, b):
    return lax.dot_general(a.astype(BF16), b.astype(BF16), (((0,), (0,)), ((), ())),
                           preferred_element_type=F32)


def _dot_f32(a, b):
    return jnp.dot(a, b, preferred_element_type=F32, precision=lax.Precision.HIGHEST)


def _inproj_kernel(x_ref, g_ref, w_ref, wab_ref, o_ref, ab_ref, xn_ref):
    @pl.when(pl.program_id(1) == 0)
    def _():
        x = x_ref[...]
        ms = jnp.mean(x * x, axis=-1, keepdims=True)
        xn = (x * lax.rsqrt(ms + RMS_EPS) * g_ref[...]).astype(BF16)
        xn_ref[...] = xn
        ab_ref[...] = jnp.dot(xn, wab_ref[...], preferred_element_type=F32)

    o_ref[...] = jnp.dot(xn_ref[...], w_ref[...], preferred_element_type=F32)


def _in_projection(x2d, g_pre, w_main, w_ab, tm, tn):
    m = x2d.shape[0]
    return pl.pallas_call(
        _inproj_kernel,
        grid=(m // tm, MAIN_WIDTH // tn),
        in_specs=[
            pl.BlockSpec((tm, D_MODEL), lambda i, j: (i, 0)),
            pl.BlockSpec((1, D_MODEL), lambda i, j: (0, 0)),
            pl.BlockSpec((D_MODEL, tn), lambda i, j: (0, j)),
            pl.BlockSpec((D_MODEL, LANES), lambda i, j: (0, 0)),
        ],
        out_specs=[
            pl.BlockSpec((tm, tn), lambda i, j: (i, j)),
            pl.BlockSpec((tm, LANES), lambda i, j: (i, 0)),
        ],
        out_shape=[
            jax.ShapeDtypeStruct((m, MAIN_WIDTH), F32),
            jax.ShapeDtypeStruct((m, LANES), F32),
        ],
        scratch_shapes=[pltpu.VMEM((tm, D_MODEL), BF16)],
        compiler_params=pltpu.CompilerParams(
            dimension_semantics=("parallel", "arbitrary"), vmem_limit_bytes=VMEM_LIMIT),
        name="in_projection",
    )(x2d, g_pre, w_main, w_ab)


def _unit_lower_inverse(a):
    c = a.shape[0]
    e = -a
    x = _dot(a, a)
    steps = int(math.log2(c)) - 1
    for s in range(steps):
        if s < steps - 1:
            both = _dot(jnp.concatenate([e, x], axis=0), x)
            e = e + x + both[:c]
            x = both[c:]
        else:
            e = e + x + _dot(e, x)
    return e


def _gdn_prompt_kernel(qkv_ref, z_ref, ab_ref, gate_ref, convw_ref, alog_ref, dtb_ref, ghn_ref,
                       wproj_ref, ha_ref, sfin_ref, xpad_ref, qkvn_ref, o_ref, s_ref):
    tl = GDN_TILE
    c = GDN_CHUNK
    step = pl.program_id(1)

    @pl.when(step == 0)
    def _():
        xpad_ref[0:8, :] = jnp.zeros((8, CONV_DIM), F32)
        s_ref[...] = jnp.zeros_like(s_ref)

    xpad_ref[8:8 + tl, :] = qkv_ref[...]
    for blk in range(CONV_DIM // LANES):
        cols = slice(blk * LANES, (blk + 1) * LANES)
        acc = xpad_ref[8:8 + tl, cols] * convw_ref[3:4, cols]
        for w in range(CONV_WIDTH - 1):
            acc = acc + xpad_ref[5 + w:5 + w + tl, cols] * convw_ref[w:w + 1, cols]
        y = _silu(acc)
        if blk < 2 * GDN_HEADS:
            y = y * lax.rsqrt(jnp.sum(y * y, axis=-1, keepdims=True) + RMS_EPS)
            if blk < GDN_HEADS:
                y = y * (HEAD_DIM ** -0.5)
        qkvn_ref[:, cols] = y
    xpad_ref[5:8, :] = xpad_ref[tl + 5:tl + 8, :]

    ab = ab_ref[...]
    g_full = -jnp.exp(alog_ref[...]) * _softplus(ab + dtb_ref[...])
    beta_full = _sigmoid(ab)
    row = lax.broadcasted_iota(jnp.int32, (tl, tl), 0)
    col = lax.broadcasted_iota(jnp.int32, (tl, tl), 1)
    same_chunk = (row // c) == (col // c)
    tri = jnp.where(same_chunk & (col <= row), 1.0, 0.0).astype(F32)
    g_cum = _dot_f32(tri, g_full)
    g_cum_t = g_cum.T
    last = jnp.where(same_chunk & ((col % c) == c - 1), 1.0, 0.0).astype(F32)
    g_last = _dot_f32(last, g_cum)
    exp_g = jnp.exp(g_cum)
    exp_rest = jnp.exp(g_last - g_cum)
    exp_last = jnp.exp(g_last)

    ci = lax.broadcasted_iota(jnp.int32, (c, c), 0)
    cj = lax.broadcasted_iota(jnp.int32, (c, c), 1)
    causal = cj <= ci
    strict = cj < ci

    for ch in range(tl // c):
        rows = slice(ch * c, (ch + 1) * c)
        for h in range(GDN_HEADS):
            qc = qkvn_ref[rows, h * LANES:(h + 1) * LANES]
            kc = qkvn_ref[rows, GDN_QK + h * LANES:GDN_QK + (h + 1) * LANES]
            vc = qkvn_ref[rows, 2 * GDN_QK + h * LANES:2 * GDN_QK + (h + 1) * LANES]
            beta = beta_full[rows, GDN_HEADS + h:GDN_HEADS + h + 1]
            gi = g_cum[rows, h:h + 1]
            gj = g_cum_t[h:h + 1, rows]
            eg = exp_g[rows, h:h + 1]
            er = exp_rest[rows, h:h + 1]
            el = exp_last[ch * c:ch * c + 1, h:h + 1]
            decay = jnp.exp(jnp.where(causal, gi - gj, NEG))
            kb = kc * beta
            a_mat = jnp.where(strict, _dot_nt(kb, kc) * decay, 0.0)
            qk = _dot_nt(qc, kc) * decay
            e = _unit_lower_inverse(a_mat)
            vb = vc * beta
            kbe = kb * eg
            u = vb + _dot(e, vb)
            w = kbe + _dot(e, kbe)
            s = s_ref[h]
            v_new = u - _dot(w, s)
            o = _dot(qc * eg, s) + _dot(qk, v_new)
            s_ref[h] = s * el + _dot_tn(kc * er, v_new)
            o_ref[rows, h * LANES:(h + 1) * LANES] = o

    for h in range(GDN_HEADS):
        cols = slice(h * LANES, (h + 1) * LANES)
        o = o_ref[:, cols]
        o = o * lax.rsqrt(jnp.mean(o * o, axis=-1, keepdims=True) + RMS_EPS) * ghn_ref[...]
        o_ref[:, cols] = o * _silu(z_ref[:, cols])
    p_a = jnp.dot(o_ref[...].astype(BF16), wproj_ref[...], preferred_element_type=F32)
    ha_ref[...] = _sigmoid(gate_ref[...]) * p_a

    @pl.when(step == pl.num_programs(1) - 1)
    def _():
        sfin_ref[...] = s_ref[...]


def _gdn_prompt(proj, ab, conv_wt, alog_pad, dtb_pad, ghn, w_proj_a, batch, seq):
    tl = GDN_TILE
    nl = seq // tl
    return pl.pallas_call(
        _gdn_prompt_kernel,
        grid=(batch, nl),
        in_specs=[
            pl.BlockSpec((tl, CONV_DIM), lambda b, l: (b * nl + l, COL_QKV_A // CONV_DIM)),
            pl.BlockSpec((tl, GDN_QK), lambda b, l: (b * nl + l, COL_Z_A // GDN_QK)),
            pl.BlockSpec((tl, LANES), lambda b, l: (b * nl + l, 0)),
            pl.BlockSpec((tl, D_MODEL), lambda b, l: (b * nl + l, COL_GATE_A // D_MODEL)),
            pl.BlockSpec((CONV_WIDTH, CONV_DIM), lambda b, l: (0, 0)),
            pl.BlockSpec((1, LANES), lambda b, l: (0, 0)),
            pl.BlockSpec((1, LANES), lambda b, l: (0, 0)),
            pl.BlockSpec((1, LANES), lambda b, l: (0, 0)),
            pl.BlockSpec((GDN_QK, D_MODEL), lambda b, l: (0, 0)),
        ],
        out_specs=[
            pl.BlockSpec((tl, D_MODEL), lambda b, l: (b * nl + l, 0)),
            pl.BlockSpec((None, GDN_HEADS, HEAD_DIM, HEAD_DIM), lambda b, l: (b, 0, 0, 0)),
        ],
        out_shape=[
            jax.ShapeDtypeStruct((batch * seq, D_MODEL), F32),
            jax.ShapeDtypeStruct((batch, GDN_HEADS, HEAD_DIM, HEAD_DIM), F32),
        ],
        scratch_shapes=[
            pltpu.VMEM((tl + 8, CONV_DIM), F32),
            pltpu.VMEM((tl, CONV_DIM), F32),
            pltpu.VMEM((tl, GDN_QK), F32),
            pltpu.VMEM((GDN_HEADS, HEAD_DIM, HEAD_DIM), F32),
        ],
        compiler_params=pltpu.CompilerParams(
            dimension_semantics=("parallel", "arbitrary"), vmem_limit_bytes=VMEM_LIMIT),
        name="gdn_prompt",
    )(proj, proj, ab, proj, conv_wt, alog_pad, dtb_pad, ghn, w_proj_a)


def _gdn_sample_kernel(sc_q_ref, sc_k_ref, sc_v_ref, nq_ref, nk_ref, nv_ref, cwq_ref, cwk_ref, cwv_ref,
                       ab_ref, alog_ref, dtb_ref, s_ref, o_ref, snew_ref):
    h = pl.program_id(0)
    nb = nq_ref.shape[0]

    def conv_silu(sc_ref, new_ref, cw_ref):
        acc = new_ref[...] * cw_ref[3:4, :]
        for w in range(CONV_WIDTH - 1):
            acc = acc + sc_ref[w] * cw_ref[w:w + 1, :]
        return _silu(acc)

    q = conv_silu(sc_q_ref, nq_ref, cwq_ref)
    k = conv_silu(sc_k_ref, nk_ref, cwk_ref)
    v = conv_silu(sc_v_ref, nv_ref, cwv_ref)
    q = q * lax.rsqrt(jnp.sum(q * q, axis=-1, keepdims=True) + RMS_EPS) * (HEAD_DIM ** -0.5)
    k = k * lax.rsqrt(jnp.sum(k * k, axis=-1, keepdims=True) + RMS_EPS)

    ab = ab_ref[...]
    lane = lax.broadcasted_iota(jnp.int32, ab.shape, 1)
    g_full = -jnp.exp(alog_ref[...]) * _softplus(ab + dtb_ref[...])
    g = jnp.sum(jnp.where(lane == h, g_full, 0.0), axis=-1, keepdims=True)
    beta = jnp.sum(jnp.where(lane == h + GDN_HEADS, _sigmoid(ab), 0.0), axis=-1, keepdims=True)
    eg = jnp.exp(g)

    eye = (lax.broadcasted_iota(jnp.int32, (HEAD_DIM, HEAD_DIM), 0)
           == lax.broadcasted_iota(jnp.int32, (HEAD_DIM, HEAD_DIM), 1))

    def as_column(row):
        return jnp.sum(jnp.where(eye, row, 0.0), axis=1, keepdims=True)

    for b in range(nb):
        s = s_ref[b]
        k_col = as_column(k[b:b + 1, :])
        q_col = as_column(q[b:b + 1, :])
        e = eg[b:b + 1, :]
        ks = jnp.sum(k_col * s, axis=0, keepdims=True)
        v_new = beta[b:b + 1, :] * (v[b:b + 1, :] - e * ks)
        s_new = s * e + k_col * v_new
        snew_ref[b] = s_new
        o_ref[b:b + 1, :] = jnp.sum(q_col * s_new, axis=0, keepdims=True)


def _gdn_sample(state_conv_t, proj, ab, conv_wt, alog_pad, dtb_pad, state, nb):
    batch = proj.shape[0]
    hq = GDN_QK // LANES
    return pl.pallas_call(
        _gdn_sample_kernel,
        grid=(GDN_HEADS, batch // nb),
        in_specs=[
            pl.BlockSpec((CONV_WIDTH - 1, nb, LANES), lambda h, i: (0, i, h)),
            pl.BlockSpec((CONV_WIDTH - 1, nb, LANES), lambda h, i: (0, i, hq + h)),
            pl.BlockSpec((CONV_WIDTH - 1, nb, LANES), lambda h, i: (0, i, 2 * hq + h)),
            pl.BlockSpec((nb, LANES), lambda h, i: (i, h)),
            pl.BlockSpec((nb, LANES), lambda h, i: (i, hq + h)),
            pl.BlockSpec((nb, LANES), lambda h, i: (i, 2 * hq + h)),
            pl.BlockSpec((CONV_WIDTH, LANES), lambda h, i: (0, h)),
            pl.BlockSpec((CONV_WIDTH, LANES), lambda h, i: (0, hq + h)),
            pl.BlockSpec((CONV_WIDTH, LANES), lambda h, i: (0, 2 * hq + h)),
            pl.BlockSpec((nb, LANES), lambda h, i: (i, 0)),
            pl.BlockSpec((1, LANES), lambda h, i: (0, 0)),
            pl.BlockSpec((1, LANES), lambda h, i: (0, 0)),
            pl.BlockSpec((nb, None, HEAD_DIM, HEAD_DIM), lambda h, i: (i, h, 0, 0)),
        ],
        out_specs=[
            pl.BlockSpec((nb, LANES), lambda h, i: (i, h)),
            pl.BlockSpec((nb, None, HEAD_DIM, HEAD_DIM), lambda h, i: (i, h, 0, 0)),
        ],
        out_shape=[
            jax.ShapeDtypeStruct((batch, GDN_QK), F32),
            jax.ShapeDtypeStruct(state.shape, F32),
        ],
        compiler_params=pltpu.CompilerParams(
            dimension_semantics=("parallel", "parallel"), vmem_limit_bytes=VMEM_LIMIT),
        name="gdn_sample",
    )(state_conv_t, state_conv_t, state_conv_t, proj, proj, proj, conv_wt, conv_wt, conv_wt,
      ab, alog_pad, dtb_pad, state)


def _gdn_sample_out_kernel(o_ref, z_ref, gate_ref, ghn_ref, wproj_ref, ha_ref, y_ref):
    for h in range(GDN_HEADS):
        cols = slice(h * LANES, (h + 1) * LANES)
        o = o_ref[:, cols]
        o = o * lax.rsqrt(jnp.mean(o * o, axis=-1, keepdims=True) + RMS_EPS) * ghn_ref[...]
        y_ref[:, cols] = o * _silu(z_ref[:, cols])
    p_a = jnp.dot(y_ref[...].astype(BF16), wproj_ref[...], preferred_element_type=F32)
    ha_ref[...] = _sigmoid(gate_ref[...]) * p_a


def _gdn_sample_out(o, proj, ghn, w_proj_a):
    batch = o.shape[0]
    return pl.pallas_call(
        _gdn_sample_out_kernel,
        grid=(1,),
        in_specs=[
            pl.BlockSpec((batch, GDN_QK), lambda i: (0, 0)),
            pl.BlockSpec((batch, GDN_QK), lambda i: (0, COL_Z_A // GDN_QK)),
            pl.BlockSpec((batch, D_MODEL), lambda i: (0, COL_GATE_A // D_MODEL)),
            pl.BlockSpec((1, LANES), lambda i: (0, 0)),
            pl.BlockSpec((GDN_QK, D_MODEL), lambda i: (0, 0)),
        ],
        out_specs=pl.BlockSpec((batch, D_MODEL), lambda i: (0, 0)),
        out_shape=jax.ShapeDtypeStruct((batch, D_MODEL), F32),
        scratch_shapes=[pltpu.VMEM((batch, GDN_QK), F32)],
        compiler_params=pltpu.CompilerParams(vmem_limit_bytes=VMEM_LIMIT),
        name="gdn_sample_out",
    )(o, proj, proj, ghn, w_proj_a)


def _attn_prompt_kernel(*refs):
    ins = refs[:5 * N_GROUPS]
    bias_ref = refs[5 * N_GROUPS]
    o_ref = refs[5 * N_GROUPS + 1]
    acc_ref, m_ref, l_ref = refs[5 * N_GROUPS + 2:]
    tq = ATTN_TILE
    first_tile = pl.program_id(1) == 0
    scale = HEAD_DIM ** -0.5
    lane2 = lax.broadcasted_iota(jnp.int32, (1, 2 * WIN_BLK), 1)
    start_mask = jnp.where(first_tile & (lane2 < WIN_BLK), NEG, 0.0).astype(F32)

    for gi, (_, dil) in enumerate(DIL_GROUPS):
        q_ref, kp_ref, kc_ref, vp_ref, vc_ref = ins[5 * gi:5 * gi + 5]
        bias = bias_ref[gi]
        n_blocks = tq // (dil * WIN_BLK)

        def unit(r, j, q_ref=q_ref, kp_ref=kp_ref, kc_ref=kc_ref, vp_ref=vp_ref, vc_ref=vc_ref,
                 bias=bias, dil=dil, gi=gi):
            base = dil * WIN_BLK * j + r
            q_rows = pl.ds(base, WIN_BLK, stride=dil) if dil > 1 else pl.ds(base, WIN_BLK)
            q = q_ref[q_rows, :]
            if j == 0:
                rows = pl.ds(r, WIN_BLK, stride=dil) if dil > 1 else pl.ds(r, WIN_BLK)
                kk = jnp.concatenate([kp_ref[rows, :], kc_ref[rows, :]], axis=0)
                vv = jnp.concatenate([vp_ref[rows, :], vc_ref[rows, :]], axis=0)
            else:
                pbase = dil * WIN_BLK * (j - 1) + r
                rows = (pl.ds(pbase, 2 * WIN_BLK, stride=dil) if dil > 1
                        else pl.ds(pbase, 2 * WIN_BLK))
                kk = kc_ref[rows, :]
                vv = vc_ref[rows, :]
            s = _dot_nt(q, kk) * scale + bias
            if j == 0:
                s = s + start_mask
            m = jnp.max(s, axis=-1, keepdims=True)
            p = jnp.exp(s - m)
            l = jnp.sum(p, axis=-1, keepdims=True)
            acc = _dot(p, vv)
            acc_ref[gi, q_rows, :] = acc
            m_ref[gi, q_rows, :] = jnp.broadcast_to(m, (WIN_BLK, LANES))
            l_ref[gi, q_rows, :] = jnp.broadcast_to(l, (WIN_BLK, LANES))

        for j in range(n_blocks):
            if dil == 1:
                unit(0, j)
            else:
                def body(r, carry, j=j, unit=unit):
                    unit(r, j)
                    return carry
                lax.fori_loop(0, dil, body, 0)

    m_all = jnp.maximum(jnp.maximum(m_ref[0], m_ref[1]), m_ref[2])
    num = jnp.zeros((tq, LANES), F32)
    den = jnp.zeros((tq, LANES), F32)
    for gi in range(N_GROUPS):
        wgt = jnp.exp(m_ref[gi] - m_all)
        num = num + wgt * acc_ref[gi]
        den = den + wgt * l_ref[gi]
    o_ref[...] = num / den


def _attn_prompt(proj, bias_tiles, batch, seq):
    tq = ATTN_TILE
    nt = seq // tq
    in_specs = []
    args = []
    for gi, (_, dil) in enumerate(DIL_GROUPS):
        span = dil * WIN_BLK
        cq = (COL_QKV_B + gi * 3 * DIL_W) // LANES
        ck = cq + DIL_HEADS
        cv = ck + DIL_HEADS

        def cur(col):
            return pl.BlockSpec((tq, LANES), lambda b, i, h, col=col: (b * nt + i, col + h))

        def prev(col, span=span):
            per = tq // span
            return pl.BlockSpec(
                (span, LANES),
                lambda b, i, h, col=col, per=per: (jnp.maximum((b * nt + i) * per - 1, 0), col + h))

        in_specs += [cur(cq), prev(ck), cur(ck), prev(cv), cur(cv)]
        args += [proj] * 5
    in_specs.append(pl.BlockSpec((N_GROUPS, None, WIN_BLK, 2 * WIN_BLK), lambda b, i, h: (0, h, 0, 0)))
    args.append(bias_tiles)
    return pl.pallas_call(
        _attn_prompt_kernel,
        grid=(batch, nt, DIL_HEADS),
        in_specs=in_specs,
        out_specs=pl.BlockSpec((tq, LANES), lambda b, i, h: (b * nt + i, h)),
        out_shape=jax.ShapeDtypeStruct((batch * seq, DIL_W), F32),
        scratch_shapes=[pltpu.VMEM((N_GROUPS, tq, LANES), F32)] * 3,
        compiler_params=pltpu.CompilerParams(
            dimension_semantics=("parallel", "parallel", "parallel"), vmem_limit_bytes=VMEM_LIMIT),
        name="attn_prompt",
    )(*args)


def _attn_sample_kernel(*refs):
    q_refs = refs[0:N_GROUPS]
    kn_refs = refs[N_GROUPS:2 * N_GROUPS]
    vn_refs = refs[2 * N_GROUPS:3 * N_GROUPS]
    kc_refs = refs[3 * N_GROUPS:4 * N_GROUPS]
    vc_refs = refs[4 * N_GROUPS:5 * N_GROUPS]
    bias0_ref, bias_ref, o_ref = refs[5 * N_GROUPS:5 * N_GROUPS + 3]
    m_ref, l_ref, acc_ref = refs[5 * N_GROUPS + 3:]
    t = pl.program_id(0)
    scale = HEAD_DIM ** -0.5

    @pl.when(t == 0)
    def _():
        for gi in range(N_GROUPS):
            s = jnp.sum(q_refs[gi][...] * kn_refs[gi][...], axis=-1, keepdims=True) * scale + bias0_ref[gi]
            m_ref[gi] = s
            l_ref[gi] = jnp.ones_like(s)
            acc_ref[gi] = vn_refs[gi][...]

    for gi in range(N_GROUPS):
        s = jnp.sum(q_refs[gi][...] * kc_refs[gi][...], axis=-1, keepdims=True) * scale + bias_ref[gi]
        m_old = m_ref[gi]
        m_new = jnp.maximum(m_old, s)
        alpha = jnp.exp(m_old - m_new)
        p = jnp.exp(s - m_new)
        l_ref[gi] = alpha * l_ref[gi] + p
        acc_ref[gi] = alpha * acc_ref[gi] + p * vc_refs[gi][...]
        m_ref[gi] = m_new

    @pl.when(t == pl.num_programs(0) - 1)
    def _():
        m_all = jnp.maximum(jnp.maximum(m_ref[0], m_ref[1]), m_ref[2])
        num = jnp.zeros(acc_ref.shape[1:], F32)
        den = jnp.zeros(m_ref.shape[1:], F32)
        for gi in range(N_GROUPS):
            wgt = jnp.exp(m_ref[gi] - m_all)
            num = num + wgt * acc_ref[gi]
            den = den + wgt * l_ref[gi]
        o_ref[...] = num / den


def _attn_sample(q_g, kn_g, vn_g, k_caches, v_caches, bias0, bias_steps):
    batch = q_g[0].shape[0]
    small = pl.BlockSpec((batch, DIL_HEADS, HEAD_DIM), lambda t: (0, 0, 0))
    in_specs = [small] * (3 * N_GROUPS)
    for _ in range(2):
        for _, dil in DIL_GROUPS:
            in_specs.append(pl.BlockSpec((None, batch, None, DIL_HEADS, HEAD_DIM),
                                         lambda t, dil=dil: (0, 0, t * dil, 0, 0)))
    in_specs.append(pl.BlockSpec((N_GROUPS, 1, DIL_HEADS, 1), lambda t: (0, 0, 0, 0)))
    in_specs.append(pl.BlockSpec((None, N_GROUPS, 1, DIL_HEADS, 1), lambda t: (t, 0, 0, 0, 0)))
    return pl.pallas_call(
        _attn_sample_kernel,
        grid=(WIN_BLK,),
        in_specs=in_specs,
        out_specs=pl.BlockSpec((batch, DIL_HEADS, HEAD_DIM), lambda t: (0, 0, 0)),
        out_shape=jax.ShapeDtypeStruct((batch, DIL_HEADS, HEAD_DIM), F32),
        scratch_shapes=[
            pltpu.VMEM((N_GROUPS, batch, DIL_HEADS, 1), F32),
            pltpu.VMEM((N_GROUPS, batch, DIL_HEADS, 1), F32),
            pltpu.VMEM((N_GROUPS, batch, DIL_HEADS, HEAD_DIM), F32),
        ],
        compiler_params=pltpu.CompilerParams(
            dimension_semantics=("arbitrary",), vmem_limit_bytes=VMEM_LIMIT),
        name="attn_sample",
    )(*q_g, *kn_g, *vn_g, *k_caches, *v_caches, bias0, bias_steps)


def _output_kernel(ob_ref, zb_ref, gateb_ref, ha_ref, x_ref, wpb_ref, wout_ref, gpost_ref, y_ref):
    yb = (ob_ref[...] * _silu(zb_ref[...])).astype(BF16)
    p_b = jnp.dot(yb, wpb_ref[...], preferred_element_type=F32)
    hid = ha_ref[...] + _sigmoid(gateb_ref[...]) * p_b
    out = jnp.dot(hid.astype(BF16), wout_ref[...], preferred_element_type=F32)
    ms = jnp.mean(out * out, axis=-1, keepdims=True)
    y_ref[...] = x_ref[...] + out * lax.rsqrt(ms + RMS_EPS) * gpost_ref[...]


def _output(o_b, proj, h_a, x2d, w_proj_b, w_out, g_post, tm):
    m = x2d.shape[0]
    return pl.pallas_call(
        _output_kernel,
        grid=(m // tm,),
        in_specs=[
            pl.BlockSpec((tm, DIL_W), lambda i: (i, 0)),
            pl.BlockSpec((tm, DIL_W), lambda i: (i, COL_Z_B // DIL_W)),
            pl.BlockSpec((tm, D_MODEL), lambda i: (i, COL_GATE_B // D_MODEL)),
            pl.BlockSpec((tm, D_MODEL), lambda i: (i, 0)),
            pl.BlockSpec((tm, D_MODEL), lambda i: (i, 0)),
            pl.BlockSpec((DIL_W, D_MODEL), lambda i: (0, 0)),
            pl.BlockSpec((D_MODEL, D_MODEL), lambda i: (0, 0)),
            pl.BlockSpec((1, D_MODEL), lambda i: (0, 0)),
        ],
        out_specs=pl.BlockSpec((tm, D_MODEL), lambda i: (i, 0)),
        out_shape=jax.ShapeDtypeStruct((m, D_MODEL), F32),
        compiler_params=pltpu.CompilerParams(
            dimension_semantics=("parallel",), vmem_limit_bytes=VMEM_LIMIT),
        name="output",
    )(o_b, proj, proj, h_a, x2d, w_proj_b, w_out, g_post)


def _rel_bucket(dist):
    max_exact = REL_BUCKETS // 2
    d = jnp.maximum(dist, 1).astype(F32)
    large = max_exact + (jnp.log(d / max_exact) / math.log(REL_MAX_DIST / max_exact)
                         * (REL_BUCKETS - max_exact)).astype(jnp.int32)
    large = jnp.minimum(large, REL_BUCKETS - 1)
    return jnp.where(dist < max_exact, dist, large)


def _group_biases(rel_table):
    out = []
    for gi, (win, dil) in enumerate(DIL_GROUPS):
        dist = jnp.arange(win // dil + 1, dtype=jnp.int32) * dil
        b = rel_table[_rel_bucket(dist)][:, gi * DIL_HEADS:(gi + 1) * DIL_HEADS]
        out.append(b.T.astype(F32))
    return out


def _bias_tiles(biases):
    i = jnp.arange(WIN_BLK)[:, None]
    kx = jnp.arange(2 * WIN_BLK)[None, :]
    j = WIN_BLK + i - kx
    band = (j >= 0) & (j <= WIN_BLK)
    tiles = [jnp.where(band[None], b[:, jnp.clip(j, 0, WIN_BLK)], NEG) for b in biases]
    return jnp.stack(tiles, axis=0).astype(F32)


def _layer_weights(w_in, conv_w, a_log, dt_bias, g_head_norm, w_proj_a, w_proj_b, w_out):
    w_main = jnp.concatenate([w_in[:, :AB_OFFSET], w_in[:, AB_OFFSET + 2 * GDN_HEADS:]], axis=1).astype(BF16)
    w_ab = jnp.pad(w_in[:, AB_OFFSET:AB_OFFSET + 2 * GDN_HEADS],
                   ((0, 0), (0, LANES - 2 * GDN_HEADS))).astype(BF16)
    pad = (0, LANES - GDN_HEADS)
    return dict(
        w_main=w_main, w_ab=w_ab, conv_wt=conv_w.T.astype(F32),
        alog_pad=jnp.pad(a_log.astype(F32), pad)[None, :],
        dtb_pad=jnp.pad(dt_bias.astype(F32), pad)[None, :],
        ghn=g_head_norm.astype(F32)[None, :],
        w_proj_a=w_proj_a.astype(BF16), w_proj_b=w_proj_b.astype(BF16), w_out=w_out.astype(BF16))


def _split_heads(a2d, col, rows_per_batch, batch, keep):
    a = a2d.reshape(batch, rows_per_batch, -1)[:, rows_per_batch - keep:, col:col + DIL_W]
    return a.reshape(batch, keep, DIL_HEADS, HEAD_DIM)


def kernel(x_prompt, x_sample, state_gdn, state_conv, cache_k_w128, cache_v_w128, cache_k_w512,
           cache_v_w512, cache_k_w2048, cache_v_w2048, rel_table, g_pre, w_in, conv_w, a_log, dt_bias,
           g_head_norm, w_proj_a, w_proj_b, w_out, g_post):
    batch, seq, _ = x_prompt.shape
    dec_batch = x_sample.shape[0]
    k_caches = (cache_k_w128, cache_k_w512, cache_k_w2048)
    v_caches = (cache_v_w128, cache_v_w512, cache_v_w2048)
    biases = _group_biases(rel_table)
    bias_tiles = _bias_tiles(biases)
    lw = _layer_weights(w_in[0], conv_w[0], a_log[0], dt_bias[0], g_head_norm[0], w_proj_a[0],
                        w_proj_b[0], w_out[0])
    g_pre_l = g_pre[0][None, :]
    g_post_l = g_post[0][None, :]

    xp = x_prompt.reshape(batch * seq, D_MODEL)
    proj, ab = _in_projection(xp, g_pre_l, lw["w_main"], lw["w_ab"], tm=1024, tn=1408)
    h_a, p_gdn = _gdn_prompt(proj, ab, lw["conv_wt"], lw["alog_pad"], lw["dtb_pad"], lw["ghn"],
                             lw["w_proj_a"], batch, seq)
    o_b = _attn_prompt(proj, bias_tiles, batch, seq)
    y_prompt = _output(o_b, proj, h_a, xp, lw["w_proj_b"], lw["w_out"], g_post_l, tm=512)
    y_prompt = y_prompt.reshape(batch, seq, D_MODEL)
    p_conv = proj.reshape(batch, seq, MAIN_WIDTH)[:, seq - (CONV_WIDTH - 1):, :CONV_DIM]
    p_rows = []
    for gi, (win, _) in enumerate(DIL_GROUPS):
        keep = min(win, seq)
        ck = COL_QKV_B + gi * 3 * DIL_W + DIL_W
        p_rows += [_split_heads(proj, ck, seq, batch, keep), _split_heads(proj, ck + DIL_W, seq, batch, keep)]

    xs = x_sample.reshape(dec_batch, D_MODEL)
    proj_s, ab_s = _in_projection(xs, g_pre_l, lw["w_main"], lw["w_ab"], tm=dec_batch, tn=1408)
    sc_t = jnp.transpose(state_conv[0], (1, 0, 2))
    o_s, s_gdn = _gdn_sample(sc_t, proj_s, ab_s, lw["conv_wt"], lw["alog_pad"], lw["dtb_pad"],
                             state_gdn[0], nb=32)
    h_a_s = _gdn_sample_out(o_s, proj_s, lw["ghn"], lw["w_proj_a"])
    q_g, kn_g, vn_g = [], [], []
    for gi in range(N_GROUPS):
        cq = COL_QKV_B + gi * 3 * DIL_W
        q_g.append(proj_s[:, cq:cq + DIL_W].reshape(dec_batch, DIL_HEADS, HEAD_DIM))
        kn_g.append(proj_s[:, cq + DIL_W:cq + 2 * DIL_W].reshape(dec_batch, DIL_HEADS, HEAD_DIM))
        vn_g.append(proj_s[:, cq + 2 * DIL_W:cq + 3 * DIL_W].reshape(dec_batch, DIL_HEADS, HEAD_DIM))
    bias_all = jnp.stack(biases, axis=0)
    bias0 = bias_all[:, None, :, 0:1]
    bias_steps = jnp.transpose(bias_all[:, :, :0:-1], (2, 0, 1))[:, :, None, :, None]
    o_b_s = _attn_sample(q_g, kn_g, vn_g, k_caches, v_caches, bias0, bias_steps)
    y_sample = _output(o_b_s.reshape(dec_batch, DIL_W), proj_s, h_a_s, xs, lw["w_proj_b"], lw["w_out"],
                       g_post_l, tm=dec_batch)
    y_sample = y_sample.reshape(dec_batch, 1, D_MODEL)
    s_conv = jnp.concatenate([state_conv[0][:, 1:, :], proj_s[:, None, :CONV_DIM]], axis=1)
    s_rows = []
    for gi in range(N_GROUPS):
        s_rows += [kn_g[gi][:, None], vn_g[gi][:, None]]

    return (y_prompt, y_sample, p_gdn[None], p_conv[None], *[r[None] for r in p_rows],
            s_gdn[None], s_conv[None], *[r[None] for r in s_rows])
```

```python
import functools
import math

import jax
import jax.numpy as jnp
from jax import lax
from jax.experimental import pallas as pl
from jax.experimental.pallas import tpu as pltpu

F32 = jnp.float32
BF16 = jnp.bfloat16

D_MODEL = 1024
GDN_HEADS = 8
HEAD_DIM = 128
GDN_QK = GDN_HEADS * HEAD_DIM
CONV_DIM = 3 * GDN_QK
CONV_WIDTH = 4
DIL_GROUPS = ((128, 1), (512, 4), (2048, 16))
N_GROUPS = 3
DIL_HEADS = 4
DIL_W = DIL_HEADS * HEAD_DIM
WIN_BLK = 128
REL_BUCKETS = 32
REL_MAX_DIST = 2048
RMS_EPS = 1e-6
NEG = -1e30

COL_QKV_A = 0
COL_Z_A = 3072
COL_QKV_B = 4096
COL_Z_B = 8704
COL_GATE_A = 9216
COL_GATE_B = 10240
MAIN_WIDTH = 11264
AB_OFFSET = 4096
LANES = 128

VMEM_LIMIT = 56 * 1024 * 1024

GDN_CHUNK = 64
GDN_SEQS = 4
ATTN_TILE = 2048


def _silu(x):
    return x * (1.0 / (1.0 + jnp.exp(-x)))


def _sigmoid(x):
    return 1.0 / (1.0 + jnp.exp(-x))


def _softplus(x):
    return jnp.maximum(x, 0.0) + jnp.log(1.0 + jnp.exp(-jnp.abs(x)))


def _dot(a, b):
    return jnp.dot(a.astype(BF16), b.astype(BF16), preferred_element_type=F32)


def _dot_nt(a, b):
    return lax.dot_general(a.astype(BF16), b.astype(BF16), (((1,), (1,)), ((), ())),
                           preferred_element_type=F32)


def _dot_tn(a, b):
    return lax.dot_general(a.astype(BF16), b.astype(BF16), (((0,), (0,)), ((), ())),
                           preferred_element_type=F32)


def _dot_f32(a, b):
    return jnp.dot(a, b, preferred_element_type=F32, precision=lax.Precision.HIGHEST)


def _inproj_kernel(x_ref, g_ref, w_ref, wab_ref, o_ref, ab_ref, xn_ref):
    @pl.when(pl.program_id(1) == 0)
    def _():
        x = x_ref[...]
        ms = jnp.mean(x * x, axis=-1, keepdims=True)
        xn = (x * lax.rsqrt(ms + RMS_EPS) * g_ref[...]).astype(BF16)
        xn_ref[...] = xn
        ab_ref[...] = jnp.dot(xn, wab_ref[...], preferred_element_type=F32)

    o_ref[...] = jnp.dot(xn_ref[...], w_ref[...], preferred_element_type=F32)


def _in_projection(x2d, g_pre, w_main, w_ab, tm, tn):
    m = x2d.shape[0]
    return pl.pallas_call(
        _inproj_kernel,
        grid=(m // tm, MAIN_WIDTH // tn),
        in_specs=[
            pl.BlockSpec((tm, D_MODEL), lambda i, j: (i, 0)),
            pl.BlockSpec((1, D_MODEL), lambda i, j: (0, 0)),
            pl.BlockSpec((D_MODEL, tn), lambda i, j: (0, j)),
            pl.BlockSpec((D_MODEL, LANES), lambda i, j: (0, 0)),
        ],
        out_specs=[
            pl.BlockSpec((tm, tn), lambda i, j: (i, j)),
            pl.BlockSpec((tm, LANES), lambda i, j: (i, 0)),
        ],
        out_shape=[
            jax.ShapeDtypeStruct((m, MAIN_WIDTH), F32),
            jax.ShapeDtypeStruct((m, LANES), F32),
        ],
        scratch_shapes=[pltpu.VMEM((tm, D_MODEL), BF16)],
        compiler_params=pltpu.CompilerParams(
            dimension_semantics=("parallel", "arbitrary"), vmem_limit_bytes=VMEM_LIMIT),
        name="in_projection",
    )(x2d, g_pre, w_main, w_ab)


GROUP_HEADS = 4


def _block_diag_rhs(x4, mask):
    return jnp.concatenate([x4] * GROUP_HEADS, axis=0) * mask


def _unit_lower_inverse4(a4s, mask):
    c = a4s[0].shape[0]

    def bd_dot(lhs, x4):
        return jnp.dot(lhs.astype(BF16), _block_diag_rhs(x4.astype(BF16), mask), preferred_element_type=F32)

    es = [-a4 for a4 in a4s]
    xs = [bd_dot(a4, a4) for a4 in a4s]
    steps = int(math.log2(c)) - 1
    for s in range(steps):
        if s < steps - 1:
            boths = [bd_dot(jnp.concatenate([e, x], axis=0), x) for e, x in zip(es, xs)]
            es = [e + x + both[:c] for e, x, both in zip(es, xs, boths)]
            xs = [both[c:] for both in boths]
        else:
            es = [e + x + bd_dot(e, x) for e, x in zip(es, xs)]
    return es


def _gdn_prompt_kernel(qkv_ref, z_ref, ab_ref, gate_ref, convw_ref, alog_ref, dtb_ref, ghn_ref,
                       wproj_ref, ha_ref, sfin_ref, xpad_ref, qkvn_ref, o_ref, s_ref,
                       u_ref, w_ref, qd_ref, kd_ref, qk_ref):
    c = GDN_CHUNK
    tl = GDN_SEQS * c
    gw = GROUP_HEADS * c
    step = pl.program_id(1)

    @pl.when(step == 0)
    def _():
        xpad_ref[:, 0:8, :] = jnp.zeros((GDN_SEQS, 8, CONV_DIM), F32)
        s_ref[...] = jnp.zeros_like(s_ref)

    for i in range(GDN_SEQS):
        xpad_ref[i, 8:8 + c, :] = qkv_ref[i]
        for blk in range(CONV_DIM // LANES):
            cols = slice(blk * LANES, (blk + 1) * LANES)
            acc = xpad_ref[i, 8:8 + c, cols] * convw_ref[3:4, cols]
            for w in range(CONV_WIDTH - 1):
                acc = acc + xpad_ref[i, 5 + w:5 + w + c, cols] * convw_ref[w:w + 1, cols]
            y = _silu(acc)
            if blk < 2 * GDN_HEADS:
                y = y * lax.rsqrt(jnp.sum(y * y, axis=-1, keepdims=True) + RMS_EPS)
                if blk < GDN_HEADS:
                    y = y * (HEAD_DIM ** -0.5)
            qkvn_ref[i * c:(i + 1) * c, cols] = y
        xpad_ref[i, 5:8, :] = xpad_ref[i, c + 5:c + 8, :]

    ab = ab_ref[...].reshape(tl, LANES)
    g_full = -jnp.exp(alog_ref[...]) * _softplus(ab + dtb_ref[...])
    beta_full = _sigmoid(ab)
    row = lax.broadcasted_iota(jnp.int32, (tl, tl), 0)
    col = lax.broadcasted_iota(jnp.int32, (tl, tl), 1)
    same_chunk = (row // c) == (col // c)
    tri = jnp.where(same_chunk & (col <= row), 1.0, 0.0).astype(F32)
    g_cum = _dot_f32(tri, g_full)
    g_cum_t = g_cum.T
    last = jnp.where(same_chunk & ((col % c) == c - 1), 1.0, 0.0).astype(F32)
    g_last = _dot_f32(last, g_cum)
    exp_g = jnp.exp(g_cum)
    exp_rest = jnp.exp(g_last - g_cum)
    exp_last = jnp.exp(g_last)

    ci = lax.broadcasted_iota(jnp.int32, (c, gw), 0)
    cj = lax.broadcasted_iota(jnp.int32, (c, gw), 1) % c
    causal4 = cj <= ci
    strict4 = cj < ci
    bd_row = lax.broadcasted_iota(jnp.int32, (gw, gw), 0) // c
    bd_col = lax.broadcasted_iota(jnp.int32, (gw, gw), 1) // c
    bd_mask = jnp.where(bd_row == bd_col, 1.0, 0.0).astype(BF16)
    hk_row = lax.broadcasted_iota(jnp.int32, (gw, GROUP_HEADS * LANES), 0) // c
    hk_col = lax.broadcasted_iota(jnp.int32, (gw, GROUP_HEADS * LANES), 1) // LANES
    hk_mask = jnp.where(hk_row == hk_col, 1.0, 0.0).astype(BF16)

    def lanes_of(col_values, width):
        return jnp.concatenate([jnp.broadcast_to(v, (c, width)) for v in col_values], axis=1)

    instances = [(slice(ch * c, (ch + 1) * c), hg) for ch in range(tl // c)
                 for hg in range(GDN_HEADS // GROUP_HEADS)]

    def group_cols(hg):
        return slice(hg * GROUP_HEADS * LANES, (hg + 1) * GROUP_HEADS * LANES)

    a4s = []
    for rows, hg in instances:
        heads = range(hg * GROUP_HEADS, (hg + 1) * GROUP_HEADS)
        cols = group_cols(hg)
        q4 = qkvn_ref[rows, cols]
        k4 = qkvn_ref[rows, GDN_QK + cols.start:GDN_QK + cols.stop]
        v4 = qkvn_ref[rows, 2 * GDN_QK + cols.start:2 * GDN_QK + cols.stop]
        beta_e = lanes_of([beta_full[rows, GDN_HEADS + h:GDN_HEADS + h + 1] for h in heads], LANES)
        eg_e = lanes_of([exp_g[rows, h:h + 1] for h in heads], LANES)
        er_e = lanes_of([exp_rest[rows, h:h + 1] for h in heads], LANES)
        gi4 = lanes_of([g_cum[rows, h:h + 1] for h in heads], c)
        gj4 = jnp.concatenate([g_cum_t[h:h + 1, rows] for h in heads], axis=1)
        decay4 = jnp.exp(jnp.where(causal4, gi4 - gj4, NEG))
        kb4 = k4 * beta_e
        k_rhs = jnp.concatenate([k4.astype(BF16)] * GROUP_HEADS, axis=0) * hk_mask
        kkqk = lax.dot_general(jnp.concatenate([kb4, q4], axis=0).astype(BF16), k_rhs,
                               (((1,), (1,)), ((), ())), preferred_element_type=F32)
        a4s.append(jnp.where(strict4, kkqk[:c] * decay4, 0.0))
        qk_ref[rows, hg * gw:(hg + 1) * gw] = kkqk[c:] * decay4
        u_ref[rows, cols] = v4 * beta_e
        w_ref[rows, cols] = kb4 * eg_e
        qd_ref[rows, cols] = q4 * eg_e
        kd_ref[rows, cols] = k4 * er_e

    e4s = _unit_lower_inverse4(a4s, bd_mask)

    for (rows, hg), e4 in zip(instances, e4s):
        cols = group_cols(hg)
        vb4 = u_ref[rows, cols]
        kbe4 = w_ref[rows, cols]
        us, ws = [], []
        for i in range(GROUP_HEADS):
            hc = slice(i * LANES, (i + 1) * LANES)
            rhs = jnp.concatenate([vb4[:, hc], kbe4[:, hc]], axis=1)
            uw = rhs + _dot(e4[:, i * c:(i + 1) * c], rhs)
            us.append(uw[:, :LANES])
            ws.append(uw[:, LANES:])
        u_ref[rows, cols] = jnp.concatenate(us, axis=1)
        w_ref[rows, cols] = jnp.concatenate(ws, axis=1)

    zero_s = jnp.zeros((HEAD_DIM, HEAD_DIM), BF16)
    zero_v = jnp.zeros((c, HEAD_DIM), BF16)
    pairs = [(ch, p) for ch in range(tl // c) for p in range(GDN_HEADS // 2)]

    def pair_slices(ch, p):
        return slice(ch * c, (ch + 1) * c), slice(2 * p * LANES, (2 * p + 2) * LANES)

    wsqs = []
    for ch, p in pairs:
        rows, cols = pair_slices(ch, p)
        s_bd = jnp.concatenate(
            [jnp.concatenate([s_ref[ch, 2 * p].astype(BF16), zero_s], axis=1),
             jnp.concatenate([zero_s, s_ref[ch, 2 * p + 1].astype(BF16)], axis=1)], axis=0)
        wq = jnp.concatenate([w_ref[rows, cols], qd_ref[rows, cols]], axis=0).astype(BF16)
        wsqs.append(jnp.dot(wq, s_bd, preferred_element_type=F32))
    vnbs = []
    for (ch, p), wsq in zip(pairs, wsqs):
        rows, cols = pair_slices(ch, p)
        vnb = (u_ref[rows, cols] - wsq[:c]).astype(BF16)
        vnbs.append(vnb)
        vn_bd = jnp.concatenate([jnp.concatenate([vnb[:, :LANES], zero_v], axis=1),
                                 jnp.concatenate([zero_v, vnb[:, LANES:]], axis=1)], axis=0)
        qk2 = qk_ref[rows, 2 * p * c:(2 * p + 2) * c]
        o_ref[rows, cols] = wsq[c:] + jnp.dot(qk2.astype(BF16), vn_bd, preferred_element_type=F32)
    for (ch, p), vnb in zip(pairs, vnbs):
        rows, cols = pair_slices(ch, p)
        kd2 = kd_ref[rows, cols]
        for i in range(2):
            h = 2 * p + i
            hc = slice(i * LANES, (i + 1) * LANES)
            el = exp_last[ch * c:ch * c + 1, h:h + 1]
            s_ref[ch, h] = s_ref[ch, h] * el + _dot_tn(kd2[:, hc], vnb[:, hc])

    for h in range(GDN_HEADS):
        cols = slice(h * LANES, (h + 1) * LANES)
        o = o_ref[:, cols]
        o = o * lax.rsqrt(jnp.mean(o * o, axis=-1, keepdims=True) + RMS_EPS) * ghn_ref[...]
        o_ref[:, cols] = o * _silu(z_ref[:, :, cols].reshape(tl, LANES))
    p_a = jnp.dot(o_ref[...].astype(BF16), wproj_ref[...], preferred_element_type=F32)
    h_a = _sigmoid(gate_ref[...].reshape(tl, D_MODEL)) * p_a
    ha_ref[...] = h_a.reshape(GDN_SEQS, c, D_MODEL)

    @pl.when(step == pl.num_programs(1) - 1)
    def _():
        sfin_ref[...] = s_ref[...]


def _gdn_prompt(proj, ab, conv_wt, alog_pad, dtb_pad, ghn, w_proj_a, batch, seq):
    c = GDN_CHUNK
    ns = GDN_SEQS
    tl = ns * c
    proj3 = proj.reshape(batch, seq, MAIN_WIDTH)
    ab3 = ab.reshape(batch, seq, LANES)
    h_a, s_fin = pl.pallas_call(
        _gdn_prompt_kernel,
        grid=(batch // ns, seq // c),
        in_specs=[
            pl.BlockSpec((ns, c, CONV_DIM), lambda b, l: (b, l, COL_QKV_A // CONV_DIM)),
            pl.BlockSpec((ns, c, GDN_QK), lambda b, l: (b, l, COL_Z_A // GDN_QK)),
            pl.BlockSpec((ns, c, LANES), lambda b, l: (b, l, 0)),
            pl.BlockSpec((ns, c, D_MODEL), lambda b, l: (b, l, COL_GATE_A // D_MODEL)),
            pl.BlockSpec((CONV_WIDTH, CONV_DIM), lambda b, l: (0, 0)),
            pl.BlockSpec((1, LANES), lambda b, l: (0, 0)),
            pl.BlockSpec((1, LANES), lambda b, l: (0, 0)),
            pl.BlockSpec((1, LANES), lambda b, l: (0, 0)),
            pl.BlockSpec((GDN_QK, D_MODEL), lambda b, l: (0, 0)),
        ],
        out_specs=[
            pl.BlockSpec((ns, c, D_MODEL), lambda b, l: (b, l, 0)),
            pl.BlockSpec((ns, GDN_HEADS, HEAD_DIM, HEAD_DIM), lambda b, l: (b, 0, 0, 0)),
        ],
        out_shape=[
            jax.ShapeDtypeStruct((batch, seq, D_MODEL), F32),
            jax.ShapeDtypeStruct((batch, GDN_HEADS, HEAD_DIM, HEAD_DIM), F32),
        ],
        scratch_shapes=[
            pltpu.VMEM((ns, c + 8, CONV_DIM), F32),
            pltpu.VMEM((tl, CONV_DIM), F32),
            pltpu.VMEM((tl, GDN_QK), F32),
            pltpu.VMEM((ns, GDN_HEADS, HEAD_DIM, HEAD_DIM), F32),
            pltpu.VMEM((tl, GDN_QK), F32),
            pltpu.VMEM((tl, GDN_QK), F32),
            pltpu.VMEM((tl, GDN_QK), F32),
            pltpu.VMEM((tl, GDN_QK), F32),
            pltpu.VMEM((tl, GDN_HEADS * GDN_CHUNK), F32),
        ],
        compiler_params=pltpu.CompilerParams(
            dimension_semantics=("parallel", "arbitrary"), vmem_limit_bytes=VMEM_LIMIT),
        name="gdn_prompt",
    )(proj3, proj3, ab3, proj3, conv_wt, alog_pad, dtb_pad, ghn, w_proj_a)
    return h_a.reshape(batch * seq, D_MODEL), s_fin


def _gdn_sample_kernel(sc_q_ref, sc_k_ref, sc_v_ref, nq_ref, nk_ref, nv_ref, cwq_ref, cwk_ref, cwv_ref,
                       ab_ref, alog_ref, dtb_ref, s_ref, o_ref, snew_ref):
    h = pl.program_id(0)
    nb = nq_ref.shape[0]

    def conv_silu(sc_ref, new_ref, cw_ref):
        acc = new_ref[...] * cw_ref[3:4, :]
        for w in range(CONV_WIDTH - 1):
            acc = acc + sc_ref[w] * cw_ref[w:w + 1, :]
        return _silu(acc)

    q = conv_silu(sc_q_ref, nq_ref, cwq_ref)
    k = conv_silu(sc_k_ref, nk_ref, cwk_ref)
    v = conv_silu(sc_v_ref, nv_ref, cwv_ref)
    q = q * lax.rsqrt(jnp.sum(q * q, axis=-1, keepdims=True) + RMS_EPS) * (HEAD_DIM ** -0.5)
    k = k * lax.rsqrt(jnp.sum(k * k, axis=-1, keepdims=True) + RMS_EPS)

    ab = ab_ref[...]
    lane = lax.broadcasted_iota(jnp.int32, ab.shape, 1)
    g_full = -jnp.exp(alog_ref[...]) * _softplus(ab + dtb_ref[...])
    g = jnp.sum(jnp.where(lane == h, g_full, 0.0), axis=-1, keepdims=True)
    beta = jnp.sum(jnp.where(lane == h + GDN_HEADS, _sigmoid(ab), 0.0), axis=-1, keepdims=True)
    eg = jnp.exp(g)

    eye = (lax.broadcasted_iota(jnp.int32, (HEAD_DIM, HEAD_DIM), 0)
           == lax.broadcasted_iota(jnp.int32, (HEAD_DIM, HEAD_DIM), 1))

    def as_column(row):
        return jnp.sum(jnp.where(eye, row, 0.0), axis=1, keepdims=True)

    for b in range(nb):
        s = s_ref[b]
        k_col = as_column(k[b:b + 1, :])
        q_col = as_column(q[b:b + 1, :])
        e = eg[b:b + 1, :]
        ks = jnp.sum(k_col * s, axis=0, keepdims=True)
        v_new = beta[b:b + 1, :] * (v[b:b + 1, :] - e * ks)
        s_new = s * e + k_col * v_new
        snew_ref[b] = s_new
        o_ref[b:b + 1, :] = jnp.sum(q_col * s_new, axis=0, keepdims=True)


def _gdn_sample(state_conv_t, proj, ab, conv_wt, alog_pad, dtb_pad, state, nb):
    batch = proj.shape[0]
    hq = GDN_QK // LANES
    return pl.pallas_call(
        _gdn_sample_kernel,
        grid=(GDN_HEADS, batch // nb),
        in_specs=[
            pl.BlockSpec((CONV_WIDTH - 1, nb, LANES), lambda h, i: (0, i, h)),
            pl.BlockSpec((CONV_WIDTH - 1, nb, LANES), lambda h, i: (0, i, hq + h)),
            pl.BlockSpec((CONV_WIDTH - 1, nb, LANES), lambda h, i: (0, i, 2 * hq + h)),
            pl.BlockSpec((nb, LANES), lambda h, i: (i, h)),
            pl.BlockSpec((nb, LANES), lambda h, i: (i, hq + h)),
            pl.BlockSpec((nb, LANES), lambda h, i: (i, 2 * hq + h)),
            pl.BlockSpec((CONV_WIDTH, LANES), lambda h, i: (0, h)),
            pl.BlockSpec((CONV_WIDTH, LANES), lambda h, i: (0, hq + h)),
            pl.BlockSpec((CONV_WIDTH, LANES), lambda h, i: (0, 2 * hq + h)),
            pl.BlockSpec((nb, LANES), lambda h, i: (i, 0)),
            pl.BlockSpec((1, LANES), lambda h, i: (0, 0)),
            pl.BlockSpec((1, LANES), lambda h, i: (0, 0)),
            pl.BlockSpec((nb, None, HEAD_DIM, HEAD_DIM), lambda h, i: (i, h, 0, 0)),
        ],
        out_specs=[
            pl.BlockSpec((nb, LANES), lambda h, i: (i, h)),
            pl.BlockSpec((nb, None, HEAD_DIM, HEAD_DIM), lambda h, i: (i, h, 0, 0)),
        ],
        out_shape=[
            jax.ShapeDtypeStruct((batch, GDN_QK), F32),
            jax.ShapeDtypeStruct(state.shape, F32),
        ],
        compiler_params=pltpu.CompilerParams(
            dimension_semantics=("parallel", "parallel"), vmem_limit_bytes=VMEM_LIMIT),
        name="gdn_sample",
    )(state_conv_t, state_conv_t, state_conv_t, proj, proj, proj, conv_wt, conv_wt, conv_wt,
      ab, alog_pad, dtb_pad, state)


def _gdn_sample_out_kernel(o_ref, z_ref, gate_ref, ghn_ref, wproj_ref, ha_ref, y_ref):
    for h in range(GDN_HEADS):
        cols = slice(h * LANES, (h + 1) * LANES)
        o = o_ref[:, cols]
        o = o * lax.rsqrt(jnp.mean(o * o, axis=-1, keepdims=True) + RMS_EPS) * ghn_ref[...]
        y_ref[:, cols] = o * _silu(z_ref[:, cols])
    p_a = jnp.dot(y_ref[...].astype(BF16), wproj_ref[...], preferred_element_type=F32)
    ha_ref[...] = _sigmoid(gate_ref[...]) * p_a


def _gdn_sample_out(o, proj, ghn, w_proj_a):
    batch = o.shape[0]
    return pl.pallas_call(
        _gdn_sample_out_kernel,
        grid=(1,),
        in_specs=[
            pl.BlockSpec((batch, GDN_QK), lambda i: (0, 0)),
            pl.BlockSpec((batch, GDN_QK), lambda i: (0, COL_Z_A // GDN_QK)),
            pl.BlockSpec((batch, D_MODEL), lambda i: (0, COL_GATE_A // D_MODEL)),
            pl.BlockSpec((1, LANES), lambda i: (0, 0)),
            pl.BlockSpec((GDN_QK, D_MODEL), lambda i: (0, 0)),
        ],
        out_specs=pl.BlockSpec((batch, D_MODEL), lambda i: (0, 0)),
        out_shape=jax.ShapeDtypeStruct((batch, D_MODEL), F32),
        scratch_shapes=[pltpu.VMEM((batch, GDN_QK), F32)],
        compiler_params=pltpu.CompilerParams(vmem_limit_bytes=VMEM_LIMIT),
        name="gdn_sample_out",
    )(o, proj, proj, ghn, w_proj_a)


def _attn_prompt_kernel(*refs):
    ins = refs[:5 * N_GROUPS]
    bias_ref = refs[5 * N_GROUPS]
    o_ref = refs[5 * N_GROUPS + 1]
    acc_ref, m_ref, l_ref = refs[5 * N_GROUPS + 2:]
    tq = ATTN_TILE
    first_tile = pl.program_id(1) == 0
    scale = HEAD_DIM ** -0.5
    lane2 = lax.broadcasted_iota(jnp.int32, (1, 2 * WIN_BLK), 1)
    start_mask = jnp.where(first_tile & (lane2 < WIN_BLK), NEG, 0.0).astype(F32)

    for gi, (_, dil) in enumerate(DIL_GROUPS):
        q_ref, kp_ref, kc_ref, vp_ref, vc_ref = ins[5 * gi:5 * gi + 5]
        bias = bias_ref[gi]
        n_blocks = tq // (dil * WIN_BLK)

        def unit(r, j, q_ref=q_ref, kp_ref=kp_ref, kc_ref=kc_ref, vp_ref=vp_ref, vc_ref=vc_ref,
                 bias=bias, dil=dil, gi=gi):
            base = dil * WIN_BLK * j + r
            q_rows = pl.ds(base, WIN_BLK, stride=dil) if dil > 1 else pl.ds(base, WIN_BLK)
            q = q_ref[q_rows, :]
            if j == 0:
                rows = pl.ds(r, WIN_BLK, stride=dil) if dil > 1 else pl.ds(r, WIN_BLK)
                kk = jnp.concatenate([kp_ref[rows, :], kc_ref[rows, :]], axis=0)
                vv = jnp.concatenate([vp_ref[rows, :], vc_ref[rows, :]], axis=0)
            else:
                pbase = dil * WIN_BLK * (j - 1) + r
                rows = (pl.ds(pbase, 2 * WIN_BLK, stride=dil) if dil > 1
                        else pl.ds(pbase, 2 * WIN_BLK))
                kk = kc_ref[rows, :]
                vv = vc_ref[rows, :]
            s = _dot_nt(q, kk) * scale + bias
            if j == 0:
                s = s + start_mask
            m = jnp.max(s, axis=-1, keepdims=True)
            p = jnp.exp(s - m)
            l = jnp.sum(p, axis=-1, keepdims=True)
            acc = _dot(p, vv)
            acc_ref[gi, q_rows, :] = acc
            m_ref[gi, q_rows, :] = jnp.broadcast_to(m, (WIN_BLK, LANES))
            l_ref[gi, q_rows, :] = jnp.broadcast_to(l, (WIN_BLK, LANES))

        for j in range(n_blocks):
            if dil == 1:
                unit(0, j)
            else:
                def body(r, carry, j=j, unit=unit):
                    unit(r, j)
                    return carry
                lax.fori_loop(0, dil, body, 0)

    m_all = jnp.maximum(jnp.maximum(m_ref[0], m_ref[1]), m_ref[2])
    num = jnp.zeros((tq, LANES), F32)
    den = jnp.zeros((tq, LANES), F32)
    for gi in range(N_GROUPS):
        wgt = jnp.exp(m_ref[gi] - m_all)
        num = num + wgt * acc_ref[gi]
        den = den + wgt * l_ref[gi]
    o_ref[...] = num / den


def _attn_prompt(proj, bias_tiles, batch, seq):
    tq = ATTN_TILE
    nt = seq // tq
    in_specs = []
    args = []
    for gi, (_, dil) in enumerate(DIL_GROUPS):
        span = dil * WIN_BLK
        cq = (COL_QKV_B + gi * 3 * DIL_W) // LANES
        ck = cq + DIL_HEADS
        cv = ck + DIL_HEADS

        def cur(col):
            return pl.BlockSpec((tq, LANES), lambda b, i, h, col=col: (b * nt + i, col + h))

        def prev(col, span=span):
            per = tq // span
            return pl.BlockSpec(
                (span, LANES),
                lambda b, i, h, col=col, per=per: (jnp.maximum((b * nt + i) * per - 1, 0), col + h))

        in_specs += [cur(cq), prev(ck), cur(ck), prev(cv), cur(cv)]
        args += [proj] * 5
    in_specs.append(pl.BlockSpec((N_GROUPS, None, WIN_BLK, 2 * WIN_BLK), lambda b, i, h: (0, h, 0, 0)))
    args.append(bias_tiles)
    return pl.pallas_call(
        _attn_prompt_kernel,
        grid=(batch, nt, DIL_HEADS),
        in_specs=in_specs,
        out_specs=pl.BlockSpec((tq, LANES), lambda b, i, h: (b * nt + i, h)),
        out_shape=jax.ShapeDtypeStruct((batch * seq, DIL_W), F32),
        scratch_shapes=[pltpu.VMEM((N_GROUPS, tq, LANES), F32)] * 3,
        compiler_params=pltpu.CompilerParams(
            dimension_semantics=("parallel", "parallel", "parallel"), vmem_limit_bytes=VMEM_LIMIT),
        name="attn_prompt",
    )(*args)


def _attn_sample_kernel(*refs):
    q_refs = refs[0:N_GROUPS]
    kn_refs = refs[N_GROUPS:2 * N_GROUPS]
    vn_refs = refs[2 * N_GROUPS:3 * N_GROUPS]
    kc_refs = refs[3 * N_GROUPS:4 * N_GROUPS]
    vc_refs = refs[4 * N_GROUPS:5 * N_GROUPS]
    bias0_ref, bias_ref, o_ref = refs[5 * N_GROUPS:5 * N_GROUPS + 3]
    m_ref, l_ref, acc_ref = refs[5 * N_GROUPS + 3:]
    t = pl.program_id(0)
    scale = HEAD_DIM ** -0.5

    @pl.when(t == 0)
    def _():
        for gi in range(N_GROUPS):
            s = jnp.sum(q_refs[gi][...] * kn_refs[gi][...], axis=-1, keepdims=True) * scale + bias0_ref[gi]
            m_ref[gi] = s
            l_ref[gi] = jnp.ones_like(s)
            acc_ref[gi] = vn_refs[gi][...]

    for gi in range(N_GROUPS):
        s = jnp.sum(q_refs[gi][...] * kc_refs[gi][...], axis=-1, keepdims=True) * scale + bias_ref[gi]
        m_old = m_ref[gi]
        m_new = jnp.maximum(m_old, s)
        alpha = jnp.exp(m_old - m_new)
        p = jnp.exp(s - m_new)
        l_ref[gi] = alpha * l_ref[gi] + p
        acc_ref[gi] = alpha * acc_ref[gi] + p * vc_refs[gi][...]
        m_ref[gi] = m_new

    @pl.when(t == pl.num_programs(0) - 1)
    def _():
        m_all = jnp.maximum(jnp.maximum(m_ref[0], m_ref[1]), m_ref[2])
        num = jnp.zeros(acc_ref.shape[1:], F32)
        den = jnp.zeros(m_ref.shape[1:], F32)
        for gi in range(N_GROUPS):
            wgt = jnp.exp(m_ref[gi] - m_all)
            num = num + wgt * acc_ref[gi]
            den = den + wgt * l_ref[gi]
        o_ref[...] = num / den


def _attn_sample(q_g, kn_g, vn_g, k_caches, v_caches, bias0, bias_steps):
    batch = q_g[0].shape[0]
    small = pl.BlockSpec((batch, DIL_HEADS, HEAD_DIM), lambda t: (0, 0, 0))
    in_specs = [small] * (3 * N_GROUPS)
    for _ in range(2):
        for _, dil in DIL_GROUPS:
            in_specs.append(pl.BlockSpec((None, batch, None, DIL_HEADS, HEAD_DIM),
                                         lambda t, dil=dil: (0, 0, t * dil, 0, 0)))
    in_specs.append(pl.BlockSpec((N_GROUPS, 1, DIL_HEADS, 1), lambda t: (0, 0, 0, 0)))
    in_specs.append(pl.BlockSpec((None, N_GROUPS, 1, DIL_HEADS, 1), lambda t: (t, 0, 0, 0, 0)))
    return pl.pallas_call(
        _attn_sample_kernel,
        grid=(WIN_BLK,),
        in_specs=in_specs,
        out_specs=pl.BlockSpec((batch, DIL_HEADS, HEAD_DIM), lambda t: (0, 0, 0)),
        out_shape=jax.ShapeDtypeStruct((batch, DIL_HEADS, HEAD_DIM), F32),
        scratch_shapes=[
            pltpu.VMEM((N_GROUPS, batch, DIL_HEADS, 1), F32),
            pltpu.VMEM((N_GROUPS, batch, DIL_HEADS, 1), F32),
            pltpu.VMEM((N_GROUPS, batch, DIL_HEADS, HEAD_DIM), F32),
        ],
        compiler_params=pltpu.CompilerParams(
            dimension_semantics=("arbitrary",), vmem_limit_bytes=VMEM_LIMIT),
        name="attn_sample",
    )(*q_g, *kn_g, *vn_g, *k_caches, *v_caches, bias0, bias_steps)


def _output_kernel(ob_ref, zb_ref, gateb_ref, ha_ref, x_ref, wpb_ref, wout_ref, gpost_ref, y_ref):
    yb = (ob_ref[...] * _silu(zb_ref[...])).astype(BF16)
    p_b = jnp.dot(yb, wpb_ref[...], preferred_element_type=F32)
    hid = ha_ref[...] + _sigmoid(gateb_ref[...]) * p_b
    out = jnp.dot(hid.astype(BF16), wout_ref[...], preferred_element_type=F32)
    ms = jnp.mean(out * out, axis=-1, keepdims=True)
    y_ref[...] = x_ref[...] + out * lax.rsqrt(ms + RMS_EPS) * gpost_ref[...]


def _output(o_b, proj, h_a, x2d, w_proj_b, w_out, g_post, tm):
    m = x2d.shape[0]
    return pl.pallas_call(
        _output_kernel,
        grid=(m // tm,),
        in_specs=[
            pl.BlockSpec((tm, DIL_W), lambda i: (i, 0)),
            pl.BlockSpec((tm, DIL_W), lambda i: (i, COL_Z_B // DIL_W)),
            pl.BlockSpec((tm, D_MODEL), lambda i: (i, COL_GATE_B // D_MODEL)),
            pl.BlockSpec((tm, D_MODEL), lambda i: (i, 0)),
            pl.BlockSpec((tm, D_MODEL), lambda i: (i, 0)),
            pl.BlockSpec((DIL_W, D_MODEL), lambda i: (0, 0)),
            pl.BlockSpec((D_MODEL, D_MODEL), lambda i: (0, 0)),
            pl.BlockSpec((1, D_MODEL), lambda i: (0, 0)),
        ],
        out_specs=pl.BlockSpec((tm, D_MODEL), lambda i: (i, 0)),
        out_shape=jax.ShapeDtypeStruct((m, D_MODEL), F32),
        compiler_params=pltpu.CompilerParams(
            dimension_semantics=("parallel",), vmem_limit_bytes=VMEM_LIMIT),
        name="output",
    )(o_b, proj, proj, h_a, x2d, w_proj_b, w_out, g_post)


def _rel_bucket(dist):
    max_exact = REL_BUCKETS // 2
    d = jnp.maximum(dist, 1).astype(F32)
    large = max_exact + (jnp.log(d / max_exact) / math.log(REL_MAX_DIST / max_exact)
                         * (REL_BUCKETS - max_exact)).astype(jnp.int32)
    large = jnp.minimum(large, REL_BUCKETS - 1)
    return jnp.where(dist < max_exact, dist, large)


def _group_biases(rel_table):
    out = []
    for gi, (win, dil) in enumerate(DIL_GROUPS):
        dist = jnp.arange(win // dil + 1, dtype=jnp.int32) * dil
        b = rel_table[_rel_bucket(dist)][:, gi * DIL_HEADS:(gi + 1) * DIL_HEADS]
        out.append(b.T.astype(F32))
    return out


def _bias_tiles(biases):
    i = jnp.arange(WIN_BLK)[:, None]
    kx = jnp.arange(2 * WIN_BLK)[None, :]
    j = WIN_BLK + i - kx
    band = (j >= 0) & (j <= WIN_BLK)
    tiles = [jnp.where(band[None], b[:, jnp.clip(j, 0, WIN_BLK)], NEG) for b in biases]
    return jnp.stack(tiles, axis=0).astype(F32)


def _layer_weights(w_in, conv_w, a_log, dt_bias, g_head_norm, w_proj_a, w_proj_b, w_out):
    w_main = jnp.concatenate([w_in[:, :AB_OFFSET], w_in[:, AB_OFFSET + 2 * GDN_HEADS:]], axis=1).astype(BF16)
    w_ab = jnp.pad(w_in[:, AB_OFFSET:AB_OFFSET + 2 * GDN_HEADS],
                   ((0, 0), (0, LANES - 2 * GDN_HEADS))).astype(BF16)
    pad = (0, LANES - GDN_HEADS)
    return dict(
        w_main=w_main, w_ab=w_ab, conv_wt=conv_w.T.astype(F32),
        alog_pad=jnp.pad(a_log.astype(F32), pad)[None, :],
        dtb_pad=jnp.pad(dt_bias.astype(F32), pad)[None, :],
        ghn=g_head_norm.astype(F32)[None, :],
        w_proj_a=w_proj_a.astype(BF16), w_proj_b=w_proj_b.astype(BF16), w_out=w_out.astype(BF16))


def _split_heads(a2d, col, rows_per_batch, batch, keep):
    a = a2d.reshape(batch, rows_per_batch, -1)[:, rows_per_batch - keep:, col:col + DIL_W]
    return a.reshape(batch, keep, DIL_HEADS, HEAD_DIM)


def kernel(x_prompt, x_sample, state_gdn, state_conv, cache_k_w128, cache_v_w128, cache_k_w512,
           cache_v_w512, cache_k_w2048, cache_v_w2048, rel_table, g_pre, w_in, conv_w, a_log, dt_bias,
           g_head_norm, w_proj_a, w_proj_b, w_out, g_post):
    batch, seq, _ = x_prompt.shape
    dec_batch = x_sample.shape[0]
    k_caches = (cache_k_w128, cache_k_w512, cache_k_w2048)
    v_caches = (cache_v_w128, cache_v_w512, cache_v_w2048)
    biases = _group_biases(rel_table)
    bias_tiles = _bias_tiles(biases)
    lw = _layer_weights(w_in[0], conv_w[0], a_log[0], dt_bias[0], g_head_norm[0], w_proj_a[0],
                        w_proj_b[0], w_out[0])
    g_pre_l = g_pre[0][None, :]
    g_post_l = g_post[0][None, :]

    xp = x_prompt.reshape(batch * seq, D_MODEL)
    proj, ab = _in_projection(xp, g_pre_l, lw["w_main"], lw["w_ab"], tm=1024, tn=1408)
    h_a, p_gdn = _gdn_prompt(proj, ab, lw["conv_wt"], lw["alog_pad"], lw["dtb_pad"], lw["ghn"],
                             lw["w_proj_a"], batch, seq)
    o_b = _attn_prompt(proj, bias_tiles, batch, seq)
    y_prompt = _output(o_b, proj, h_a, xp, lw["w_proj_b"], lw["w_out"], g_post_l, tm=512)
    y_prompt = y_prompt.reshape(batch, seq, D_MODEL)
    p_conv = proj.reshape(batch, seq, MAIN_WIDTH)[:, seq - (CONV_WIDTH - 1):, :CONV_DIM]
    p_rows = []
    for gi, (win, _) in enumerate(DIL_GROUPS):
        keep = min(win, seq)
        ck = COL_QKV_B + gi * 3 * DIL_W + DIL_W
        p_rows += [_split_heads(proj, ck, seq, batch, keep), _split_heads(proj, ck + DIL_W, seq, batch, keep)]

    xs = x_sample.reshape(dec_batch, D_MODEL)
    proj_s, ab_s = _in_projection(xs, g_pre_l, lw["w_main"], lw["w_ab"], tm=dec_batch, tn=1408)
    sc_t = jnp.transpose(state_conv[0], (1, 0, 2))
    o_s, s_gdn = _gdn_sample(sc_t, proj_s, ab_s, lw["conv_wt"], lw["alog_pad"], lw["dtb_pad"],
                             state_gdn[0], nb=32)
    h_a_s = _gdn_sample_out(o_s, proj_s, lw["ghn"], lw["w_proj_a"])
    q_g, kn_g, vn_g = [], [], []
    for gi in range(N_GROUPS):
        cq = COL_QKV_B + gi * 3 * DIL_W
        q_g.append(proj_s[:, cq:cq + DIL_W].reshape(dec_batch, DIL_HEADS, HEAD_DIM))
        kn_g.append(proj_s[:, cq + DIL_W:cq + 2 * DIL_W].reshape(dec_batch, DIL_HEADS, HEAD_DIM))
        vn_g.append(proj_s[:, cq + 2 * DIL_W:cq + 3 * DIL_W].reshape(dec_batch, DIL_HEADS, HEAD_DIM))
    bias_all = jnp.stack(biases, axis=0)
    bias0 = bias_all[:, None, :, 0:1]
    bias_steps = jnp.transpose(bias_all[:, :, :0:-1], (2, 0, 1))[:, :, None, :, None]
    o_b_s = _attn_sample(q_g, kn_g, vn_g, k_caches, v_caches, bias0, bias_steps)
    y_sample = _output(o_b_s.reshape(dec_batch, DIL_W), proj_s, h_a_s, xs, lw["w_proj_b"], lw["w_out"],
                       g_post_l, tm=dec_batch)
    y_sample = y_sample.reshape(dec_batch, 1, D_MODEL)
    s_conv = jnp.concatenate([state_conv[0][:, 1:, :], proj_s[:, None, :CONV_DIM]], axis=1)
    s_rows = []
    for gi in range(N_GROUPS):
        s_rows += [kn_g[gi][:, None], vn_g[gi][:, None]]

    return (y_prompt, y_sample, p_gdn[None], p_conv[None], *[r[None] for r in p_rows],
            s_gdn[None], s_conv[None], *[r[None] for r in s_rows])
```

```python
import functools
import math

import jax
import jax.numpy as jnp
from jax import lax
from jax.experimental import pallas as pl
from jax.experimental.pallas import tpu as pltpu

F32 = jnp.float32
BF16 = jnp.bfloat16

D_MODEL = 1024
GDN_HEADS = 8
HEAD_DIM = 128
GDN_QK = GDN_HEADS * HEAD_DIM
CONV_DIM = 3 * GDN_QK
CONV_WIDTH = 4
DIL_GROUPS = ((128, 1), (512, 4), (2048, 16))
N_GROUPS = 3
DIL_HEADS = 4
DIL_W = DIL_HEADS * HEAD_DIM
WIN_BLK = 128
REL_BUCKETS = 32
REL_MAX_DIST = 2048
RMS_EPS = 1e-6
NEG = -1e30
LOG2_E = math.log2(math.e)

COL_QKV_A = 0
COL_Z_A = 3072
COL_QKV_B = 4096
COL_Z_B = 8704
COL_GATE_A = 9216
COL_GATE_B = 10240
MAIN_WIDTH = 11264
AB_OFFSET = 4096
LANES = 128

VMEM_LIMIT = 56 * 1024 * 1024

GDN_CHUNK = 64
GDN_SEQS = 4
ATTN_TILE = 2048
ATTN_UNITS = 8
MAX_ROW_STRIDE = 4


def _silu(x):
    return x * (1.0 / (1.0 + jnp.exp(-x)))


def _sigmoid(x):
    return 1.0 / (1.0 + jnp.exp(-x))


def _softplus(x):
    return jnp.maximum(x, 0.0) + jnp.log(1.0 + jnp.exp(-jnp.abs(x)))


def _dot(a, b):
    return jnp.dot(a.astype(BF16), b.astype(BF16), preferred_element_type=F32)


def _dot_nt(a, b):
    return lax.dot_general(a.astype(BF16), b.astype(BF16), (((1,), (1,)), ((), ())),
                           preferred_element_type=F32)


def _dot_tn(a, b):
    return lax.dot_general(a.astype(BF16), b.astype(BF16), (((0,), (0,)), ((), ())),
                           preferred_element_type=F32)


def _dot_f32(a, b):
    return jnp.dot(a, b, preferred_element_type=F32, precision=lax.Precision.HIGHEST)


def _inproj_kernel(x_ref, g_ref, w_ref, wab_ref, o_ref, ab_ref, xn_ref):
    @pl.when(pl.program_id(1) == 0)
    def _():
        x = x_ref[...]
        ms = jnp.mean(x * x, axis=-1, keepdims=True)
        xn = (x * lax.rsqrt(ms + RMS_EPS) * g_ref[...]).astype(BF16)
        xn_ref[...] = xn
        ab_ref[...] = jnp.dot(xn, wab_ref[...], preferred_element_type=F32)

    o_ref[...] = jnp.dot(xn_ref[...], w_ref[...], preferred_element_type=F32)


def _in_projection(x2d, g_pre, w_main, w_ab, tm, tn):
    m = x2d.shape[0]
    return pl.pallas_call(
        _inproj_kernel,
        grid=(m // tm, MAIN_WIDTH // tn),
        in_specs=[
            pl.BlockSpec((tm, D_MODEL), lambda i, j: (i, 0)),
            pl.BlockSpec((1, D_MODEL), lambda i, j: (0, 0)),
            pl.BlockSpec((D_MODEL, tn), lambda i, j: (0, j)),
            pl.BlockSpec((D_MODEL, LANES), lambda i, j: (0, 0)),
        ],
        out_specs=[
            pl.BlockSpec((tm, tn), lambda i, j: (i, j)),
            pl.BlockSpec((tm, LANES), lambda i, j: (i, 0)),
        ],
        out_shape=[
            jax.ShapeDtypeStruct((m, MAIN_WIDTH), F32),
            jax.ShapeDtypeStruct((m, LANES), F32),
        ],
        scratch_shapes=[pltpu.VMEM((tm, D_MODEL), BF16)],
        compiler_params=pltpu.CompilerParams(
            dimension_semantics=("parallel", "arbitrary"), vmem_limit_bytes=VMEM_LIMIT),
        name="in_projection",
    )(x2d, g_pre, w_main, w_ab)


GROUP_HEADS = 4


def _block_diag_rhs(x4, mask):
    return jnp.concatenate([x4] * GROUP_HEADS, axis=0) * mask


def _unit_lower_inverse4(a4s, mask):
    c = a4s[0].shape[0]

    def bd_dot(lhs, x4):
        return jnp.dot(lhs.astype(BF16), _block_diag_rhs(x4.astype(BF16), mask), preferred_element_type=F32)

    es = [-a4 for a4 in a4s]
    xs = [bd_dot(a4, a4) for a4 in a4s]
    steps = int(math.log2(c)) - 1
    for s in range(steps):
        if s < steps - 1:
            boths = [bd_dot(jnp.concatenate([e, x], axis=0), x) for e, x in zip(es, xs)]
            es = [e + x + both[:c] for e, x, both in zip(es, xs, boths)]
            xs = [both[c:] for both in boths]
        else:
            es = [e + x + bd_dot(e, x) for e, x in zip(es, xs)]
    return es


def _gdn_prompt_kernel(qkv_ref, z_ref, ab_ref, gate_ref, convw_ref, alog_ref, dtb_ref, ghn_ref,
                       wproj_ref, ha_ref, sfin_ref, xpad_ref, qkvn_ref, o_ref, s_ref,
                       u_ref, w_ref, qd_ref, kd_ref, qk_ref):
    c = GDN_CHUNK
    tl = GDN_SEQS * c
    gw = GROUP_HEADS * c
    step = pl.program_id(1)

    @pl.when(step == 0)
    def _():
        xpad_ref[:, 0:8, :] = jnp.zeros((GDN_SEQS, 8, CONV_DIM), F32)
        s_ref[...] = jnp.zeros_like(s_ref)

    for i in range(GDN_SEQS):
        xpad_ref[i, 8:8 + c, :] = qkv_ref[i]
        for blk in range(CONV_DIM // LANES):
            cols = slice(blk * LANES, (blk + 1) * LANES)
            acc = xpad_ref[i, 8:8 + c, cols] * convw_ref[3:4, cols]
            for w in range(CONV_WIDTH - 1):
                acc = acc + xpad_ref[i, 5 + w:5 + w + c, cols] * convw_ref[w:w + 1, cols]
            y = _silu(acc)
            if blk < 2 * GDN_HEADS:
                y = y * lax.rsqrt(jnp.sum(y * y, axis=-1, keepdims=True) + RMS_EPS)
                if blk < GDN_HEADS:
                    y = y * (HEAD_DIM ** -0.5)
            qkvn_ref[i * c:(i + 1) * c, cols] = y
        xpad_ref[i, 5:8, :] = xpad_ref[i, c + 5:c + 8, :]

    ab = ab_ref[...].reshape(tl, LANES)
    g_full = -jnp.exp(alog_ref[...]) * _softplus(ab + dtb_ref[...])
    beta_full = _sigmoid(ab)
    row = lax.broadcasted_iota(jnp.int32, (tl, tl), 0)
    col = lax.broadcasted_iota(jnp.int32, (tl, tl), 1)
    same_chunk = (row // c) == (col // c)
    tri = jnp.where(same_chunk & (col <= row), 1.0, 0.0).astype(F32)
    g_cum = _dot_f32(tri, g_full)
    g_cum_t = g_cum.T
    last = jnp.where(same_chunk & ((col % c) == c - 1), 1.0, 0.0).astype(F32)
    g_last = _dot_f32(last, g_cum)
    exp_g = jnp.exp(g_cum)
    exp_rest = jnp.exp(g_last - g_cum)
    exp_last = jnp.exp(g_last)

    ci = lax.broadcasted_iota(jnp.int32, (c, gw), 0)
    cj = lax.broadcasted_iota(jnp.int32, (c, gw), 1) % c
    causal4 = cj <= ci
    strict4 = cj < ci
    bd_row = lax.broadcasted_iota(jnp.int32, (gw, gw), 0) // c
    bd_col = lax.broadcasted_iota(jnp.int32, (gw, gw), 1) // c
    bd_mask = jnp.where(bd_row == bd_col, 1.0, 0.0).astype(BF16)
    hk_row = lax.broadcasted_iota(jnp.int32, (gw, GROUP_HEADS * LANES), 0) // c
    hk_col = lax.broadcasted_iota(jnp.int32, (gw, GROUP_HEADS * LANES), 1) // LANES
    hk_mask = jnp.where(hk_row == hk_col, 1.0, 0.0).astype(BF16)

    def lanes_of(col_values, width):
        return jnp.concatenate([jnp.broadcast_to(v, (c, width)) for v in col_values], axis=1)

    instances = [(slice(ch * c, (ch + 1) * c), hg) for ch in range(tl // c)
                 for hg in range(GDN_HEADS // GROUP_HEADS)]

    def group_cols(hg):
        return slice(hg * GROUP_HEADS * LANES, (hg + 1) * GROUP_HEADS * LANES)

    a4s = []
    for rows, hg in instances:
        heads = range(hg * GROUP_HEADS, (hg + 1) * GROUP_HEADS)
        cols = group_cols(hg)
        q4 = qkvn_ref[rows, cols]
        k4 = qkvn_ref[rows, GDN_QK + cols.start:GDN_QK + cols.stop]
        v4 = qkvn_ref[rows, 2 * GDN_QK + cols.start:2 * GDN_QK + cols.stop]
        beta_e = lanes_of([beta_full[rows, GDN_HEADS + h:GDN_HEADS + h + 1] for h in heads], LANES)
        eg_e = lanes_of([exp_g[rows, h:h + 1] for h in heads], LANES)
        er_e = lanes_of([exp_rest[rows, h:h + 1] for h in heads], LANES)
        gi4 = lanes_of([g_cum[rows, h:h + 1] for h in heads], c)
        gj4 = jnp.concatenate([g_cum_t[h:h + 1, rows] for h in heads], axis=1)
        decay4 = jnp.exp(jnp.where(causal4, gi4 - gj4, NEG))
        kb4 = k4 * beta_e
        k_rhs = jnp.concatenate([k4.astype(BF16)] * GROUP_HEADS, axis=0) * hk_mask
        kkqk = lax.dot_general(jnp.concatenate([kb4, q4], axis=0).astype(BF16), k_rhs,
                               (((1,), (1,)), ((), ())), preferred_element_type=F32)
        a4s.append(jnp.where(strict4, kkqk[:c] * decay4, 0.0))
        qk_ref[rows, hg * gw:(hg + 1) * gw] = kkqk[c:] * decay4
        u_ref[rows, cols] = v4 * beta_e
        w_ref[rows, cols] = kb4 * eg_e
        qd_ref[rows, cols] = q4 * eg_e
        kd_ref[rows, cols] = k4 * er_e

    e4s = _unit_lower_inverse4(a4s, bd_mask)

    for (rows, hg), e4 in zip(instances, e4s):
        cols = group_cols(hg)
        vb4 = u_ref[rows, cols]
        kbe4 = w_ref[rows, cols]
        us, ws = [], []
        for i in range(GROUP_HEADS):
            hc = slice(i * LANES, (i + 1) * LANES)
            rhs = jnp.concatenate([vb4[:, hc], kbe4[:, hc]], axis=1)
            uw = rhs + _dot(e4[:, i * c:(i + 1) * c], rhs)
            us.append(uw[:, :LANES])
            ws.append(uw[:, LANES:])
        u_ref[rows, cols] = jnp.concatenate(us, axis=1)
        w_ref[rows, cols] = jnp.concatenate(ws, axis=1)

    zero_s = jnp.zeros((HEAD_DIM, HEAD_DIM), BF16)
    zero_v = jnp.zeros((c, HEAD_DIM), BF16)
    pairs = [(ch, p) for ch in range(tl // c) for p in range(GDN_HEADS // 2)]

    def pair_slices(ch, p):
        return slice(ch * c, (ch + 1) * c), slice(2 * p * LANES, (2 * p + 2) * LANES)

    wsqs = []
    for ch, p in pairs:
        rows, cols = pair_slices(ch, p)
        s_bd = jnp.concatenate(
            [jnp.concatenate([s_ref[ch, 2 * p].astype(BF16), zero_s], axis=1),
             jnp.concatenate([zero_s, s_ref[ch, 2 * p + 1].astype(BF16)], axis=1)], axis=0)
        wq = jnp.concatenate([w_ref[rows, cols], qd_ref[rows, cols]], axis=0).astype(BF16)
        wsqs.append(jnp.dot(wq, s_bd, preferred_element_type=F32))
    vnbs = []
    for (ch, p), wsq in zip(pairs, wsqs):
        rows, cols = pair_slices(ch, p)
        vnb = (u_ref[rows, cols] - wsq[:c]).astype(BF16)
        vnbs.append(vnb)
        vn_bd = jnp.concatenate([jnp.concatenate([vnb[:, :LANES], zero_v], axis=1),
                                 jnp.concatenate([zero_v, vnb[:, LANES:]], axis=1)], axis=0)
        qk2 = qk_ref[rows, 2 * p * c:(2 * p + 2) * c]
        o_ref[rows, cols] = wsq[c:] + jnp.dot(qk2.astype(BF16), vn_bd, preferred_element_type=F32)
    for (ch, p), vnb in zip(pairs, vnbs):
        rows, cols = pair_slices(ch, p)
        kd2 = kd_ref[rows, cols]
        for i in range(2):
            h = 2 * p + i
            hc = slice(i * LANES, (i + 1) * LANES)
            el = exp_last[ch * c:ch * c + 1, h:h + 1]
            s_ref[ch, h] = s_ref[ch, h] * el + _dot_tn(kd2[:, hc], vnb[:, hc])

    for h in range(GDN_HEADS):
        cols = slice(h * LANES, (h + 1) * LANES)
        o = o_ref[:, cols]
        o = o * lax.rsqrt(jnp.mean(o * o, axis=-1, keepdims=True) + RMS_EPS) * ghn_ref[...]
        o_ref[:, cols] = o * _silu(z_ref[:, :, cols].reshape(tl, LANES))
    p_a = jnp.dot(o_ref[...].astype(BF16), wproj_ref[...], preferred_element_type=F32)
    h_a = _sigmoid(gate_ref[...].reshape(tl, D_MODEL)) * p_a
    ha_ref[...] = h_a.reshape(GDN_SEQS, c, D_MODEL)

    @pl.when(step == pl.num_programs(1) - 1)
    def _():
        sfin_ref[...] = s_ref[...]


def _gdn_prompt(proj, ab, conv_wt, alog_pad, dtb_pad, ghn, w_proj_a, batch, seq):
    c = GDN_CHUNK
    ns = GDN_SEQS
    tl = ns * c
    proj3 = proj.reshape(batch, seq, MAIN_WIDTH)
    ab3 = ab.reshape(batch, seq, LANES)
    h_a, s_fin = pl.pallas_call(
        _gdn_prompt_kernel,
        grid=(batch // ns, seq // c),
        in_specs=[
            pl.BlockSpec((ns, c, CONV_DIM), lambda b, l: (b, l, COL_QKV_A // CONV_DIM)),
            pl.BlockSpec((ns, c, GDN_QK), lambda b, l: (b, l, COL_Z_A // GDN_QK)),
            pl.BlockSpec((ns, c, LANES), lambda b, l: (b, l, 0)),
            pl.BlockSpec((ns, c, D_MODEL), lambda b, l: (b, l, COL_GATE_A // D_MODEL)),
            pl.BlockSpec((CONV_WIDTH, CONV_DIM), lambda b, l: (0, 0)),
            pl.BlockSpec((1, LANES), lambda b, l: (0, 0)),
            pl.BlockSpec((1, LANES), lambda b, l: (0, 0)),
            pl.BlockSpec((1, LANES), lambda b, l: (0, 0)),
            pl.BlockSpec((GDN_QK, D_MODEL), lambda b, l: (0, 0)),
        ],
        out_specs=[
            pl.BlockSpec((ns, c, D_MODEL), lambda b, l: (b, l, 0)),
            pl.BlockSpec((ns, GDN_HEADS, HEAD_DIM, HEAD_DIM), lambda b, l: (b, 0, 0, 0)),
        ],
        out_shape=[
            jax.ShapeDtypeStruct((batch, seq, D_MODEL), F32),
            jax.ShapeDtypeStruct((batch, GDN_HEADS, HEAD_DIM, HEAD_DIM), F32),
        ],
        scratch_shapes=[
            pltpu.VMEM((ns, c + 8, CONV_DIM), F32),
            pltpu.VMEM((tl, CONV_DIM), F32),
            pltpu.VMEM((tl, GDN_QK), F32),
            pltpu.VMEM((ns, GDN_HEADS, HEAD_DIM, HEAD_DIM), F32),
            pltpu.VMEM((tl, GDN_QK), F32),
            pltpu.VMEM((tl, GDN_QK), F32),
            pltpu.VMEM((tl, GDN_QK), F32),
            pltpu.VMEM((tl, GDN_QK), F32),
            pltpu.VMEM((tl, GDN_HEADS * GDN_CHUNK), F32),
        ],
        compiler_params=pltpu.CompilerParams(
            dimension_semantics=("parallel", "arbitrary"), vmem_limit_bytes=VMEM_LIMIT),
        name="gdn_prompt",
    )(proj3, proj3, ab3, proj3, conv_wt, alog_pad, dtb_pad, ghn, w_proj_a)
    return h_a.reshape(batch * seq, D_MODEL), s_fin


def _gdn_sample_kernel(sc_q_ref, sc_k_ref, sc_v_ref, nq_ref, nk_ref, nv_ref, cwq_ref, cwk_ref, cwv_ref,
                       ab_ref, alog_ref, dtb_ref, s_ref, o_ref, snew_ref):
    h = pl.program_id(0)
    nb = nq_ref.shape[0]

    def conv_silu(sc_ref, new_ref, cw_ref):
        acc = new_ref[...] * cw_ref[3:4, :]
        for w in range(CONV_WIDTH - 1):
            acc = acc + sc_ref[w] * cw_ref[w:w + 1, :]
        return _silu(acc)

    q = conv_silu(sc_q_ref, nq_ref, cwq_ref)
    k = conv_silu(sc_k_ref, nk_ref, cwk_ref)
    v = conv_silu(sc_v_ref, nv_ref, cwv_ref)
    q = q * lax.rsqrt(jnp.sum(q * q, axis=-1, keepdims=True) + RMS_EPS) * (HEAD_DIM ** -0.5)
    k = k * lax.rsqrt(jnp.sum(k * k, axis=-1, keepdims=True) + RMS_EPS)

    ab = ab_ref[...]
    lane = lax.broadcasted_iota(jnp.int32, ab.shape, 1)
    g_full = -jnp.exp(alog_ref[...]) * _softplus(ab + dtb_ref[...])
    g = jnp.sum(jnp.where(lane == h, g_full, 0.0), axis=-1, keepdims=True)
    beta = jnp.sum(jnp.where(lane == h + GDN_HEADS, _sigmoid(ab), 0.0), axis=-1, keepdims=True)
    eg = jnp.exp(g)

    eye = (lax.broadcasted_iota(jnp.int32, (HEAD_DIM, HEAD_DIM), 0)
           == lax.broadcasted_iota(jnp.int32, (HEAD_DIM, HEAD_DIM), 1))

    def as_column(row):
        return jnp.sum(jnp.where(eye, row, 0.0), axis=1, keepdims=True)

    for b in range(nb):
        s = s_ref[b]
        k_col = as_column(k[b:b + 1, :])
        q_col = as_column(q[b:b + 1, :])
        e = eg[b:b + 1, :]
        ks = jnp.sum(k_col * s, axis=0, keepdims=True)
        v_new = beta[b:b + 1, :] * (v[b:b + 1, :] - e * ks)
        s_new = s * e + k_col * v_new
        snew_ref[b] = s_new
        o_ref[b:b + 1, :] = jnp.sum(q_col * s_new, axis=0, keepdims=True)


def _gdn_sample(state_conv_t, proj, ab, conv_wt, alog_pad, dtb_pad, state, nb):
    batch = proj.shape[0]
    hq = GDN_QK // LANES
    return pl.pallas_call(
        _gdn_sample_kernel,
        grid=(GDN_HEADS, batch // nb),
        in_specs=[
            pl.BlockSpec((CONV_WIDTH - 1, nb, LANES), lambda h, i: (0, i, h)),
            pl.BlockSpec((CONV_WIDTH - 1, nb, LANES), lambda h, i: (0, i, hq + h)),
            pl.BlockSpec((CONV_WIDTH - 1, nb, LANES), lambda h, i: (0, i, 2 * hq + h)),
            pl.BlockSpec((nb, LANES), lambda h, i: (i, h)),
            pl.BlockSpec((nb, LANES), lambda h, i: (i, hq + h)),
            pl.BlockSpec((nb, LANES), lambda h, i: (i, 2 * hq + h)),
            pl.BlockSpec((CONV_WIDTH, LANES), lambda h, i: (0, h)),
            pl.BlockSpec((CONV_WIDTH, LANES), lambda h, i: (0, hq + h)),
            pl.BlockSpec((CONV_WIDTH, LANES), lambda h, i: (0, 2 * hq + h)),
            pl.BlockSpec((nb, LANES), lambda h, i: (i, 0)),
            pl.BlockSpec((1, LANES), lambda h, i: (0, 0)),
            pl.BlockSpec((1, LANES), lambda h, i: (0, 0)),
            pl.BlockSpec((nb, None, HEAD_DIM, HEAD_DIM), lambda h, i: (i, h, 0, 0)),
        ],
        out_specs=[
            pl.BlockSpec((nb, LANES), lambda h, i: (i, h)),
            pl.BlockSpec((nb, None, HEAD_DIM, HEAD_DIM), lambda h, i: (i, h, 0, 0)),
        ],
        out_shape=[
            jax.ShapeDtypeStruct((batch, GDN_QK), F32),
            jax.ShapeDtypeStruct(state.shape, F32),
        ],
        compiler_params=pltpu.CompilerParams(
            dimension_semantics=("parallel", "parallel"), vmem_limit_bytes=VMEM_LIMIT),
        name="gdn_sample",
    )(state_conv_t, state_conv_t, state_conv_t, proj, proj, proj, conv_wt, conv_wt, conv_wt,
      ab, alog_pad, dtb_pad, state)


def _gdn_sample_out_kernel(o_ref, z_ref, gate_ref, ghn_ref, wproj_ref, ha_ref, y_ref):
    for h in range(GDN_HEADS):
        cols = slice(h * LANES, (h + 1) * LANES)
        o = o_ref[:, cols]
        o = o * lax.rsqrt(jnp.mean(o * o, axis=-1, keepdims=True) + RMS_EPS) * ghn_ref[...]
        y_ref[:, cols] = o * _silu(z_ref[:, cols])
    p_a = jnp.dot(y_ref[...].astype(BF16), wproj_ref[...], preferred_element_type=F32)
    ha_ref[...] = _sigmoid(gate_ref[...]) * p_a


def _gdn_sample_out(o, proj, ghn, w_proj_a):
    batch = o.shape[0]
    return pl.pallas_call(
        _gdn_sample_out_kernel,
        grid=(1,),
        in_specs=[
            pl.BlockSpec((batch, GDN_QK), lambda i: (0, 0)),
            pl.BlockSpec((batch, GDN_QK), lambda i: (0, COL_Z_A // GDN_QK)),
            pl.BlockSpec((batch, D_MODEL), lambda i: (0, COL_GATE_A // D_MODEL)),
            pl.BlockSpec((1, LANES), lambda i: (0, 0)),
            pl.BlockSpec((GDN_QK, D_MODEL), lambda i: (0, 0)),
        ],
        out_specs=pl.BlockSpec((batch, D_MODEL), lambda i: (0, 0)),
        out_shape=jax.ShapeDtypeStruct((batch, D_MODEL), F32),
        scratch_shapes=[pltpu.VMEM((batch, GDN_QK), F32)],
        compiler_params=pltpu.CompilerParams(vmem_limit_bytes=VMEM_LIMIT),
        name="gdn_sample_out",
    )(o, proj, proj, ghn, w_proj_a)


def _attn_prompt_kernel(*refs):
    ins = refs[:5 * N_GROUPS]
    bias_ref = refs[5 * N_GROUPS]
    o_ref = refs[5 * N_GROUPS + 1]
    acc_ref, m_ref, l_ref, bias_start_ref, split_ref = refs[5 * N_GROUPS + 2:]
    tq = ATTN_TILE
    first_tile = pl.program_id(1) == 0
    scale = (HEAD_DIM ** -0.5) * LOG2_E
    lane2 = lax.broadcasted_iota(jnp.int32, (1, 2 * WIN_BLK), 1)
    start_mask = jnp.where(first_tile & (lane2 < WIN_BLK), NEG, 0.0).astype(F32)
    for gi in range(N_GROUPS):
        bias_start_ref[gi] = bias_ref[gi] + start_mask

    def strided(start, n, stride):
        return pl.ds(start, n, stride=stride) if stride > 1 else pl.ds(start, n)

    for gi, (_, dil) in enumerate(DIL_GROUPS):
        srcs = ins[5 * gi:5 * gi + 5]
        n_blocks = tq // (dil * WIN_BLK)
        inner = min(dil, MAX_ROW_STRIDE)
        outer = dil // inner
        if outer > 1:
            for a, src in enumerate(srcs):
                n = src.shape[0]
                for p in range(outer):
                    split_ref[a, p * (n // outer):(p + 1) * (n // outer), :] = src[strided(p, n // outer, outer), :]
            srcs = [split_ref.at[a] for a in range(5)]
        q_ref, kp_ref, kc_ref, vp_ref, vc_ref = srcs

        def rows_of(ref_rows, r, first, count, inner=inner, outer=outer):
            if outer == 1:
                return strided(r + inner * first, count, inner)
            part = r % outer
            return strided(part * (ref_rows // outer) + r // outer + inner * first, count, inner)

        def run_units(units, q_ref=q_ref, kp_ref=kp_ref, kc_ref=kc_ref, vp_ref=vp_ref, vc_ref=vc_ref,
                      gi=gi, rows_of=rows_of):
            span = kp_ref.shape[0]
            qs, kks, vvs = [], [], []
            for r, j in units:
                qs.append((q_ref[rows_of(tq, r, WIN_BLK * j, WIN_BLK), :] * scale).astype(BF16))
                if j == 0:
                    prev = rows_of(span, r, 0, WIN_BLK)
                    cur = rows_of(tq, r, 0, WIN_BLK)
                    kks.append(jnp.concatenate([kp_ref[prev, :], kc_ref[cur, :]], axis=0).astype(BF16))
                    vvs.append(jnp.concatenate([vp_ref[prev, :], vc_ref[cur, :]], axis=0).astype(BF16))
                else:
                    rows = rows_of(tq, r, WIN_BLK * (j - 1), 2 * WIN_BLK)
                    kks.append(kc_ref[rows, :].astype(BF16))
                    vvs.append(vc_ref[rows, :].astype(BF16))
            ss = [lax.dot_general(q, kk, (((1,), (1,)), ((), ())), preferred_element_type=F32)
                  for q, kk in zip(qs, kks)]
            ps, ms, ls = [], [], []
            for (r, j), s in zip(units, ss):
                s = s + (bias_start_ref[gi] if j == 0 else bias_ref[gi])
                m = jnp.max(s, axis=-1, keepdims=True)
                p = jnp.exp2(s - m)
                ms.append(m)
                ls.append(jnp.sum(p, axis=-1, keepdims=True))
                ps.append(p.astype(BF16))
            accs = [jnp.dot(p, vv, preferred_element_type=F32) for p, vv in zip(ps, vvs)]
            for (r, j), acc, m, l in zip(units, accs, ms, ls):
                q_rows = rows_of(tq, r, WIN_BLK * j, WIN_BLK)
                acc_ref[gi, q_rows, :] = acc
                m_ref[gi, q_rows, :] = jnp.broadcast_to(m, (WIN_BLK, LANES))
                l_ref[gi, q_rows, :] = jnp.broadcast_to(l, (WIN_BLK, LANES))

        if dil == 1:
            for j0 in range(0, n_blocks, ATTN_UNITS):
                run_units([(0, j) for j in range(j0, min(j0 + ATTN_UNITS, n_blocks))])
        else:
            jb = min(n_blocks, ATTN_UNITS)
            rb = max(ATTN_UNITS // n_blocks, 1)
            for j0 in range(0, n_blocks, jb):
                def body(it, carry, j0=j0, jb=jb, rb=rb, run_units=run_units):
                    run_units([(it * rb + k, j) for k in range(rb) for j in range(j0, j0 + jb)])
                    return carry
                lax.fori_loop(0, dil // rb, body, 0)

    outers = [dil // min(dil, MAX_ROW_STRIDE) for _, dil in DIL_GROUPS]
    parts = max(outers)
    cnt = 2 * WIN_BLK
    for p in range(parts):
        for i0 in range(0, tq // parts, cnt):
            def rows_in(gi):
                if outers[gi] == parts:
                    return pl.ds(p * (tq // parts) + i0, cnt)
                assert outers[gi] == 1
                return strided(p + parts * i0, cnt, parts)
            m_all = jnp.maximum(jnp.maximum(m_ref[0, rows_in(0), :], m_ref[1, rows_in(1), :]),
                                m_ref[2, rows_in(2), :])
            num = jnp.zeros((cnt, LANES), F32)
            den = jnp.zeros((cnt, LANES), F32)
            for gi in range(N_GROUPS):
                wgt = jnp.exp2(m_ref[gi, rows_in(gi), :] - m_all)
                num = num + wgt * acc_ref[gi, rows_in(gi), :]
                den = den + wgt * l_ref[gi, rows_in(gi), :]
            o_ref[strided(p + parts * i0, cnt, parts), :] = num / den


def _attn_prompt(proj, bias_tiles, batch, seq):
    tq = ATTN_TILE
    nt = seq // tq
    in_specs = []
    args = []
    for gi, (_, dil) in enumerate(DIL_GROUPS):
        span = dil * WIN_BLK
        cq = (COL_QKV_B + gi * 3 * DIL_W) // LANES
        ck = cq + DIL_HEADS
        cv = ck + DIL_HEADS

        def cur(col):
            return pl.BlockSpec((tq, LANES), lambda b, i, h, col=col: (b * nt + i, col + h))

        def prev(col, span=span):
            per = tq // span
            return pl.BlockSpec(
                (span, LANES),
                lambda b, i, h, col=col, per=per: (jnp.maximum((b * nt + i) * per - 1, 0), col + h))

        in_specs += [cur(cq), prev(ck), cur(ck), prev(cv), cur(cv)]
        args += [proj] * 5
    in_specs.append(pl.BlockSpec((N_GROUPS, None, WIN_BLK, 2 * WIN_BLK), lambda b, i, h: (0, h, 0, 0)))
    args.append(bias_tiles)
    return pl.pallas_call(
        _attn_prompt_kernel,
        grid=(batch, nt, DIL_HEADS),
        in_specs=in_specs,
        out_specs=pl.BlockSpec((tq, LANES), lambda b, i, h: (b * nt + i, h)),
        out_shape=jax.ShapeDtypeStruct((batch * seq, DIL_W), F32),
        scratch_shapes=[pltpu.VMEM((N_GROUPS, tq, LANES), F32)] * 3
        + [pltpu.VMEM((N_GROUPS, WIN_BLK, 2 * WIN_BLK), F32), pltpu.VMEM((5, tq, LANES), F32)],
        compiler_params=pltpu.CompilerParams(
            dimension_semantics=("parallel", "parallel", "parallel"), vmem_limit_bytes=VMEM_LIMIT),
        name="attn_prompt",
    )(*args)


def _attn_sample_kernel(*refs):
    q_refs = refs[0:N_GROUPS]
    kn_refs = refs[N_GROUPS:2 * N_GROUPS]
    vn_refs = refs[2 * N_GROUPS:3 * N_GROUPS]
    kc_refs = refs[3 * N_GROUPS:4 * N_GROUPS]
    vc_refs = refs[4 * N_GROUPS:5 * N_GROUPS]
    bias0_ref, bias_ref, o_ref = refs[5 * N_GROUPS:5 * N_GROUPS + 3]
    m_ref, l_ref, acc_ref = refs[5 * N_GROUPS + 3:]
    t = pl.program_id(0)
    scale = HEAD_DIM ** -0.5

    @pl.when(t == 0)
    def _():
        for gi in range(N_GROUPS):
            s = jnp.sum(q_refs[gi][...] * kn_refs[gi][...], axis=-1, keepdims=True) * scale + bias0_ref[gi]
            m_ref[gi] = s
            l_ref[gi] = jnp.ones_like(s)
            acc_ref[gi] = vn_refs[gi][...]

    for gi in range(N_GROUPS):
        s = jnp.sum(q_refs[gi][...] * kc_refs[gi][...], axis=-1, keepdims=True) * scale + bias_ref[gi]
        m_old = m_ref[gi]
        m_new = jnp.maximum(m_old, s)
        alpha = jnp.exp(m_old - m_new)
        p = jnp.exp(s - m_new)
        l_ref[gi] = alpha * l_ref[gi] + p
        acc_ref[gi] = alpha * acc_ref[gi] + p * vc_refs[gi][...]
        m_ref[gi] = m_new

    @pl.when(t == pl.num_programs(0) - 1)
    def _():
        m_all = jnp.maximum(jnp.maximum(m_ref[0], m_ref[1]), m_ref[2])
        num = jnp.zeros(acc_ref.shape[1:], F32)
        den = jnp.zeros(m_ref.shape[1:], F32)
        for gi in range(N_GROUPS):
            wgt = jnp.exp(m_ref[gi] - m_all)
            num = num + wgt * acc_ref[gi]
            den = den + wgt * l_ref[gi]
        o_ref[...] = num / den


def _attn_sample(q_g, kn_g, vn_g, k_caches, v_caches, bias0, bias_steps):
    batch = q_g[0].shape[0]
    small = pl.BlockSpec((batch, DIL_HEADS, HEAD_DIM), lambda t: (0, 0, 0))
    in_specs = [small] * (3 * N_GROUPS)
    for _ in range(2):
        for _, dil in DIL_GROUPS:
            in_specs.append(pl.BlockSpec((None, batch, None, DIL_HEADS, HEAD_DIM),
                                         lambda t, dil=dil: (0, 0, t * dil, 0, 0)))
    in_specs.append(pl.BlockSpec((N_GROUPS, 1, DIL_HEADS, 1), lambda t: (0, 0, 0, 0)))
    in_specs.append(pl.BlockSpec((None, N_GROUPS, 1, DIL_HEADS, 1), lambda t: (t, 0, 0, 0, 0)))
    return pl.pallas_call(
        _attn_sample_kernel,
        grid=(WIN_BLK,),
        in_specs=in_specs,
        out_specs=pl.BlockSpec((batch, DIL_HEADS, HEAD_DIM), lambda t: (0, 0, 0)),
        out_shape=jax.ShapeDtypeStruct((batch, DIL_HEADS, HEAD_DIM), F32),
        scratch_shapes=[
            pltpu.VMEM((N_GROUPS, batch, DIL_HEADS, 1), F32),
            pltpu.VMEM((N_GROUPS, batch, DIL_HEADS, 1), F32),
            pltpu.VMEM((N_GROUPS, batch, DIL_HEADS, HEAD_DIM), F32),
        ],
        compiler_params=pltpu.CompilerParams(
            dimension_semantics=("arbitrary",), vmem_limit_bytes=VMEM_LIMIT),
        name="attn_sample",
    )(*q_g, *kn_g, *vn_g, *k_caches, *v_caches, bias0, bias_steps)


def _output_kernel(ob_ref, zb_ref, gateb_ref, ha_ref, x_ref, wpb_ref, wout_ref, gpost_ref, y_ref):
    yb = (ob_ref[...] * _silu(zb_ref[...])).astype(BF16)
    p_b = jnp.dot(yb, wpb_ref[...], preferred_element_type=F32)
    hid = ha_ref[...] + _sigmoid(gateb_ref[...]) * p_b
    out = jnp.dot(hid.astype(BF16), wout_ref[...], preferred_element_type=F32)
    ms = jnp.mean(out * out, axis=-1, keepdims=True)
    y_ref[...] = x_ref[...] + out * lax.rsqrt(ms + RMS_EPS) * gpost_ref[...]


def _output(o_b, proj, h_a, x2d, w_proj_b, w_out, g_post, tm):
    m = x2d.shape[0]
    return pl.pallas_call(
        _output_kernel,
        grid=(m // tm,),
        in_specs=[
            pl.BlockSpec((tm, DIL_W), lambda i: (i, 0)),
            pl.BlockSpec((tm, DIL_W), lambda i: (i, COL_Z_B // DIL_W)),
            pl.BlockSpec((tm, D_MODEL), lambda i: (i, COL_GATE_B // D_MODEL)),
            pl.BlockSpec((tm, D_MODEL), lambda i: (i, 0)),
            pl.BlockSpec((tm, D_MODEL), lambda i: (i, 0)),
            pl.BlockSpec((DIL_W, D_MODEL), lambda i: (0, 0)),
            pl.BlockSpec((D_MODEL, D_MODEL), lambda i: (0, 0)),
            pl.BlockSpec((1, D_MODEL), lambda i: (0, 0)),
        ],
        out_specs=pl.BlockSpec((tm, D_MODEL), lambda i: (i, 0)),
        out_shape=jax.ShapeDtypeStruct((m, D_MODEL), F32),
        compiler_params=pltpu.CompilerParams(
            dimension_semantics=("parallel",), vmem_limit_bytes=VMEM_LIMIT),
        name="output",
    )(o_b, proj, proj, h_a, x2d, w_proj_b, w_out, g_post)


def _rel_bucket(dist):
    max_exact = REL_BUCKETS // 2
    d = jnp.maximum(dist, 1).astype(F32)
    large = max_exact + (jnp.log(d / max_exact) / math.log(REL_MAX_DIST / max_exact)
                         * (REL_BUCKETS - max_exact)).astype(jnp.int32)
    large = jnp.minimum(large, REL_BUCKETS - 1)
    return jnp.where(dist < max_exact, dist, large)


def _group_biases(rel_table):
    out = []
    for gi, (win, dil) in enumerate(DIL_GROUPS):
        dist = jnp.arange(win // dil + 1, dtype=jnp.int32) * dil
        b = rel_table[_rel_bucket(dist)][:, gi * DIL_HEADS:(gi + 1) * DIL_HEADS]
        out.append(b.T.astype(F32))
    return out


def _bias_tiles(biases):
    period = 3 * WIN_BLK
    tiles = []
    for b in biases:
        u = jnp.concatenate([b[:, ::-1], jnp.full((DIL_HEADS, period - WIN_BLK - 1), NEG, F32)], axis=1)
        flat = jnp.tile(u, (1, WIN_BLK))[:, :WIN_BLK * (period - 1)]
        tiles.append(flat.reshape(DIL_HEADS, WIN_BLK, period - 1)[:, :, :2 * WIN_BLK])
    return jnp.stack(tiles, axis=0).astype(F32)


def _layer_weights(w_in, conv_w, a_log, dt_bias, g_head_norm, w_proj_a, w_proj_b, w_out):
    w_main = jnp.concatenate([w_in[:, :AB_OFFSET], w_in[:, AB_OFFSET + 2 * GDN_HEADS:]], axis=1).astype(BF16)
    w_ab = jnp.pad(w_in[:, AB_OFFSET:AB_OFFSET + 2 * GDN_HEADS],
                   ((0, 0), (0, LANES - 2 * GDN_HEADS))).astype(BF16)
    pad = (0, LANES - GDN_HEADS)
    return dict(
        w_main=w_main, w_ab=w_ab, conv_wt=conv_w.T.astype(F32),
        alog_pad=jnp.pad(a_log.astype(F32), pad)[None, :],
        dtb_pad=jnp.pad(dt_bias.astype(F32), pad)[None, :],
        ghn=g_head_norm.astype(F32)[None, :],
        w_proj_a=w_proj_a.astype(BF16), w_proj_b=w_proj_b.astype(BF16), w_out=w_out.astype(BF16))


def _split_heads(a2d, col, rows_per_batch, batch, keep):
    a = a2d.reshape(batch, rows_per_batch, -1)[:, rows_per_batch - keep:, col:col + DIL_W]
    return a.reshape(batch, keep, DIL_HEADS, HEAD_DIM)


def kernel(x_prompt, x_sample, state_gdn, state_conv, cache_k_w128, cache_v_w128, cache_k_w512,
           cache_v_w512, cache_k_w2048, cache_v_w2048, rel_table, g_pre, w_in, conv_w, a_log, dt_bias,
           g_head_norm, w_proj_a, w_proj_b, w_out, g_post):
    batch, seq, _ = x_prompt.shape
    dec_batch = x_sample.shape[0]
    k_caches = (cache_k_w128, cache_k_w512, cache_k_w2048)
    v_caches = (cache_v_w128, cache_v_w512, cache_v_w2048)
    biases = _group_biases(rel_table)
    bias_tiles = _bias_tiles(biases) * LOG2_E
    lw = _layer_weights(w_in[0], conv_w[0], a_log[0], dt_bias[0], g_head_norm[0], w_proj_a[0],
                        w_proj_b[0], w_out[0])
    g_pre_l = g_pre[0][None, :]
    g_post_l = g_post[0][None, :]

    xp = x_prompt.reshape(batch * seq, D_MODEL)
    proj, ab = _in_projection(xp, g_pre_l, lw["w_main"], lw["w_ab"], tm=1024, tn=1408)
    h_a, p_gdn = _gdn_prompt(proj, ab, lw["conv_wt"], lw["alog_pad"], lw["dtb_pad"], lw["ghn"],
                             lw["w_proj_a"], batch, seq)
    o_b = _attn_prompt(proj, bias_tiles, batch, seq)
    y_prompt = _output(o_b, proj, h_a, xp, lw["w_proj_b"], lw["w_out"], g_post_l, tm=512)
    y_prompt = y_prompt.reshape(batch, seq, D_MODEL)
    p_conv = proj.reshape(batch, seq, MAIN_WIDTH)[:, seq - (CONV_WIDTH - 1):, :CONV_DIM]
    p_rows = []
    for gi, (win, _) in enumerate(DIL_GROUPS):
        keep = min(win, seq)
        ck = COL_QKV_B + gi * 3 * DIL_W + DIL_W
        p_rows += [_split_heads(proj, ck, seq, batch, keep), _split_heads(proj, ck + DIL_W, seq, batch, keep)]

    xs = x_sample.reshape(dec_batch, D_MODEL)
    proj_s, ab_s = _in_projection(xs, g_pre_l, lw["w_main"], lw["w_ab"], tm=dec_batch, tn=1408)
    sc_t = jnp.transpose(state_conv[0], (1, 0, 2))
    o_s, s_gdn = _gdn_sample(sc_t, proj_s, ab_s, lw["conv_wt"], lw["alog_pad"], lw["dtb_pad"],
                             state_gdn[0], nb=32)
    h_a_s = _gdn_sample_out(o_s, proj_s, lw["ghn"], lw["w_proj_a"])
    q_g, kn_g, vn_g = [], [], []
    for gi in range(N_GROUPS):
        cq = COL_QKV_B + gi * 3 * DIL_W
        q_g.append(proj_s[:, cq:cq + DIL_W].reshape(dec_batch, DIL_HEADS, HEAD_DIM))
        kn_g.append(proj_s[:, cq + DIL_W:cq + 2 * DIL_W].reshape(dec_batch, DIL_HEADS, HEAD_DIM))
        vn_g.append(proj_s[:, cq + 2 * DIL_W:cq + 3 * DIL_W].reshape(dec_batch, DIL_HEADS, HEAD_DIM))
    bias_all = jnp.stack(biases, axis=0)
    bias0 = bias_all[:, None, :, 0:1]
    bias_steps = jnp.transpose(bias_all[:, :, :0:-1], (2, 0, 1))[:, :, None, :, None]
    o_b_s = _attn_sample(q_g, kn_g, vn_g, k_caches, v_caches, bias0, bias_steps)
    y_sample = _output(o_b_s.reshape(dec_batch, DIL_W), proj_s, h_a_s, xs, lw["w_proj_b"], lw["w_out"],
                       g_post_l, tm=dec_batch)
    y_sample = y_sample.reshape(dec_batch, 1, D_MODEL)
    s_conv = jnp.concatenate([state_conv[0][:, 1:, :], proj_s[:, None, :CONV_DIM]], axis=1)
    s_rows = []
    for gi in range(N_GROUPS):
        s_rows += [kn_g[gi][:, None], vn_g[gi][:, None]]

    return (y_prompt, y_sample, p_gdn[None], p_conv[None], *[r[None] for r in p_rows],
            s_gdn[None], s_conv[None], *[r[None] for r in s_rows])
```

```python
import functools
import math

import jax
import jax.numpy as jnp
from jax import lax
from jax.experimental import pallas as pl
from jax.experimental.pallas import tpu as pltpu

F32 = jnp.float32
BF16 = jnp.bfloat16

D_MODEL = 1024
GDN_HEADS = 8
HEAD_DIM = 128
GDN_QK = GDN_HEADS * HEAD_DIM
CONV_DIM = 3 * GDN_QK
CONV_WIDTH = 4
DIL_GROUPS = ((128, 1), (512, 4), (2048, 16))
N_GROUPS = 3
DIL_HEADS = 4
DIL_W = DIL_HEADS * HEAD_DIM
WIN_BLK = 128
REL_BUCKETS = 32
REL_MAX_DIST = 2048
RMS_EPS = 1e-6
NEG = -1e30
LOG2_E = math.log2(math.e)

COL_QKV_A = 0
COL_Z_A = 3072
COL_QKV_B = 4096
COL_Z_B = 8704
COL_GATE_A = 9216
COL_GATE_B = 10240
MAIN_WIDTH = 11264
AB_OFFSET = 4096
LANES = 128

VMEM_LIMIT = 56 * 1024 * 1024

GDN_CHUNK = 64
GDN_SEQS = 4
GDN_SKEW = 0
ATTN_TILE = 2048
ATTN_UNITS = 8
MAX_ROW_STRIDE = 4


def _silu(x):
    half = 0.5 * x
    return half * jnp.tanh(half) + half


def _sigmoid(x):
    return 1.0 / (1.0 + jnp.exp(-x))


def _softplus(x):
    return jnp.maximum(x, 0.0) + jnp.log(1.0 + jnp.exp(-jnp.abs(x)))


def _dot(a, b):
    return jnp.dot(a.astype(BF16), b.astype(BF16), preferred_element_type=F32)


def _dot_nt(a, b):
    return lax.dot_general(a.astype(BF16), b.astype(BF16), (((1,), (1,)), ((), ())),
                           preferred_element_type=F32)


def _dot_tn(a, b):
    return lax.dot_general(a.astype(BF16), b.astype(BF16), (((0,), (0,)), ((), ())),
                           preferred_element_type=F32)


def _dot_f32(a, b):
    return jnp.dot(a, b, preferred_element_type=F32, precision=lax.Precision.HIGHEST)


def _inproj_kernel(x_ref, g_ref, w_ref, wab_ref, o_ref, ab_ref, xn_ref):
    @pl.when(pl.program_id(1) == 0)
    def _():
        x = x_ref[...]
        ms = jnp.mean(x * x, axis=-1, keepdims=True)
        xn = (x * lax.rsqrt(ms + RMS_EPS) * g_ref[...]).astype(BF16)
        xn_ref[...] = xn
        ab_ref[...] = jnp.dot(xn, wab_ref[...], preferred_element_type=F32)

    o_ref[...] = jnp.dot(xn_ref[...], w_ref[...], preferred_element_type=F32)


def _in_projection(x2d, g_pre, w_main, w_ab, tm, tn):
    m = x2d.shape[0]
    return pl.pallas_call(
        _inproj_kernel,
        grid=(m // tm, MAIN_WIDTH // tn),
        in_specs=[
            pl.BlockSpec((tm, D_MODEL), lambda i, j: (i, 0)),
            pl.BlockSpec((1, D_MODEL), lambda i, j: (0, 0)),
            pl.BlockSpec((D_MODEL, tn), lambda i, j: (0, j)),
            pl.BlockSpec((D_MODEL, LANES), lambda i, j: (0, 0)),
        ],
        out_specs=[
            pl.BlockSpec((tm, tn), lambda i, j: (i, j)),
            pl.BlockSpec((tm, LANES), lambda i, j: (i, 0)),
        ],
        out_shape=[
            jax.ShapeDtypeStruct((m, MAIN_WIDTH), F32),
            jax.ShapeDtypeStruct((m, LANES), F32),
        ],
        scratch_shapes=[pltpu.VMEM((tm, D_MODEL), BF16)],
        compiler_params=pltpu.CompilerParams(
            dimension_semantics=("parallel", "arbitrary"), vmem_limit_bytes=VMEM_LIMIT),
        name="in_projection",
    )(x2d, g_pre, w_main, w_ab)


GROUP_HEADS = 4


def _block_diag_rhs(x4, half_masks):
    c = x4.shape[0]
    zeros = jnp.zeros((c, LANES), x4.dtype)
    row_blocks = []
    for h in range(GROUP_HEADS):
        piece = x4[:, (h // 2) * LANES:(h // 2 + 1) * LANES] * half_masks[h % 2]
        row_blocks.append(jnp.concatenate([piece if v == h // 2 else zeros for v in range(2)], axis=1))
    return jnp.concatenate(row_blocks, axis=0)


def _gdn_prompt_kernel(qkv_ref, z_ref, ab_ref, gate_ref, convw_ref, alog_ref, dtb_ref, ghn_ref,
                       wproj_ref, ha_ref, sfin_ref, xpad_ref, qkvn_ref, o_ref, s_ref,
                       u_ref, w_ref, qd_ref, kd_ref, qk_ref):
    c = GDN_CHUNK
    tl = GDN_SEQS * c
    gw = GROUP_HEADS * c
    step = pl.program_id(1)

    @pl.when(step == 0)
    def _():
        xpad_ref[:, :, 0:8, :] = jnp.zeros((GDN_SEQS, CONV_DIM // LANES, 8, LANES), F32)
        s_ref[...] = jnp.zeros_like(s_ref)

    def conv_stage(i):
        for blk in range(CONV_DIM // LANES):
            cols = slice(blk * LANES, (blk + 1) * LANES)
            xpad_ref[i, blk, 8:8 + c, :] = qkv_ref[i, :, cols]
            for phase in range(2):
                acc = None
                for w in range(CONV_WIDTH):
                    tap = xpad_ref[i, blk, pl.ds(5 + phase + w, c // 2, stride=2), :] * convw_ref[w:w + 1, cols]
                    acc = tap if acc is None else acc + tap
                y = _silu(acc)
                if blk < 2 * GDN_HEADS:
                    y = y * lax.rsqrt(jnp.sum(y * y, axis=-1, keepdims=True) + RMS_EPS)
                    if blk < GDN_HEADS:
                        y = y * (HEAD_DIM ** -0.5)
                qkvn_ref[blk, pl.ds(i * c + phase, c // 2, stride=2), :] = y
            xpad_ref[i, blk, 5:8, :] = xpad_ref[i, blk, c + 5:c + 8, :]

    ab = ab_ref[...].reshape(tl, LANES)
    g_full = -jnp.exp(alog_ref[...]) * _softplus(ab + dtb_ref[...])
    beta_full = _sigmoid(ab)
    row = lax.broadcasted_iota(jnp.int32, (tl, tl), 0)
    col = lax.broadcasted_iota(jnp.int32, (tl, tl), 1)
    same_chunk = (row // c) == (col // c)
    tri = jnp.where(same_chunk & (col <= row), 1.0, 0.0).astype(F32)
    g_cum = _dot_f32(tri, g_full)
    g_cum_t = g_cum.T
    last = jnp.where(same_chunk & ((col % c) == c - 1), 1.0, 0.0).astype(F32)
    g_last = _dot_f32(last, g_cum)
    exp_g = jnp.exp(g_cum)
    exp_rest = jnp.exp(g_last - g_cum)
    exp_last = jnp.exp(g_last)

    ci = lax.broadcasted_iota(jnp.int32, (c, gw), 0)
    cj = lax.broadcasted_iota(jnp.int32, (c, gw), 1) % c
    causal4 = cj <= ci
    strict4 = cj < ci
    half_lane = lax.broadcasted_iota(jnp.int32, (c, LANES), 1) // c
    half_masks = [jnp.where(half_lane == k, 1.0, 0.0).astype(BF16) for k in range(2)]

    def lanes_of(col_values, width):
        return jnp.concatenate([jnp.broadcast_to(v, (c, width)) for v in col_values], axis=1)

    n_groups = GDN_HEADS // GROUP_HEADS
    zero_s = jnp.zeros((HEAD_DIM, HEAD_DIM), BF16)
    zero_v = jnp.zeros((c, HEAD_DIM), BF16)

    def group_cols(hg):
        return slice(hg * GROUP_HEADS * LANES, (hg + 1) * GROUP_HEADS * LANES)

    def bd_dot(lhs, x4):
        return jnp.dot(lhs.astype(BF16), _block_diag_rhs(x4.astype(BF16), half_masks),
                       preferred_element_type=F32)

    def seq_stages(i):
        rows = slice(i * c, (i + 1) * c)
        st = {}

        def build():
            a4s = []
            for hg in range(n_groups):
                heads = range(hg * GROUP_HEADS, (hg + 1) * GROUP_HEADS)
                cols = group_cols(hg)

                def head_blocks(first_blk):
                    return jnp.concatenate([qkvn_ref[first_blk + h, rows, :] for h in heads], axis=1)

                q4 = head_blocks(0)
                k4 = head_blocks(GDN_HEADS)
                v4 = head_blocks(2 * GDN_HEADS)
                beta_e = lanes_of([beta_full[rows, GDN_HEADS + h:GDN_HEADS + h + 1] for h in heads], LANES)
                eg_e = lanes_of([exp_g[rows, h:h + 1] for h in heads], LANES)
                er_e = lanes_of([exp_rest[rows, h:h + 1] for h in heads], LANES)
                gi4 = lanes_of([g_cum[rows, h:h + 1] for h in heads], c)
                gj4 = jnp.concatenate([g_cum_t[h:h + 1, rows] for h in heads], axis=1)
                decay4 = jnp.exp(jnp.where(causal4, gi4 - gj4, NEG))
                kb4 = k4 * beta_e
                k4b = k4.astype(BF16)
                k_rhs = jnp.concatenate(
                    [jnp.concatenate([k4b[:, g * LANES:(g + 1) * LANES] if g == h else zero_v
                                      for g in range(GROUP_HEADS)], axis=1)
                     for h in range(GROUP_HEADS)], axis=0)
                kkqk = lax.dot_general(jnp.concatenate([kb4, q4], axis=0).astype(BF16), k_rhs,
                                       (((1,), (1,)), ((), ())), preferred_element_type=F32)
                a4s.append(jnp.where(strict4, kkqk[:c] * decay4, 0.0))
                qk_ref[rows, hg * gw:(hg + 1) * gw] = kkqk[c:] * decay4
                u_ref[rows, cols] = v4 * beta_e
                w_ref[rows, cols] = kb4 * eg_e
                qd_ref[rows, cols] = q4 * eg_e
                kd_ref[rows, cols] = k4 * er_e
            st["a"] = a4s

        def square():
            st["e"] = [-a4 for a4 in st["a"]]
            st["x"] = [bd_dot(a4, a4) for a4 in st["a"]]

        def double():
            boths = [bd_dot(jnp.concatenate([e, x], axis=0), x) for e, x in zip(st["e"], st["x"])]
            st["e"] = [e + x + both[:c] for e, x, both in zip(st["e"], st["x"], boths)]
            st["x"] = [both[c:] for both in boths]

        def last_factor():
            st["e"] = [e + x + bd_dot(e, x) for e, x in zip(st["e"], st["x"])]

        def solve():
            for hg, e4 in enumerate(st["e"]):
                cols = group_cols(hg)
                vb4 = u_ref[rows, cols]
                kbe4 = w_ref[rows, cols]
                us, ws = [], []
                for k in range(GROUP_HEADS):
                    hc = slice(k * LANES, (k + 1) * LANES)
                    rhs = jnp.concatenate([vb4[:, hc], kbe4[:, hc]], axis=1)
                    uw = rhs + _dot(e4[:, k * c:(k + 1) * c], rhs)
                    us.append(uw[:, :LANES])
                    ws.append(uw[:, LANES:])
                u_ref[rows, cols] = jnp.concatenate(us, axis=1)
                w_ref[rows, cols] = jnp.concatenate(ws, axis=1)

        def pair_cols(p):
            return slice(2 * p * LANES, (2 * p + 2) * LANES)

        def state_products():
            wsqs = []
            for p in range(GDN_HEADS // 2):
                cols = pair_cols(p)
                s_bd = jnp.concatenate(
                    [jnp.concatenate([s_ref[i, 2 * p].astype(BF16), zero_s], axis=1),
                     jnp.concatenate([zero_s, s_ref[i, 2 * p + 1].astype(BF16)], axis=1)], axis=0)
                wq = jnp.concatenate([w_ref[rows, cols], qd_ref[rows, cols]], axis=0).astype(BF16)
                wsqs.append(jnp.dot(wq, s_bd, preferred_element_type=F32))
            st["wsq"] = wsqs

        def outputs():
            vnbs = []
            for p, wsq in enumerate(st["wsq"]):
                cols = pair_cols(p)
                vnb = (u_ref[rows, cols] - wsq[:c]).astype(BF16)
                vnbs.append(vnb)
                vn_bd = jnp.concatenate([jnp.concatenate([vnb[:, :LANES], zero_v], axis=1),
                                         jnp.concatenate([zero_v, vnb[:, LANES:]], axis=1)], axis=0)
                qk2 = qk_ref[rows, 2 * p * c:(2 * p + 2) * c]
                o_ref[rows, cols] = wsq[c:] + jnp.dot(qk2.astype(BF16), vn_bd, preferred_element_type=F32)
            st["vn"] = vnbs

        def state_update():
            for p, vnb in enumerate(st["vn"]):
                kd2 = kd_ref[rows, pair_cols(p)]
                for k in range(2):
                    h = 2 * p + k
                    hc = slice(k * LANES, (k + 1) * LANES)
                    el = exp_last[i * c:i * c + 1, h:h + 1]
                    s_ref[i, h] = s_ref[i, h] * el + _dot_tn(kd2[:, hc], vnb[:, hc])

        doublings = int(math.log2(c)) - 2
        return ([functools.partial(conv_stage, i), build, square] + [double] * doublings
                + [last_factor, solve, state_products, outputs, state_update])

    stage_lists = [seq_stages(i) for i in range(GDN_SEQS)]
    n_stages = len(stage_lists[0])
    for t in range(n_stages + GDN_SKEW * (GDN_SEQS - 1)):
        for i in range(GDN_SEQS):
            stage = t - GDN_SKEW * i
            if 0 <= stage < n_stages:
                stage_lists[i][stage]()

    for h in range(GDN_HEADS):
        cols = slice(h * LANES, (h + 1) * LANES)
        o = o_ref[:, cols]
        o = o * lax.rsqrt(jnp.mean(o * o, axis=-1, keepdims=True) + RMS_EPS) * ghn_ref[...]
        o_ref[:, cols] = o * _silu(z_ref[:, :, cols].reshape(tl, LANES))
    p_a = jnp.dot(o_ref[...].astype(BF16), wproj_ref[...], preferred_element_type=F32)
    h_a = _sigmoid(gate_ref[...].reshape(tl, D_MODEL)) * p_a
    ha_ref[...] = h_a.reshape(GDN_SEQS, c, D_MODEL).astype(ha_ref.dtype)

    @pl.when(step == pl.num_programs(1) - 1)
    def _():
        sfin_ref[...] = s_ref[...]


def _gdn_prompt(proj, ab, conv_wt, alog_pad, dtb_pad, ghn, w_proj_a, batch, seq):
    c = GDN_CHUNK
    ns = GDN_SEQS
    tl = ns * c
    proj3 = proj.reshape(batch, seq, MAIN_WIDTH)
    ab3 = ab.reshape(batch, seq, LANES)
    h_a, s_fin = pl.pallas_call(
        _gdn_prompt_kernel,
        grid=(batch // ns, seq // c),
        in_specs=[
            pl.BlockSpec((ns, c, CONV_DIM), lambda b, l: (b, l, COL_QKV_A // CONV_DIM)),
            pl.BlockSpec((ns, c, GDN_QK), lambda b, l: (b, l, COL_Z_A // GDN_QK)),
            pl.BlockSpec((ns, c, LANES), lambda b, l: (b, l, 0)),
            pl.BlockSpec((ns, c, D_MODEL), lambda b, l: (b, l, COL_GATE_A // D_MODEL)),
            pl.BlockSpec((CONV_WIDTH, CONV_DIM), lambda b, l: (0, 0)),
            pl.BlockSpec((1, LANES), lambda b, l: (0, 0)),
            pl.BlockSpec((1, LANES), lambda b, l: (0, 0)),
            pl.BlockSpec((1, LANES), lambda b, l: (0, 0)),
            pl.BlockSpec((GDN_QK, D_MODEL), lambda b, l: (0, 0)),
        ],
        out_specs=[
            pl.BlockSpec((ns, c, D_MODEL), lambda b, l: (b, l, 0)),
            pl.BlockSpec((ns, GDN_HEADS, HEAD_DIM, HEAD_DIM), lambda b, l: (b, 0, 0, 0)),
        ],
        out_shape=[
            jax.ShapeDtypeStruct((batch, seq, D_MODEL), BF16),
            jax.ShapeDtypeStruct((batch, GDN_HEADS, HEAD_DIM, HEAD_DIM), F32),
        ],
        scratch_shapes=[
            pltpu.VMEM((ns, CONV_DIM // LANES, c + 8, LANES), F32),
            pltpu.VMEM((CONV_DIM // LANES, tl, LANES), F32),
            pltpu.VMEM((tl, GDN_QK), F32),
            pltpu.VMEM((ns, GDN_HEADS, HEAD_DIM, HEAD_DIM), F32),
            pltpu.VMEM((tl, GDN_QK), F32),
            pltpu.VMEM((tl, GDN_QK), F32),
            pltpu.VMEM((tl, GDN_QK), F32),
            pltpu.VMEM((tl, GDN_QK), F32),
            pltpu.VMEM((tl, GDN_HEADS * GDN_CHUNK), F32),
        ],
        compiler_params=pltpu.CompilerParams(
            dimension_semantics=("parallel", "arbitrary"), vmem_limit_bytes=VMEM_LIMIT),
        name="gdn_prompt",
    )(proj3, proj3, ab3, proj3, conv_wt, alog_pad, dtb_pad, ghn, w_proj_a)
    return h_a.reshape(batch * seq, D_MODEL), s_fin


def _gdn_sample_kernel(sc_q_ref, sc_k_ref, sc_v_ref, nq_ref, nk_ref, nv_ref, cwq_ref, cwk_ref, cwv_ref,
                       ab_ref, alog_ref, dtb_ref, s_ref, o_ref, snew_ref):
    h = pl.program_id(0)
    nb = nq_ref.shape[0]

    def conv_silu(sc_ref, new_ref, cw_ref):
        acc = new_ref[...] * cw_ref[3:4, :]
        for w in range(CONV_WIDTH - 1):
            acc = acc + sc_ref[w] * cw_ref[w:w + 1, :]
        return _silu(acc)

    q = conv_silu(sc_q_ref, nq_ref, cwq_ref)
    k = conv_silu(sc_k_ref, nk_ref, cwk_ref)
    v = conv_silu(sc_v_ref, nv_ref, cwv_ref)
    q = q * lax.rsqrt(jnp.sum(q * q, axis=-1, keepdims=True) + RMS_EPS) * (HEAD_DIM ** -0.5)
    k = k * lax.rsqrt(jnp.sum(k * k, axis=-1, keepdims=True) + RMS_EPS)

    ab = ab_ref[...]
    lane = lax.broadcasted_iota(jnp.int32, ab.shape, 1)
    g_full = -jnp.exp(alog_ref[...]) * _softplus(ab + dtb_ref[...])
    g = jnp.sum(jnp.where(lane == h, g_full, 0.0), axis=-1, keepdims=True)
    beta = jnp.sum(jnp.where(lane == h + GDN_HEADS, _sigmoid(ab), 0.0), axis=-1, keepdims=True)
    eg = jnp.exp(g)

    eye = (lax.broadcasted_iota(jnp.int32, (HEAD_DIM, HEAD_DIM), 0)
           == lax.broadcasted_iota(jnp.int32, (HEAD_DIM, HEAD_DIM), 1))

    def as_column(row):
        return jnp.sum(jnp.where(eye, row, 0.0), axis=1, keepdims=True)

    for b in range(nb):
        s = s_ref[b]
        k_col = as_column(k[b:b + 1, :])
        q_col = as_column(q[b:b + 1, :])
        e = eg[b:b + 1, :]
        ks = jnp.sum(k_col * s, axis=0, keepdims=True)
        v_new = beta[b:b + 1, :] * (v[b:b + 1, :] - e * ks)
        s_new = s * e + k_col * v_new
        snew_ref[b] = s_new
        o_ref[b:b + 1, :] = jnp.sum(q_col * s_new, axis=0, keepdims=True)


def _gdn_sample(state_conv_t, proj, ab, conv_wt, alog_pad, dtb_pad, state, nb):
    batch = proj.shape[0]
    hq = GDN_QK // LANES
    return pl.pallas_call(
        _gdn_sample_kernel,
        grid=(GDN_HEADS, batch // nb),
        in_specs=[
            pl.BlockSpec((CONV_WIDTH - 1, nb, LANES), lambda h, i: (0, i, h)),
            pl.BlockSpec((CONV_WIDTH - 1, nb, LANES), lambda h, i: (0, i, hq + h)),
            pl.BlockSpec((CONV_WIDTH - 1, nb, LANES), lambda h, i: (0, i, 2 * hq + h)),
            pl.BlockSpec((nb, LANES), lambda h, i: (i, h)),
            pl.BlockSpec((nb, LANES), lambda h, i: (i, hq + h)),
            pl.BlockSpec((nb, LANES), lambda h, i: (i, 2 * hq + h)),
            pl.BlockSpec((CONV_WIDTH, LANES), lambda h, i: (0, h)),
            pl.BlockSpec((CONV_WIDTH, LANES), lambda h, i: (0, hq + h)),
            pl.BlockSpec((CONV_WIDTH, LANES), lambda h, i: (0, 2 * hq + h)),
            pl.BlockSpec((nb, LANES), lambda h, i: (i, 0)),
            pl.BlockSpec((1, LANES), lambda h, i: (0, 0)),
            pl.BlockSpec((1, LANES), lambda h, i: (0, 0)),
            pl.BlockSpec((nb, None, HEAD_DIM, HEAD_DIM), lambda h, i: (i, h, 0, 0)),
        ],
        out_specs=[
            pl.BlockSpec((nb, LANES), lambda h, i: (i, h)),
            pl.BlockSpec((nb, None, HEAD_DIM, HEAD_DIM), lambda h, i: (i, h, 0, 0)),
        ],
        out_shape=[
            jax.ShapeDtypeStruct((batch, GDN_QK), F32),
            jax.ShapeDtypeStruct(state.shape, F32),
        ],
        compiler_params=pltpu.CompilerParams(
            dimension_semantics=("parallel", "parallel"), vmem_limit_bytes=VMEM_LIMIT),
        name="gdn_sample",
    )(state_conv_t, state_conv_t, state_conv_t, proj, proj, proj, conv_wt, conv_wt, conv_wt,
      ab, alog_pad, dtb_pad, state)


def _gdn_sample_out_kernel(o_ref, z_ref, gate_ref, ghn_ref, wproj_ref, ha_ref, y_ref):
    for h in range(GDN_HEADS):
        cols = slice(h * LANES, (h + 1) * LANES)
        o = o_ref[:, cols]
        o = o * lax.rsqrt(jnp.mean(o * o, axis=-1, keepdims=True) + RMS_EPS) * ghn_ref[...]
        y_ref[:, cols] = o * _silu(z_ref[:, cols])
    p_a = jnp.dot(y_ref[...].astype(BF16), wproj_ref[...], preferred_element_type=F32)
    ha_ref[...] = (_sigmoid(gate_ref[...]) * p_a).astype(ha_ref.dtype)


def _gdn_sample_out(o, proj, ghn, w_proj_a):
    batch = o.shape[0]
    return pl.pallas_call(
        _gdn_sample_out_kernel,
        grid=(1,),
        in_specs=[
            pl.BlockSpec((batch, GDN_QK), lambda i: (0, 0)),
            pl.BlockSpec((batch, GDN_QK), lambda i: (0, COL_Z_A // GDN_QK)),
            pl.BlockSpec((batch, D_MODEL), lambda i: (0, COL_GATE_A // D_MODEL)),
            pl.BlockSpec((1, LANES), lambda i: (0, 0)),
            pl.BlockSpec((GDN_QK, D_MODEL), lambda i: (0, 0)),
        ],
        out_specs=pl.BlockSpec((batch, D_MODEL), lambda i: (0, 0)),
        out_shape=jax.ShapeDtypeStruct((batch, D_MODEL), BF16),
        scratch_shapes=[pltpu.VMEM((batch, GDN_QK), F32)],
        compiler_params=pltpu.CompilerParams(vmem_limit_bytes=VMEM_LIMIT),
        name="gdn_sample_out",
    )(o, proj, proj, ghn, w_proj_a)


def _attn_prompt_kernel(*refs):
    ins = refs[:5 * N_GROUPS]
    bias_ref = refs[5 * N_GROUPS]
    o_ref = refs[5 * N_GROUPS + 1]
    acc_ref, m_ref, l_ref, bias_start_ref, split_ref = refs[5 * N_GROUPS + 2:]
    tq = ATTN_TILE
    first_tile = pl.program_id(1) == 0
    scale = (HEAD_DIM ** -0.5) * LOG2_E
    lane2 = lax.broadcasted_iota(jnp.int32, (1, 2 * WIN_BLK), 1)
    start_mask = jnp.where(first_tile & (lane2 < WIN_BLK), NEG, 0.0).astype(F32)
    for gi in range(N_GROUPS):
        bias_start_ref[gi] = bias_ref[gi] + start_mask

    def strided(start, n, stride):
        return pl.ds(start, n, stride=stride) if stride > 1 else pl.ds(start, n)

    for gi, (_, dil) in enumerate(DIL_GROUPS):
        srcs = ins[5 * gi:5 * gi + 5]
        n_blocks = tq // (dil * WIN_BLK)
        inner = min(dil, MAX_ROW_STRIDE)
        outer = dil // inner
        if outer > 1:
            for a, src in enumerate(srcs):
                n = src.shape[0]
                for p in range(outer):
                    split_ref[a, p * (n // outer):(p + 1) * (n // outer), :] = src[strided(p, n // outer, outer), :]
            srcs = [split_ref.at[a] for a in range(5)]
        q_ref, kp_ref, kc_ref, vp_ref, vc_ref = srcs

        def rows_of(ref_rows, r, first, count, inner=inner, outer=outer):
            if outer == 1:
                return strided(r + inner * first, count, inner)
            part = r % outer
            return strided(part * (ref_rows // outer) + r // outer + inner * first, count, inner)

        def run_units(units, q_ref=q_ref, kp_ref=kp_ref, kc_ref=kc_ref, vp_ref=vp_ref, vc_ref=vc_ref,
                      gi=gi, rows_of=rows_of):
            span = kp_ref.shape[0]
            qs, kks, vvs = [], [], []
            for r, j in units:
                qs.append((q_ref[rows_of(tq, r, WIN_BLK * j, WIN_BLK), :] * scale).astype(BF16))
                if j == 0:
                    prev = rows_of(span, r, 0, WIN_BLK)
                    cur = rows_of(tq, r, 0, WIN_BLK)
                    kks.append(jnp.concatenate([kp_ref[prev, :], kc_ref[cur, :]], axis=0).astype(BF16))
                    vvs.append(jnp.concatenate([vp_ref[prev, :], vc_ref[cur, :]], axis=0).astype(BF16))
                else:
                    rows = rows_of(tq, r, WIN_BLK * (j - 1), 2 * WIN_BLK)
                    kks.append(kc_ref[rows, :].astype(BF16))
                    vvs.append(vc_ref[rows, :].astype(BF16))
            ss = [lax.dot_general(q, kk, (((1,), (1,)), ((), ())), preferred_element_type=F32)
                  for q, kk in zip(qs, kks)]
            ps, ms, ls = [], [], []
            for (r, j), s in zip(units, ss):
                s = s + (bias_start_ref[gi] if j == 0 else bias_ref[gi])
                m = jnp.max(s, axis=-1, keepdims=True)
                p = jnp.exp2(s - m)
                ms.append(m)
                ls.append(jnp.sum(p, axis=-1, keepdims=True))
                ps.append(p.astype(BF16))
            accs = [jnp.dot(p, vv, preferred_element_type=F32) for p, vv in zip(ps, vvs)]
            for (r, j), acc, m, l in zip(units, accs, ms, ls):
                q_rows = rows_of(tq, r, WIN_BLK * j, WIN_BLK)
                acc_ref[gi, q_rows, :] = acc
                m_ref[gi, q_rows, :] = jnp.broadcast_to(m, (WIN_BLK, LANES))
                l_ref[gi, q_rows, :] = jnp.broadcast_to(l, (WIN_BLK, LANES))

        if dil == 1:
            for j0 in range(0, n_blocks, ATTN_UNITS):
                run_units([(0, j) for j in range(j0, min(j0 + ATTN_UNITS, n_blocks))])
        else:
            jb = min(n_blocks, ATTN_UNITS)
            rb = max(ATTN_UNITS // n_blocks, 1)
            for j0 in range(0, n_blocks, jb):
                def body(it, carry, j0=j0, jb=jb, rb=rb, run_units=run_units):
                    run_units([(it * rb + k, j) for k in range(rb) for j in range(j0, j0 + jb)])
                    return carry
                lax.fori_loop(0, dil // rb, body, 0)

    outers = [dil // min(dil, MAX_ROW_STRIDE) for _, dil in DIL_GROUPS]
    parts = max(outers)
    cnt = 2 * WIN_BLK
    for p in range(parts):
        for i0 in range(0, tq // parts, cnt):
            def rows_in(gi):
                if outers[gi] == parts:
                    return pl.ds(p * (tq // parts) + i0, cnt)
                assert outers[gi] == 1
                return strided(p + parts * i0, cnt, parts)
            m_all = jnp.maximum(jnp.maximum(m_ref[0, rows_in(0), :], m_ref[1, rows_in(1), :]),
                                m_ref[2, rows_in(2), :])
            num = jnp.zeros((cnt, LANES), F32)
            den = jnp.zeros((cnt, LANES), F32)
            for gi in range(N_GROUPS):
                wgt = jnp.exp2(m_ref[gi, rows_in(gi), :] - m_all)
                num = num + wgt * acc_ref[gi, rows_in(gi), :]
                den = den + wgt * l_ref[gi, rows_in(gi), :]
            o_ref[strided(p + parts * i0, cnt, parts), :] = num / den


def _attn_prompt(proj, bias_tiles, batch, seq):
    tq = ATTN_TILE
    nt = seq // tq
    in_specs = []
    args = []
    for gi, (_, dil) in enumerate(DIL_GROUPS):
        span = dil * WIN_BLK
        cq = (COL_QKV_B + gi * 3 * DIL_W) // LANES
        ck = cq + DIL_HEADS
        cv = ck + DIL_HEADS

        def cur(col):
            return pl.BlockSpec((tq, LANES), lambda b, i, h, col=col: (b * nt + i, col + h))

        def prev(col, span=span):
            per = tq // span
            return pl.BlockSpec(
                (span, LANES),
                lambda b, i, h, col=col, per=per: (jnp.maximum((b * nt + i) * per - 1, 0), col + h))

        in_specs += [cur(cq), prev(ck), cur(ck), prev(cv), cur(cv)]
        args += [proj] * 5
    in_specs.append(pl.BlockSpec((N_GROUPS, None, WIN_BLK, 2 * WIN_BLK), lambda b, i, h: (0, h, 0, 0)))
    args.append(bias_tiles)
    return pl.pallas_call(
        _attn_prompt_kernel,
        grid=(batch, nt, DIL_HEADS),
        in_specs=in_specs,
        out_specs=pl.BlockSpec((tq, LANES), lambda b, i, h: (b * nt + i, h)),
        out_shape=jax.ShapeDtypeStruct((batch * seq, DIL_W), F32),
        scratch_shapes=[pltpu.VMEM((N_GROUPS, tq, LANES), F32)] * 3
        + [pltpu.VMEM((N_GROUPS, WIN_BLK, 2 * WIN_BLK), F32), pltpu.VMEM((5, tq, LANES), F32)],
        compiler_params=pltpu.CompilerParams(
            dimension_semantics=("parallel", "parallel", "parallel"), vmem_limit_bytes=VMEM_LIMIT),
        name="attn_prompt",
    )(*args)


def _attn_sample_kernel(*refs):
    q_refs = refs[0:N_GROUPS]
    kn_refs = refs[N_GROUPS:2 * N_GROUPS]
    vn_refs = refs[2 * N_GROUPS:3 * N_GROUPS]
    kc_refs = refs[3 * N_GROUPS:4 * N_GROUPS]
    vc_refs = refs[4 * N_GROUPS:5 * N_GROUPS]
    bias0_ref, bias_ref, o_ref = refs[5 * N_GROUPS:5 * N_GROUPS + 3]
    m_ref, l_ref, acc_ref = refs[5 * N_GROUPS + 3:]
    t = pl.program_id(0)
    scale = HEAD_DIM ** -0.5

    @pl.when(t == 0)
    def _():
        for gi in range(N_GROUPS):
            s = jnp.sum(q_refs[gi][...] * kn_refs[gi][...], axis=-1, keepdims=True) * scale + bias0_ref[gi]
            m_ref[gi] = s
            l_ref[gi] = jnp.ones_like(s)
            acc_ref[gi] = vn_refs[gi][...]

    for gi in range(N_GROUPS):
        s = jnp.sum(q_refs[gi][...] * kc_refs[gi][...], axis=-1, keepdims=True) * scale + bias_ref[gi]
        m_old = m_ref[gi]
        m_new = jnp.maximum(m_old, s)
        alpha = jnp.exp(m_old - m_new)
        p = jnp.exp(s - m_new)
        l_ref[gi] = alpha * l_ref[gi] + p
        acc_ref[gi] = alpha * acc_ref[gi] + p * vc_refs[gi][...]
        m_ref[gi] = m_new

    @pl.when(t == pl.num_programs(0) - 1)
    def _():
        m_all = jnp.maximum(jnp.maximum(m_ref[0], m_ref[1]), m_ref[2])
        num = jnp.zeros(acc_ref.shape[1:], F32)
        den = jnp.zeros(m_ref.shape[1:], F32)
        for gi in range(N_GROUPS):
            wgt = jnp.exp(m_ref[gi] - m_all)
            num = num + wgt * acc_ref[gi]
            den = den + wgt * l_ref[gi]
        o_ref[...] = num / den


def _attn_sample(q_g, kn_g, vn_g, k_caches, v_caches, bias0, bias_steps):
    batch = q_g[0].shape[0]
    small = pl.BlockSpec((batch, DIL_HEADS, HEAD_DIM), lambda t: (0, 0, 0))
    in_specs = [small] * (3 * N_GROUPS)
    for _ in range(2):
        for _, dil in DIL_GROUPS:
            in_specs.append(pl.BlockSpec((None, batch, None, DIL_HEADS, HEAD_DIM),
                                         lambda t, dil=dil: (0, 0, t * dil, 0, 0)))
    in_specs.append(pl.BlockSpec((N_GROUPS, 1, DIL_HEADS, 1), lambda t: (0, 0, 0, 0)))
    in_specs.append(pl.BlockSpec((None, N_GROUPS, 1, DIL_HEADS, 1), lambda t: (t, 0, 0, 0, 0)))
    return pl.pallas_call(
        _attn_sample_kernel,
        grid=(WIN_BLK,),
        in_specs=in_specs,
        out_specs=pl.BlockSpec((batch, DIL_HEADS, HEAD_DIM), lambda t: (0, 0, 0)),
        out_shape=jax.ShapeDtypeStruct((batch, DIL_HEADS, HEAD_DIM), F32),
        scratch_shapes=[
            pltpu.VMEM((N_GROUPS, batch, DIL_HEADS, 1), F32),
            pltpu.VMEM((N_GROUPS, batch, DIL_HEADS, 1), F32),
            pltpu.VMEM((N_GROUPS, batch, DIL_HEADS, HEAD_DIM), F32),
        ],
        compiler_params=pltpu.CompilerParams(
            dimension_semantics=("arbitrary",), vmem_limit_bytes=VMEM_LIMIT),
        name="attn_sample",
    )(*q_g, *kn_g, *vn_g, *k_caches, *v_caches, bias0, bias_steps)


def _output_kernel(ob_ref, zb_ref, gateb_ref, ha_ref, x_ref, wpb_ref, wout_ref, gpost_ref, y_ref):
    yb = (ob_ref[...] * _silu(zb_ref[...])).astype(BF16)
    p_b = jnp.dot(yb, wpb_ref[...], preferred_element_type=F32)
    hid = ha_ref[...].astype(F32) + _sigmoid(gateb_ref[...]) * p_b
    out = jnp.dot(hid.astype(BF16), wout_ref[...], preferred_element_type=F32)
    ms = jnp.mean(out * out, axis=-1, keepdims=True)
    y_ref[...] = x_ref[...] + out * lax.rsqrt(ms + RMS_EPS) * gpost_ref[...]


def _output(o_b, proj, h_a, x2d, w_proj_b, w_out, g_post, tm):
    m = x2d.shape[0]
    return pl.pallas_call(
        _output_kernel,
        grid=(m // tm,),
        in_specs=[
            pl.BlockSpec((tm, DIL_W), lambda i: (i, 0)),
            pl.BlockSpec((tm, DIL_W), lambda i: (i, COL_Z_B // DIL_W)),
            pl.BlockSpec((tm, D_MODEL), lambda i: (i, COL_GATE_B // D_MODEL)),
            pl.BlockSpec((tm, D_MODEL), lambda i: (i, 0)),
            pl.BlockSpec((tm, D_MODEL), lambda i: (i, 0)),
            pl.BlockSpec((DIL_W, D_MODEL), lambda i: (0, 0)),
            pl.BlockSpec((D_MODEL, D_MODEL), lambda i: (0, 0)),
            pl.BlockSpec((1, D_MODEL), lambda i: (0, 0)),
        ],
        out_specs=pl.BlockSpec((tm, D_MODEL), lambda i: (i, 0)),
        out_shape=jax.ShapeDtypeStruct((m, D_MODEL), F32),
        compiler_params=pltpu.CompilerParams(
            dimension_semantics=("parallel",), vmem_limit_bytes=VMEM_LIMIT),
        name="output",
    )(o_b, proj, proj, h_a, x2d, w_proj_b, w_out, g_post)


def _rel_bucket(dist):
    max_exact = REL_BUCKETS // 2
    d = jnp.maximum(dist, 1).astype(F32)
    large = max_exact + (jnp.log(d / max_exact) / math.log(REL_MAX_DIST / max_exact)
                         * (REL_BUCKETS - max_exact)).astype(jnp.int32)
    large = jnp.minimum(large, REL_BUCKETS - 1)
    return jnp.where(dist < max_exact, dist, large)


def _group_biases(rel_table):
    out = []
    for gi, (win, dil) in enumerate(DIL_GROUPS):
        dist = jnp.arange(win // dil + 1, dtype=jnp.int32) * dil
        b = rel_table[_rel_bucket(dist)][:, gi * DIL_HEADS:(gi + 1) * DIL_HEADS]
        out.append(b.T.astype(F32))
    return out


def _bias_tiles(biases):
    period = 3 * WIN_BLK
    tiles = []
    for b in biases:
        u = jnp.concatenate([b[:, ::-1], jnp.full((DIL_HEADS, period - WIN_BLK - 1), NEG, F32)], axis=1)
        flat = jnp.tile(u, (1, WIN_BLK))[:, :WIN_BLK * (period - 1)]
        tiles.append(flat.reshape(DIL_HEADS, WIN_BLK, period - 1)[:, :, :2 * WIN_BLK])
    return jnp.stack(tiles, axis=0).astype(F32)


def _layer_weights(w_in, conv_w, a_log, dt_bias, g_head_norm, w_proj_a, w_proj_b, w_out):
    w_main = jnp.concatenate([w_in[:, :AB_OFFSET], w_in[:, AB_OFFSET + 2 * GDN_HEADS:]], axis=1).astype(BF16)
    w_ab = jnp.pad(w_in[:, AB_OFFSET:AB_OFFSET + 2 * GDN_HEADS],
                   ((0, 0), (0, LANES - 2 * GDN_HEADS))).astype(BF16)
    pad = (0, LANES - GDN_HEADS)
    return dict(
        w_main=w_main, w_ab=w_ab, conv_wt=conv_w.T.astype(F32),
        alog_pad=jnp.pad(a_log.astype(F32), pad)[None, :],
        dtb_pad=jnp.pad(dt_bias.astype(F32), pad)[None, :],
        ghn=g_head_norm.astype(F32)[None, :],
        w_proj_a=w_proj_a.astype(BF16), w_proj_b=w_proj_b.astype(BF16), w_out=w_out.astype(BF16))


def _split_heads(a2d, col, rows_per_batch, batch, keep):
    a = a2d.reshape(batch, rows_per_batch, -1)[:, rows_per_batch - keep:, col:col + DIL_W]
    return a.reshape(batch, keep, DIL_HEADS, HEAD_DIM)


def kernel(x_prompt, x_sample, state_gdn, state_conv, cache_k_w128, cache_v_w128, cache_k_w512,
           cache_v_w512, cache_k_w2048, cache_v_w2048, rel_table, g_pre, w_in, conv_w, a_log, dt_bias,
           g_head_norm, w_proj_a, w_proj_b, w_out, g_post):
    batch, seq, _ = x_prompt.shape
    dec_batch = x_sample.shape[0]
    k_caches = (cache_k_w128, cache_k_w512, cache_k_w2048)
    v_caches = (cache_v_w128, cache_v_w512, cache_v_w2048)
    biases = _group_biases(rel_table)
    bias_tiles = _bias_tiles(biases) * LOG2_E
    lw = _layer_weights(w_in[0], conv_w[0], a_log[0], dt_bias[0], g_head_norm[0], w_proj_a[0],
                        w_proj_b[0], w_out[0])
    g_pre_l = g_pre[0][None, :]
    g_post_l = g_post[0][None, :]

    xp = x_prompt.reshape(batch * seq, D_MODEL)
    proj, ab = _in_projection(xp, g_pre_l, lw["w_main"], lw["w_ab"], tm=2048, tn=1024)
    h_a, p_gdn = _gdn_prompt(proj, ab, lw["conv_wt"], lw["alog_pad"], lw["dtb_pad"], lw["ghn"],
                             lw["w_proj_a"], batch, seq)
    o_b = _attn_prompt(proj, bias_tiles, batch, seq)
    y_prompt = _output(o_b, proj, h_a, xp, lw["w_proj_b"], lw["w_out"], g_post_l, tm=512)
    y_prompt = y_prompt.reshape(batch, seq, D_MODEL)
    p_conv = proj.reshape(batch, seq, MAIN_WIDTH)[:, seq - (CONV_WIDTH - 1):, :CONV_DIM]
    p_rows = []
    for gi, (win, _) in enumerate(DIL_GROUPS):
        keep = min(win, seq)
        ck = COL_QKV_B + gi * 3 * DIL_W + DIL_W
        p_rows += [_split_heads(proj, ck, seq, batch, keep), _split_heads(proj, ck + DIL_W, seq, batch, keep)]

    xs = x_sample.reshape(dec_batch, D_MODEL)
    proj_s, ab_s = _in_projection(xs, g_pre_l, lw["w_main"], lw["w_ab"], tm=dec_batch, tn=1024)
    sc_t = jnp.transpose(state_conv[0], (1, 0, 2))
    o_s, s_gdn = _gdn_sample(sc_t, proj_s, ab_s, lw["conv_wt"], lw["alog_pad"], lw["dtb_pad"],
                             state_gdn[0], nb=32)
    h_a_s = _gdn_sample_out(o_s, proj_s, lw["ghn"], lw["w_proj_a"])
    q_g, kn_g, vn_g = [], [], []
    for gi in range(N_GROUPS):
        cq = COL_QKV_B + gi * 3 * DIL_W
        q_g.append(proj_s[:, cq:cq + DIL_W].reshape(dec_batch, DIL_HEADS, HEAD_DIM))
        kn_g.append(proj_s[:, cq + DIL_W:cq + 2 * DIL_W].reshape(dec_batch, DIL_HEADS, HEAD_DIM))
        vn_g.append(proj_s[:, cq + 2 * DIL_W:cq + 3 * DIL_W].reshape(dec_batch, DIL_HEADS, HEAD_DIM))
    bias_all = jnp.stack(biases, axis=0)
    bias0 = bias_all[:, None, :, 0:1]
    bias_steps = jnp.transpose(bias_all[:, :, :0:-1], (2, 0, 1))[:, :, None, :, None]
    o_b_s = _attn_sample(q_g, kn_g, vn_g, k_caches, v_caches, bias0, bias_steps)
    y_sample = _output(o_b_s.reshape(dec_batch, DIL_W), proj_s, h_a_s, xs, lw["w_proj_b"], lw["w_out"],
                       g_post_l, tm=dec_batch)
    y_sample = y_sample.reshape(dec_batch, 1, D_MODEL)
    s_conv = jnp.concatenate([state_conv[0][:, 1:, :], proj_s[:, None, :CONV_DIM]], axis=1)
    s_rows = []
    for gi in range(N_GROUPS):
        s_rows += [kn_g[gi][:, None], vn_g[gi][:, None]]

    return (y_prompt, y_sample, p_gdn[None], p_conv[None], *[r[None] for r in p_rows],
            s_gdn[None], s_conv[None], *[r[None] for r in s_rows])
```

```python
import functools
import math

import jax
import jax.numpy as jnp
from jax import lax
from jax.experimental import pallas as pl
from jax.experimental.pallas import tpu as pltpu

F32 = jnp.float32
BF16 = jnp.bfloat16

D_MODEL = 1024
GDN_HEADS = 8
HEAD_DIM = 128
GDN_QK = GDN_HEADS * HEAD_DIM
CONV_DIM = 3 * GDN_QK
CONV_WIDTH = 4
DIL_GROUPS = ((128, 1), (512, 4), (2048, 16))
N_GROUPS = 3
DIL_HEADS = 4
DIL_W = DIL_HEADS * HEAD_DIM
WIN_BLK = 128
REL_BUCKETS = 32
REL_MAX_DIST = 2048
RMS_EPS = 1e-6
NEG = -1e30
LOG2_E = math.log2(math.e)

COL_QKV_A = 0
COL_Z_A = 3072
COL_QKV_B = 4096
COL_Z_B = 8704
COL_GATE_A = 9216
COL_GATE_B = 10240
MAIN_WIDTH = 11264
AB_OFFSET = 4096
LANES = 128

VMEM_LIMIT = 56 * 1024 * 1024

GDN_CHUNK = 64
GDN_SEQS = 4
GDN_SKEW = 0
ATTN_TILE = 2048
ATTN_UNITS = 16
MAX_ROW_STRIDE = 4


def _silu(x):
    half = 0.5 * x
    return half * jnp.tanh(half) + half


def _sigmoid(x):
    return 1.0 / (1.0 + jnp.exp(-x))


def _softplus(x):
    return jnp.maximum(x, 0.0) + jnp.log(1.0 + jnp.exp(-jnp.abs(x)))


def _dot(a, b):
    return jnp.dot(a.astype(BF16), b.astype(BF16), preferred_element_type=F32)


def _dot_nt(a, b):
    return lax.dot_general(a.astype(BF16), b.astype(BF16), (((1,), (1,)), ((), ())),
                           preferred_element_type=F32)


def _dot_tn(a, b):
    return lax.dot_general(a.astype(BF16), b.astype(BF16), (((0,), (0,)), ((), ())),
                           preferred_element_type=F32)


def _dot_f32(a, b):
    return jnp.dot(a, b, preferred_element_type=F32, precision=lax.Precision.HIGHEST)


def _inproj_kernel(x_ref, g_ref, w_ref, wab_ref, o_ref, ab_ref, xn_ref):
    @pl.when(pl.program_id(1) == 0)
    def _():
        x = x_ref[...]
        ms = jnp.mean(x * x, axis=-1, keepdims=True)
        xn = (x * lax.rsqrt(ms + RMS_EPS) * g_ref[...]).astype(BF16)
        xn_ref[...] = xn
        ab_ref[...] = jnp.dot(xn, wab_ref[...], preferred_element_type=F32)

    o_ref[...] = jnp.dot(xn_ref[...], w_ref[...], preferred_element_type=F32)


def _in_projection(x2d, g_pre, w_main, w_ab, tm, tn):
    m = x2d.shape[0]
    return pl.pallas_call(
        _inproj_kernel,
        grid=(m // tm, MAIN_WIDTH // tn),
        in_specs=[
            pl.BlockSpec((tm, D_MODEL), lambda i, j: (i, 0)),
            pl.BlockSpec((1, D_MODEL), lambda i, j: (0, 0)),
            pl.BlockSpec((D_MODEL, tn), lambda i, j: (0, j)),
            pl.BlockSpec((D_MODEL, LANES), lambda i, j: (0, 0)),
        ],
        out_specs=[
            pl.BlockSpec((tm, tn), lambda i, j: (i, j)),
            pl.BlockSpec((tm, LANES), lambda i, j: (i, 0)),
        ],
        out_shape=[
            jax.ShapeDtypeStruct((m, MAIN_WIDTH), F32),
            jax.ShapeDtypeStruct((m, LANES), F32),
        ],
        scratch_shapes=[pltpu.VMEM((tm, D_MODEL), BF16)],
        compiler_params=pltpu.CompilerParams(
            dimension_semantics=("parallel", "arbitrary"), vmem_limit_bytes=VMEM_LIMIT),
        name="in_projection",
    )(x2d, g_pre, w_main, w_ab)


GROUP_HEADS = 4


def _block_diag_rhs(x4, half_masks):
    c = x4.shape[0]
    zeros = jnp.zeros((c, LANES), x4.dtype)
    row_blocks = []
    for h in range(GROUP_HEADS):
        piece = x4[:, (h // 2) * LANES:(h // 2 + 1) * LANES] * half_masks[h % 2]
        row_blocks.append(jnp.concatenate([piece if v == h // 2 else zeros for v in range(2)], axis=1))
    return jnp.concatenate(row_blocks, axis=0)


def _gdn_prompt_kernel(qkv_ref, z_ref, ab_ref, gate_ref, convw_ref, alog_ref, dtb_ref, ghn_ref,
                       wproj_ref, ha_ref, sfin_ref, xpad_ref, qkvn_ref, o_ref, s_ref,
                       u_ref, w_ref, qd_ref, kd_ref, qk_ref):
    c = GDN_CHUNK
    tl = GDN_SEQS * c
    gw = GROUP_HEADS * c
    step = pl.program_id(1)

    @pl.when(step == 0)
    def _():
        xpad_ref[:, :, 0:8, :] = jnp.zeros((GDN_SEQS, CONV_DIM // LANES, 8, LANES), F32)
        s_ref[...] = jnp.zeros_like(s_ref)

    def conv_stage(i):
        for blk in range(CONV_DIM // LANES):
            cols = slice(blk * LANES, (blk + 1) * LANES)
            xpad_ref[i, blk, 8:8 + c, :] = qkv_ref[i, :, cols]
            for phase in range(2):
                acc = None
                for w in range(CONV_WIDTH):
                    tap = xpad_ref[i, blk, pl.ds(5 + phase + w, c // 2, stride=2), :] * convw_ref[w:w + 1, cols]
                    acc = tap if acc is None else acc + tap
                y = _silu(acc)
                if blk < 2 * GDN_HEADS:
                    y = y * lax.rsqrt(jnp.sum(y * y, axis=-1, keepdims=True) + RMS_EPS)
                    if blk < GDN_HEADS:
                        y = y * (HEAD_DIM ** -0.5)
                qkvn_ref[blk, pl.ds(i * c + phase, c // 2, stride=2), :] = y
            xpad_ref[i, blk, 5:8, :] = xpad_ref[i, blk, c + 5:c + 8, :]

    ab = ab_ref[...].reshape(tl, LANES)
    g_full = -jnp.exp(alog_ref[...]) * _softplus(ab + dtb_ref[...])
    beta_full = _sigmoid(ab)
    row = lax.broadcasted_iota(jnp.int32, (tl, tl), 0)
    col = lax.broadcasted_iota(jnp.int32, (tl, tl), 1)
    same_chunk = (row // c) == (col // c)
    tri = jnp.where(same_chunk & (col <= row), 1.0, 0.0).astype(F32)
    g_cum = _dot_f32(tri, g_full)
    g_cum_t = g_cum.T
    last = jnp.where(same_chunk & ((col % c) == c - 1), 1.0, 0.0).astype(F32)
    g_last = _dot_f32(last, g_cum)
    exp_g = jnp.exp(g_cum)
    exp_rest = jnp.exp(g_last - g_cum)
    exp_last = jnp.exp(g_last)

    ci = lax.broadcasted_iota(jnp.int32, (c, gw), 0)
    cj = lax.broadcasted_iota(jnp.int32, (c, gw), 1) % c
    causal4 = cj <= ci
    strict4 = cj < ci
    half_lane = lax.broadcasted_iota(jnp.int32, (c, LANES), 1) // c
    half_masks = [jnp.where(half_lane == k, 1.0, 0.0).astype(BF16) for k in range(2)]

    def lanes_of(col_values, width):
        return jnp.concatenate([jnp.broadcast_to(v, (c, width)) for v in col_values], axis=1)

    n_groups = GDN_HEADS // GROUP_HEADS
    zero_s = jnp.zeros((HEAD_DIM, HEAD_DIM), BF16)
    zero_v = jnp.zeros((c, HEAD_DIM), BF16)

    def group_cols(hg):
        return slice(hg * GROUP_HEADS * LANES, (hg + 1) * GROUP_HEADS * LANES)

    def bd_dot(lhs, x4):
        return jnp.dot(lhs.astype(BF16), _block_diag_rhs(x4.astype(BF16), half_masks),
                       preferred_element_type=F32)

    def seq_stages(i):
        rows = slice(i * c, (i + 1) * c)
        st = {}

        def build():
            a4s = []
            for hg in range(n_groups):
                heads = range(hg * GROUP_HEADS, (hg + 1) * GROUP_HEADS)
                cols = group_cols(hg)

                def head_blocks(first_blk):
                    return jnp.concatenate([qkvn_ref[first_blk + h, rows, :] for h in heads], axis=1)

                q4 = head_blocks(0)
                k4 = head_blocks(GDN_HEADS)
                v4 = head_blocks(2 * GDN_HEADS)
                beta_e = lanes_of([beta_full[rows, GDN_HEADS + h:GDN_HEADS + h + 1] for h in heads], LANES)
                eg_e = lanes_of([exp_g[rows, h:h + 1] for h in heads], LANES)
                er_e = lanes_of([exp_rest[rows, h:h + 1] for h in heads], LANES)
                gi4 = lanes_of([g_cum[rows, h:h + 1] for h in heads], c)
                gj4 = jnp.concatenate([g_cum_t[h:h + 1, rows] for h in heads], axis=1)
                decay4 = jnp.exp(jnp.where(causal4, gi4 - gj4, NEG))
                kb4 = k4 * beta_e
                k4b = k4.astype(BF16)
                k_rhs = jnp.concatenate(
                    [jnp.concatenate([k4b[:, g * LANES:(g + 1) * LANES] if g == h else zero_v
                                      for g in range(GROUP_HEADS)], axis=1)
                     for h in range(GROUP_HEADS)], axis=0)
                kkqk = lax.dot_general(jnp.concatenate([kb4, q4], axis=0).astype(BF16), k_rhs,
                                       (((1,), (1,)), ((), ())), preferred_element_type=F32)
                a4s.append(jnp.where(strict4, kkqk[:c] * decay4, 0.0))
                qk_ref[rows, hg * gw:(hg + 1) * gw] = kkqk[c:] * decay4
                u_ref[rows, cols] = v4 * beta_e
                w_ref[rows, cols] = kb4 * eg_e
                qd_ref[rows, cols] = q4 * eg_e
                kd_ref[rows, cols] = k4 * er_e
            st["a"] = a4s

        def square():
            st["e"] = [-a4 for a4 in st["a"]]
            st["x"] = [bd_dot(a4, a4) for a4 in st["a"]]

        def double():
            boths = [bd_dot(jnp.concatenate([e, x], axis=0), x) for e, x in zip(st["e"], st["x"])]
            st["e"] = [e + x + both[:c] for e, x, both in zip(st["e"], st["x"], boths)]
            st["x"] = [both[c:] for both in boths]

        def last_factor():
            st["e"] = [e + x + bd_dot(e, x) for e, x in zip(st["e"], st["x"])]

        def solve():
            for hg, e4 in enumerate(st["e"]):
                cols = group_cols(hg)
                vb4 = u_ref[rows, cols]
                kbe4 = w_ref[rows, cols]
                us, ws = [], []
                for k in range(GROUP_HEADS):
                    hc = slice(k * LANES, (k + 1) * LANES)
                    rhs = jnp.concatenate([vb4[:, hc], kbe4[:, hc]], axis=1)
                    uw = rhs + _dot(e4[:, k * c:(k + 1) * c], rhs)
                    us.append(uw[:, :LANES])
                    ws.append(uw[:, LANES:])
                u_ref[rows, cols] = jnp.concatenate(us, axis=1)
                w_ref[rows, cols] = jnp.concatenate(ws, axis=1)

        def pair_cols(p):
            return slice(2 * p * LANES, (2 * p + 2) * LANES)

        def state_products():
            wsqs = []
            for p in range(GDN_HEADS // 2):
                cols = pair_cols(p)
                s_bd = jnp.concatenate(
                    [jnp.concatenate([s_ref[i, 2 * p].astype(BF16), zero_s], axis=1),
                     jnp.concatenate([zero_s, s_ref[i, 2 * p + 1].astype(BF16)], axis=1)], axis=0)
                wq = jnp.concatenate([w_ref[rows, cols], qd_ref[rows, cols]], axis=0).astype(BF16)
                wsqs.append(jnp.dot(wq, s_bd, preferred_element_type=F32))
            st["wsq"] = wsqs

        def outputs():
            vnbs = []
            for p, wsq in enumerate(st["wsq"]):
                cols = pair_cols(p)
                vnb = (u_ref[rows, cols] - wsq[:c]).astype(BF16)
                vnbs.append(vnb)
                vn_bd = jnp.concatenate([jnp.concatenate([vnb[:, :LANES], zero_v], axis=1),
                                         jnp.concatenate([zero_v, vnb[:, LANES:]], axis=1)], axis=0)
                qk2 = qk_ref[rows, 2 * p * c:(2 * p + 2) * c]
                o_ref[rows, cols] = wsq[c:] + jnp.dot(qk2.astype(BF16), vn_bd, preferred_element_type=F32)
            st["vn"] = vnbs

        def state_update():
            for p, vnb in enumerate(st["vn"]):
                kd2 = kd_ref[rows, pair_cols(p)]
                for k in range(2):
                    h = 2 * p + k
                    hc = slice(k * LANES, (k + 1) * LANES)
                    el = exp_last[i * c:i * c + 1, h:h + 1]
                    s_ref[i, h] = s_ref[i, h] * el + _dot_tn(kd2[:, hc], vnb[:, hc])

        doublings = int(math.log2(c)) - 2
        return ([functools.partial(conv_stage, i), build, square] + [double] * doublings
                + [last_factor, solve, state_products, outputs, state_update])

    stage_lists = [seq_stages(i) for i in range(GDN_SEQS)]
    n_stages = len(stage_lists[0])
    for t in range(n_stages + GDN_SKEW * (GDN_SEQS - 1)):
        for i in range(GDN_SEQS):
            stage = t - GDN_SKEW * i
            if 0 <= stage < n_stages:
                stage_lists[i][stage]()

    for h in range(GDN_HEADS):
        cols = slice(h * LANES, (h + 1) * LANES)
        o = o_ref[:, cols]
        o = o * lax.rsqrt(jnp.mean(o * o, axis=-1, keepdims=True) + RMS_EPS) * ghn_ref[...]
        o_ref[:, cols] = o * _silu(z_ref[:, :, cols].reshape(tl, LANES))
    p_a = jnp.dot(o_ref[...].astype(BF16), wproj_ref[...], preferred_element_type=F32)
    h_a = _sigmoid(gate_ref[...].reshape(tl, D_MODEL)) * p_a
    ha_ref[...] = h_a.reshape(GDN_SEQS, c, D_MODEL).astype(ha_ref.dtype)

    @pl.when(step == pl.num_programs(1) - 1)
    def _():
        sfin_ref[...] = s_ref[...]


def _gdn_prompt(proj, ab, conv_wt, alog_pad, dtb_pad, ghn, w_proj_a, batch, seq):
    c = GDN_CHUNK
    ns = GDN_SEQS
    tl = ns * c
    proj3 = proj.reshape(batch, seq, MAIN_WIDTH)
    ab3 = ab.reshape(batch, seq, LANES)
    h_a, s_fin = pl.pallas_call(
        _gdn_prompt_kernel,
        grid=(batch // ns, seq // c),
        in_specs=[
            pl.BlockSpec((ns, c, CONV_DIM), lambda b, l: (b, l, COL_QKV_A // CONV_DIM)),
            pl.BlockSpec((ns, c, GDN_QK), lambda b, l: (b, l, COL_Z_A // GDN_QK)),
            pl.BlockSpec((ns, c, LANES), lambda b, l: (b, l, 0)),
            pl.BlockSpec((ns, c, D_MODEL), lambda b, l: (b, l, COL_GATE_A // D_MODEL)),
            pl.BlockSpec((CONV_WIDTH, CONV_DIM), lambda b, l: (0, 0)),
            pl.BlockSpec((1, LANES), lambda b, l: (0, 0)),
            pl.BlockSpec((1, LANES), lambda b, l: (0, 0)),
            pl.BlockSpec((1, LANES), lambda b, l: (0, 0)),
            pl.BlockSpec((GDN_QK, D_MODEL), lambda b, l: (0, 0)),
        ],
        out_specs=[
            pl.BlockSpec((ns, c, D_MODEL), lambda b, l: (b, l, 0)),
            pl.BlockSpec((ns, GDN_HEADS, HEAD_DIM, HEAD_DIM), lambda b, l: (b, 0, 0, 0)),
        ],
        out_shape=[
            jax.ShapeDtypeStruct((batch, seq, D_MODEL), BF16),
            jax.ShapeDtypeStruct((batch, GDN_HEADS, HEAD_DIM, HEAD_DIM), F32),
        ],
        scratch_shapes=[
            pltpu.VMEM((ns, CONV_DIM // LANES, c + 8, LANES), F32),
            pltpu.VMEM((CONV_DIM // LANES, tl, LANES), F32),
            pltpu.VMEM((tl, GDN_QK), F32),
            pltpu.VMEM((ns, GDN_HEADS, HEAD_DIM, HEAD_DIM), F32),
            pltpu.VMEM((tl, GDN_QK), F32),
            pltpu.VMEM((tl, GDN_QK), F32),
            pltpu.VMEM((tl, GDN_QK), F32),
            pltpu.VMEM((tl, GDN_QK), F32),
            pltpu.VMEM((tl, GDN_HEADS * GDN_CHUNK), F32),
        ],
        compiler_params=pltpu.CompilerParams(
            dimension_semantics=("parallel", "arbitrary"), vmem_limit_bytes=VMEM_LIMIT),
        name="gdn_prompt",
    )(proj3, proj3, ab3, proj3, conv_wt, alog_pad, dtb_pad, ghn, w_proj_a)
    return h_a.reshape(batch * seq, D_MODEL), s_fin


def _gdn_sample_kernel(sc_q_ref, sc_k_ref, sc_v_ref, nq_ref, nk_ref, nv_ref, cwq_ref, cwk_ref, cwv_ref,
                       ab_ref, alog_ref, dtb_ref, s_ref, o_ref, snew_ref):
    h = pl.program_id(0)
    nb = nq_ref.shape[0]

    def conv_silu(sc_ref, new_ref, cw_ref):
        acc = new_ref[...] * cw_ref[3:4, :]
        for w in range(CONV_WIDTH - 1):
            acc = acc + sc_ref[w] * cw_ref[w:w + 1, :]
        return _silu(acc)

    q = conv_silu(sc_q_ref, nq_ref, cwq_ref)
    k = conv_silu(sc_k_ref, nk_ref, cwk_ref)
    v = conv_silu(sc_v_ref, nv_ref, cwv_ref)
    q = q * lax.rsqrt(jnp.sum(q * q, axis=-1, keepdims=True) + RMS_EPS) * (HEAD_DIM ** -0.5)
    k = k * lax.rsqrt(jnp.sum(k * k, axis=-1, keepdims=True) + RMS_EPS)

    ab = ab_ref[...]
    lane = lax.broadcasted_iota(jnp.int32, ab.shape, 1)
    g_full = -jnp.exp(alog_ref[...]) * _softplus(ab + dtb_ref[...])
    g = jnp.sum(jnp.where(lane == h, g_full, 0.0), axis=-1, keepdims=True)
    beta = jnp.sum(jnp.where(lane == h + GDN_HEADS, _sigmoid(ab), 0.0), axis=-1, keepdims=True)
    eg = jnp.exp(g)

    eye = (lax.broadcasted_iota(jnp.int32, (HEAD_DIM, HEAD_DIM), 0)
           == lax.broadcasted_iota(jnp.int32, (HEAD_DIM, HEAD_DIM), 1))

    def as_column(row):
        return jnp.sum(jnp.where(eye, row, 0.0), axis=1, keepdims=True)

    for b in range(nb):
        s = s_ref[b]
        k_col = as_column(k[b:b + 1, :])
        q_col = as_column(q[b:b + 1, :])
        e = eg[b:b + 1, :]
        ks = jnp.sum(k_col * s, axis=0, keepdims=True)
        v_new = beta[b:b + 1, :] * (v[b:b + 1, :] - e * ks)
        s_new = s * e + k_col * v_new
        snew_ref[b] = s_new
        o_ref[b:b + 1, :] = jnp.sum(q_col * s_new, axis=0, keepdims=True)


def _gdn_sample(state_conv_t, proj, ab, conv_wt, alog_pad, dtb_pad, state, nb):
    batch = proj.shape[0]
    hq = GDN_QK // LANES
    return pl.pallas_call(
        _gdn_sample_kernel,
        grid=(GDN_HEADS, batch // nb),
        in_specs=[
            pl.BlockSpec((CONV_WIDTH - 1, nb, LANES), lambda h, i: (0, i, h)),
            pl.BlockSpec((CONV_WIDTH - 1, nb, LANES), lambda h, i: (0, i, hq + h)),
            pl.BlockSpec((CONV_WIDTH - 1, nb, LANES), lambda h, i: (0, i, 2 * hq + h)),
            pl.BlockSpec((nb, LANES), lambda h, i: (i, h)),
            pl.BlockSpec((nb, LANES), lambda h, i: (i, hq + h)),
            pl.BlockSpec((nb, LANES), lambda h, i: (i, 2 * hq + h)),
            pl.BlockSpec((CONV_WIDTH, LANES), lambda h, i: (0, h)),
            pl.BlockSpec((CONV_WIDTH, LANES), lambda h, i: (0, hq + h)),
            pl.BlockSpec((CONV_WIDTH, LANES), lambda h, i: (0, 2 * hq + h)),
            pl.BlockSpec((nb, LANES), lambda h, i: (i, 0)),
            pl.BlockSpec((1, LANES), lambda h, i: (0, 0)),
            pl.BlockSpec((1, LANES), lambda h, i: (0, 0)),
            pl.BlockSpec((nb, None, HEAD_DIM, HEAD_DIM), lambda h, i: (i, h, 0, 0)),
        ],
        out_specs=[
            pl.BlockSpec((nb, LANES), lambda h, i: (i, h)),
            pl.BlockSpec((nb, None, HEAD_DIM, HEAD_DIM), lambda h, i: (i, h, 0, 0)),
        ],
        out_shape=[
            jax.ShapeDtypeStruct((batch, GDN_QK), F32),
            jax.ShapeDtypeStruct(state.shape, F32),
        ],
        compiler_params=pltpu.CompilerParams(
            dimension_semantics=("parallel", "parallel"), vmem_limit_bytes=VMEM_LIMIT),
        name="gdn_sample",
    )(state_conv_t, state_conv_t, state_conv_t, proj, proj, proj, conv_wt, conv_wt, conv_wt,
      ab, alog_pad, dtb_pad, state)


def _gdn_sample_out_kernel(o_ref, z_ref, gate_ref, ghn_ref, wproj_ref, ha_ref, y_ref):
    for h in range(GDN_HEADS):
        cols = slice(h * LANES, (h + 1) * LANES)
        o = o_ref[:, cols]
        o = o * lax.rsqrt(jnp.mean(o * o, axis=-1, keepdims=True) + RMS_EPS) * ghn_ref[...]
        y_ref[:, cols] = o * _silu(z_ref[:, cols])
    p_a = jnp.dot(y_ref[...].astype(BF16), wproj_ref[...], preferred_element_type=F32)
    ha_ref[...] = (_sigmoid(gate_ref[...]) * p_a).astype(ha_ref.dtype)


def _gdn_sample_out(o, proj, ghn, w_proj_a):
    batch = o.shape[0]
    return pl.pallas_call(
        _gdn_sample_out_kernel,
        grid=(1,),
        in_specs=[
            pl.BlockSpec((batch, GDN_QK), lambda i: (0, 0)),
            pl.BlockSpec((batch, GDN_QK), lambda i: (0, COL_Z_A // GDN_QK)),
            pl.BlockSpec((batch, D_MODEL), lambda i: (0, COL_GATE_A // D_MODEL)),
            pl.BlockSpec((1, LANES), lambda i: (0, 0)),
            pl.BlockSpec((GDN_QK, D_MODEL), lambda i: (0, 0)),
        ],
        out_specs=pl.BlockSpec((batch, D_MODEL), lambda i: (0, 0)),
        out_shape=jax.ShapeDtypeStruct((batch, D_MODEL), BF16),
        scratch_shapes=[pltpu.VMEM((batch, GDN_QK), F32)],
        compiler_params=pltpu.CompilerParams(vmem_limit_bytes=VMEM_LIMIT),
        name="gdn_sample_out",
    )(o, proj, proj, ghn, w_proj_a)


def _attn_prompt_kernel(*refs):
    ins = refs[:5 * N_GROUPS]
    bias_ref = refs[5 * N_GROUPS]
    o_ref = refs[5 * N_GROUPS + 1]
    acc_ref, m_ref, l_ref, bias_start_ref, split_ref = refs[5 * N_GROUPS + 2:]
    tq = ATTN_TILE
    first_tile = pl.program_id(1) == 0
    scale = (HEAD_DIM ** -0.5) * LOG2_E
    lane2 = lax.broadcasted_iota(jnp.int32, (1, 2 * WIN_BLK), 1)
    start_mask = jnp.where(first_tile & (lane2 < WIN_BLK), NEG, 0.0).astype(F32)
    for gi in range(N_GROUPS):
        bias_start_ref[gi] = bias_ref[gi] + start_mask

    def strided(start, n, stride):
        return pl.ds(start, n, stride=stride) if stride > 1 else pl.ds(start, n)

    for gi, (_, dil) in enumerate(DIL_GROUPS):
        srcs = ins[5 * gi:5 * gi + 5]
        n_blocks = tq // (dil * WIN_BLK)
        inner = min(dil, MAX_ROW_STRIDE)
        outer = dil // inner
        if outer > 1:
            for a, src in enumerate(srcs):
                n = src.shape[0]
                for p in range(outer):
                    split_ref[a, p * (n // outer):(p + 1) * (n // outer), :] = src[strided(p, n // outer, outer), :]
            srcs = [split_ref.at[a] for a in range(5)]
        q_ref, kp_ref, kc_ref, vp_ref, vc_ref = srcs

        def rows_of(ref_rows, r, first, count, inner=inner, outer=outer):
            if outer == 1:
                return strided(r + inner * first, count, inner)
            part = r % outer
            return strided(part * (ref_rows // outer) + r // outer + inner * first, count, inner)

        def run_units(units, q_ref=q_ref, kp_ref=kp_ref, kc_ref=kc_ref, vp_ref=vp_ref, vc_ref=vc_ref,
                      gi=gi, rows_of=rows_of):
            span = kp_ref.shape[0]
            n = len(units)
            st = [{} for _ in units]

            def logits(u):
                r, j = units[u]
                q = (q_ref[rows_of(tq, r, WIN_BLK * j, WIN_BLK), :] * scale).astype(BF16)
                if j == 0:
                    prev = rows_of(span, r, 0, WIN_BLK)
                    cur = rows_of(tq, r, 0, WIN_BLK)
                    kk = jnp.concatenate([kp_ref[prev, :], kc_ref[cur, :]], axis=0).astype(BF16)
                    st[u]["v"] = (vp_ref, prev, vc_ref, cur)
                else:
                    rows = rows_of(tq, r, WIN_BLK * (j - 1), 2 * WIN_BLK)
                    kk = kc_ref[rows, :].astype(BF16)
                    st[u]["v"] = (vc_ref, rows)
                st[u]["s"] = lax.dot_general(q, kk, (((1,), (1,)), ((), ())), preferred_element_type=F32)

            def softmax(u):
                r, j = units[u]
                s = st[u].pop("s") + (bias_start_ref[gi] if j == 0 else bias_ref[gi])
                m = jnp.max(s, axis=-1, keepdims=True)
                p = jnp.exp2(s - m)
                st[u]["m"] = m
                st[u]["l"] = jnp.sum(p, axis=-1, keepdims=True)
                st[u]["p"] = p.astype(BF16)

            def values(u):
                r, j = units[u]
                v = st[u].pop("v")
                if len(v) == 4:
                    vv = jnp.concatenate([v[0][v[1], :], v[2][v[3], :]], axis=0).astype(BF16)
                else:
                    vv = v[0][v[1], :].astype(BF16)
                acc = jnp.dot(st[u].pop("p"), vv, preferred_element_type=F32)
                q_rows = rows_of(tq, r, WIN_BLK * j, WIN_BLK)
                acc_ref[gi, q_rows, :] = acc
                m_ref[gi, q_rows, :] = jnp.broadcast_to(st[u].pop("m"), (WIN_BLK, LANES))
                l_ref[gi, q_rows, :] = jnp.broadcast_to(st[u].pop("l"), (WIN_BLK, LANES))

            for stage in (logits, softmax, values):
                for u in range(n):
                    stage(u)

        if dil == 1:
            for j0 in range(0, n_blocks, ATTN_UNITS):
                run_units([(0, j) for j in range(j0, min(j0 + ATTN_UNITS, n_blocks))])
        else:
            jb = min(n_blocks, ATTN_UNITS)
            rb = max(ATTN_UNITS // n_blocks, 1)
            for j0 in range(0, n_blocks, jb):
                def body(it, carry, j0=j0, jb=jb, rb=rb, run_units=run_units):
                    run_units([(it * rb + k, j) for k in range(rb) for j in range(j0, j0 + jb)])
                    return carry
                if dil == rb:
                    body(0, 0)
                else:
                    lax.fori_loop(0, dil // rb, body, 0)

    outers = [dil // min(dil, MAX_ROW_STRIDE) for _, dil in DIL_GROUPS]
    parts = max(outers)
    cnt = 2 * WIN_BLK
    for p in range(parts):
        for i0 in range(0, tq // parts, cnt):
            def rows_in(gi):
                if outers[gi] == parts:
                    return pl.ds(p * (tq // parts) + i0, cnt)
                assert outers[gi] == 1
                return strided(p + parts * i0, cnt, parts)
            m_all = jnp.maximum(jnp.maximum(m_ref[0, rows_in(0), :], m_ref[1, rows_in(1), :]),
                                m_ref[2, rows_in(2), :])
            num = jnp.zeros((cnt, LANES), F32)
            den = jnp.zeros((cnt, LANES), F32)
            for gi in range(N_GROUPS):
                wgt = jnp.exp2(m_ref[gi, rows_in(gi), :] - m_all)
                num = num + wgt * acc_ref[gi, rows_in(gi), :]
                den = den + wgt * l_ref[gi, rows_in(gi), :]
            o_ref[strided(p + parts * i0, cnt, parts), :] = num / den


def _attn_prompt(proj, bias_tiles, batch, seq):
    tq = ATTN_TILE
    nt = seq // tq
    in_specs = []
    args = []
    for gi, (_, dil) in enumerate(DIL_GROUPS):
        span = dil * WIN_BLK
        cq = (COL_QKV_B + gi * 3 * DIL_W) // LANES
        ck = cq + DIL_HEADS
        cv = ck + DIL_HEADS

        def cur(col):
            return pl.BlockSpec((tq, LANES), lambda b, i, h, col=col: (b * nt + i, col + h))

        def prev(col, span=span):
            per = tq // span
            return pl.BlockSpec(
                (span, LANES),
                lambda b, i, h, col=col, per=per: (jnp.maximum((b * nt + i) * per - 1, 0), col + h))

        in_specs += [cur(cq), prev(ck), cur(ck), prev(cv), cur(cv)]
        args += [proj] * 5
    in_specs.append(pl.BlockSpec((N_GROUPS, None, WIN_BLK, 2 * WIN_BLK), lambda b, i, h: (0, h, 0, 0)))
    args.append(bias_tiles)
    return pl.pallas_call(
        _attn_prompt_kernel,
        grid=(batch, nt, DIL_HEADS),
        in_specs=in_specs,
        out_specs=pl.BlockSpec((tq, LANES), lambda b, i, h: (b * nt + i, h)),
        out_shape=jax.ShapeDtypeStruct((batch * seq, DIL_W), F32),
        scratch_shapes=[pltpu.VMEM((N_GROUPS, tq, LANES), F32)] * 3
        + [pltpu.VMEM((N_GROUPS, WIN_BLK, 2 * WIN_BLK), F32), pltpu.VMEM((5, tq, LANES), F32)],
        compiler_params=pltpu.CompilerParams(
            dimension_semantics=("parallel", "parallel", "parallel"), vmem_limit_bytes=VMEM_LIMIT),
        name="attn_prompt",
    )(*args)


def _attn_sample_kernel(*refs):
    q_refs = refs[0:N_GROUPS]
    kn_refs = refs[N_GROUPS:2 * N_GROUPS]
    vn_refs = refs[2 * N_GROUPS:3 * N_GROUPS]
    kc_refs = refs[3 * N_GROUPS:4 * N_GROUPS]
    vc_refs = refs[4 * N_GROUPS:5 * N_GROUPS]
    bias0_ref, bias_ref, o_ref = refs[5 * N_GROUPS:5 * N_GROUPS + 3]
    m_ref, l_ref, acc_ref = refs[5 * N_GROUPS + 3:]
    t = pl.program_id(0)
    scale = HEAD_DIM ** -0.5

    @pl.when(t == 0)
    def _():
        for gi in range(N_GROUPS):
            s = jnp.sum(q_refs[gi][...] * kn_refs[gi][...], axis=-1, keepdims=True) * scale + bias0_ref[gi]
            m_ref[gi] = s
            l_ref[gi] = jnp.ones_like(s)
            acc_ref[gi] = vn_refs[gi][...]

    for gi in range(N_GROUPS):
        s = jnp.sum(q_refs[gi][...] * kc_refs[gi][...], axis=-1, keepdims=True) * scale + bias_ref[gi]
        m_old = m_ref[gi]
        m_new = jnp.maximum(m_old, s)
        alpha = jnp.exp(m_old - m_new)
        p = jnp.exp(s - m_new)
        l_ref[gi] = alpha * l_ref[gi] + p
        acc_ref[gi] = alpha * acc_ref[gi] + p * vc_refs[gi][...]
        m_ref[gi] = m_new

    @pl.when(t == pl.num_programs(0) - 1)
    def _():
        m_all = jnp.maximum(jnp.maximum(m_ref[0], m_ref[1]), m_ref[2])
        num = jnp.zeros(acc_ref.shape[1:], F32)
        den = jnp.zeros(m_ref.shape[1:], F32)
        for gi in range(N_GROUPS):
            wgt = jnp.exp(m_ref[gi] - m_all)
            num = num + wgt * acc_ref[gi]
            den = den + wgt * l_ref[gi]
        o_ref[...] = num / den


def _attn_sample(q_g, kn_g, vn_g, k_caches, v_caches, bias0, bias_steps):
    batch = q_g[0].shape[0]
    small = pl.BlockSpec((batch, DIL_HEADS, HEAD_DIM), lambda t: (0, 0, 0))
    in_specs = [small] * (3 * N_GROUPS)
    for _ in range(2):
        for _, dil in DIL_GROUPS:
            in_specs.append(pl.BlockSpec((None, batch, None, DIL_HEADS, HEAD_DIM),
                                         lambda t, dil=dil: (0, 0, t * dil, 0, 0)))
    in_specs.append(pl.BlockSpec((N_GROUPS, 1, DIL_HEADS, 1), lambda t: (0, 0, 0, 0)))
    in_specs.append(pl.BlockSpec((None, N_GROUPS, 1, DIL_HEADS, 1), lambda t: (t, 0, 0, 0, 0)))
    return pl.pallas_call(
        _attn_sample_kernel,
        grid=(WIN_BLK,),
        in_specs=in_specs,
        out_specs=pl.BlockSpec((batch, DIL_HEADS, HEAD_DIM), lambda t: (0, 0, 0)),
        out_shape=jax.ShapeDtypeStruct((batch, DIL_HEADS, HEAD_DIM), F32),
        scratch_shapes=[
            pltpu.VMEM((N_GROUPS, batch, DIL_HEADS, 1), F32),
            pltpu.VMEM((N_GROUPS, batch, DIL_HEADS, 1), F32),
            pltpu.VMEM((N_GROUPS, batch, DIL_HEADS, HEAD_DIM), F32),
        ],
        compiler_params=pltpu.CompilerParams(
            dimension_semantics=("arbitrary",), vmem_limit_bytes=VMEM_LIMIT),
        name="attn_sample",
    )(*q_g, *kn_g, *vn_g, *k_caches, *v_caches, bias0, bias_steps)


def _output_kernel(ob_ref, zb_ref, gateb_ref, ha_ref, x_ref, wpb_ref, wout_ref, gpost_ref, y_ref):
    yb = (ob_ref[...] * _silu(zb_ref[...])).astype(BF16)
    p_b = jnp.dot(yb, wpb_ref[...], preferred_element_type=F32)
    hid = ha_ref[...].astype(F32) + _sigmoid(gateb_ref[...]) * p_b
    out = jnp.dot(hid.astype(BF16), wout_ref[...], preferred_element_type=F32)
    ms = jnp.mean(out * out, axis=-1, keepdims=True)
    y_ref[...] = x_ref[...] + out * lax.rsqrt(ms + RMS_EPS) * gpost_ref[...]


def _output(o_b, proj, h_a, x2d, w_proj_b, w_out, g_post, tm):
    m = x2d.shape[0]
    return pl.pallas_call(
        _output_kernel,
        grid=(m // tm,),
        in_specs=[
            pl.BlockSpec((tm, DIL_W), lambda i: (i, 0)),
            pl.BlockSpec((tm, DIL_W), lambda i: (i, COL_Z_B // DIL_W)),
            pl.BlockSpec((tm, D_MODEL), lambda i: (i, COL_GATE_B // D_MODEL)),
            pl.BlockSpec((tm, D_MODEL), lambda i: (i, 0)),
            pl.BlockSpec((tm, D_MODEL), lambda i: (i, 0)),
            pl.BlockSpec((DIL_W, D_MODEL), lambda i: (0, 0)),
            pl.BlockSpec((D_MODEL, D_MODEL), lambda i: (0, 0)),
            pl.BlockSpec((1, D_MODEL), lambda i: (0, 0)),
        ],
        out_specs=pl.BlockSpec((tm, D_MODEL), lambda i: (i, 0)),
        out_shape=jax.ShapeDtypeStruct((m, D_MODEL), F32),
        compiler_params=pltpu.CompilerParams(
            dimension_semantics=("parallel",), vmem_limit_bytes=VMEM_LIMIT),
        name="output",
    )(o_b, proj, proj, h_a, x2d, w_proj_b, w_out, g_post)


def _rel_bucket(dist):
    max_exact = REL_BUCKETS // 2
    d = jnp.maximum(dist, 1).astype(F32)
    large = max_exact + (jnp.log(d / max_exact) / math.log(REL_MAX_DIST / max_exact)
                         * (REL_BUCKETS - max_exact)).astype(jnp.int32)
    large = jnp.minimum(large, REL_BUCKETS - 1)
    return jnp.where(dist < max_exact, dist, large)


def _group_biases(rel_table):
    out = []
    for gi, (win, dil) in enumerate(DIL_GROUPS):
        dist = jnp.arange(win // dil + 1, dtype=jnp.int32) * dil
        b = rel_table[_rel_bucket(dist)][:, gi * DIL_HEADS:(gi + 1) * DIL_HEADS]
        out.append(b.T.astype(F32))
    return out


def _bias_tiles(biases):
    period = 3 * WIN_BLK
    tiles = []
    for b in biases:
        u = jnp.concatenate([b[:, ::-1], jnp.full((DIL_HEADS, period - WIN_BLK - 1), NEG, F32)], axis=1)
        flat = jnp.tile(u, (1, WIN_BLK))[:, :WIN_BLK * (period - 1)]
        tiles.append(flat.reshape(DIL_HEADS, WIN_BLK, period - 1)[:, :, :2 * WIN_BLK])
    return jnp.stack(tiles, axis=0).astype(F32)


def _layer_weights(w_in, conv_w, a_log, dt_bias, g_head_norm, w_proj_a, w_proj_b, w_out):
    w_main = jnp.concatenate([w_in[:, :AB_OFFSET], w_in[:, AB_OFFSET + 2 * GDN_HEADS:]], axis=1).astype(BF16)
    w_ab = jnp.pad(w_in[:, AB_OFFSET:AB_OFFSET + 2 * GDN_HEADS],
                   ((0, 0), (0, LANES - 2 * GDN_HEADS))).astype(BF16)
    pad = (0, LANES - GDN_HEADS)
    return dict(
        w_main=w_main, w_ab=w_ab, conv_wt=conv_w.T.astype(F32),
        alog_pad=jnp.pad(a_log.astype(F32), pad)[None, :],
        dtb_pad=jnp.pad(dt_bias.astype(F32), pad)[None, :],
        ghn=g_head_norm.astype(F32)[None, :],
        w_proj_a=w_proj_a.astype(BF16), w_proj_b=w_proj_b.astype(BF16), w_out=w_out.astype(BF16))


def _split_heads(a2d, col, rows_per_batch, batch, keep):
    a = a2d.reshape(batch, rows_per_batch, -1)[:, rows_per_batch - keep:, col:col + DIL_W]
    return a.reshape(batch, keep, DIL_HEADS, HEAD_DIM)


def kernel(x_prompt, x_sample, state_gdn, state_conv, cache_k_w128, cache_v_w128, cache_k_w512,
           cache_v_w512, cache_k_w2048, cache_v_w2048, rel_table, g_pre, w_in, conv_w, a_log, dt_bias,
           g_head_norm, w_proj_a, w_proj_b, w_out, g_post):
    batch, seq, _ = x_prompt.shape
    dec_batch = x_sample.shape[0]
    k_caches = (cache_k_w128, cache_k_w512, cache_k_w2048)
    v_caches = (cache_v_w128, cache_v_w512, cache_v_w2048)
    biases = _group_biases(rel_table)
    bias_tiles = _bias_tiles(biases) * LOG2_E
    lw = _layer_weights(w_in[0], conv_w[0], a_log[0], dt_bias[0], g_head_norm[0], w_proj_a[0],
                        w_proj_b[0], w_out[0])
    g_pre_l = g_pre[0][None, :]
    g_post_l = g_post[0][None, :]

    xp = x_prompt.reshape(batch * seq, D_MODEL)
    proj, ab = _in_projection(xp, g_pre_l, lw["w_main"], lw["w_ab"], tm=2048, tn=1024)
    h_a, p_gdn = _gdn_prompt(proj, ab, lw["conv_wt"], lw["alog_pad"], lw["dtb_pad"], lw["ghn"],
                             lw["w_proj_a"], batch, seq)
    o_b = _attn_prompt(proj, bias_tiles, batch, seq)
    y_prompt = _output(o_b, proj, h_a, xp, lw["w_proj_b"], lw["w_out"], g_post_l, tm=1024)
    y_prompt = y_prompt.reshape(batch, seq, D_MODEL)
    p_conv = proj.reshape(batch, seq, MAIN_WIDTH)[:, seq - (CONV_WIDTH - 1):, :CONV_DIM]
    p_rows = []
    for gi, (win, _) in enumerate(DIL_GROUPS):
        keep = min(win, seq)
        ck = COL_QKV_B + gi * 3 * DIL_W + DIL_W
        p_rows += [_split_heads(proj, ck, seq, batch, keep), _split_heads(proj, ck + DIL_W, seq, batch, keep)]

    xs = x_sample.reshape(dec_batch, D_MODEL)
    proj_s, ab_s = _in_projection(xs, g_pre_l, lw["w_main"], lw["w_ab"], tm=dec_batch, tn=1024)
    sc_t = jnp.transpose(state_conv[0], (1, 0, 2))
    o_s, s_gdn = _gdn_sample(sc_t, proj_s, ab_s, lw["conv_wt"], lw["alog_pad"], lw["dtb_pad"],
                             state_gdn[0], nb=32)
    h_a_s = _gdn_sample_out(o_s, proj_s, lw["ghn"], lw["w_proj_a"])
    q_g, kn_g, vn_g = [], [], []
    for gi in range(N_GROUPS):
        cq = COL_QKV_B + gi * 3 * DIL_W
        q_g.append(proj_s[:, cq:cq + DIL_W].reshape(dec_batch, DIL_HEADS, HEAD_DIM))
        kn_g.append(proj_s[:, cq + DIL_W:cq + 2 * DIL_W].reshape(dec_batch, DIL_HEADS, HEAD_DIM))
        vn_g.append(proj_s[:, cq + 2 * DIL_W:cq + 3 * DIL_W].reshape(dec_batch, DIL_HEADS, HEAD_DIM))
    bias_all = jnp.stack(biases, axis=0)
    bias0 = bias_all[:, None, :, 0:1]
    bias_steps = jnp.transpose(bias_all[:, :, :0:-1], (2, 0, 1))[:, :, None, :, None]
    o_b_s = _attn_sample(q_g, kn_g, vn_g, k_caches, v_caches, bias0, bias_steps)
    y_sample = _output(o_b_s.reshape(dec_batch, DIL_W), proj_s, h_a_s, xs, lw["w_proj_b"], lw["w_out"],
                       g_post_l, tm=dec_batch)
    y_sample = y_sample.reshape(dec_batch, 1, D_MODEL)
    s_conv = jnp.concatenate([state_conv[0][:, 1:, :], proj_s[:, None, :CONV_DIM]], axis=1)
    s_rows = []
    for gi in range(N_GROUPS):
        s_rows += [kn_g[gi][:, None], vn_g[gi][:, None]]

    return (y_prompt, y_sample, p_gdn[None], p_conv[None], *[r[None] for r in p_rows],
            s_gdn[None], s_conv[None], *[r[None] for r in s_rows])
```

```python
import functools
import math

import jax
import jax.numpy as jnp
from jax import lax
from jax.experimental import pallas as pl
from jax.experimental.pallas import tpu as pltpu

F32 = jnp.float32
BF16 = jnp.bfloat16

D_MODEL = 1024
GDN_HEADS = 8
HEAD_DIM = 128
GDN_QK = GDN_HEADS * HEAD_DIM
CONV_DIM = 3 * GDN_QK
CONV_WIDTH = 4
DIL_GROUPS = ((128, 1), (512, 4), (2048, 16))
N_GROUPS = 3
DIL_HEADS = 4
DIL_W = DIL_HEADS * HEAD_DIM
WIN_BLK = 128
REL_BUCKETS = 32
REL_MAX_DIST = 2048
RMS_EPS = 1e-6
NEG = -1e30
LOG2_E = math.log2(math.e)

COL_QKV_A = 0
COL_Z_A = 3072
COL_QKV_B = 4096
COL_Z_B = 8704
COL_GATE_A = 9216
COL_GATE_B = 10240
MAIN_WIDTH = 11264
AB_OFFSET = 4096
LANES = 128

VMEM_LIMIT = 56 * 1024 * 1024

GDN_CHUNK = 64
GDN_SEQS = 4
GDN_SKEW = 0
ATTN_TILE = 2048
ATTN_UNITS = 16
MAX_ROW_STRIDE = 4


def _silu(x):
    half = 0.5 * x
    return half * jnp.tanh(half) + half


def _sigmoid(x):
    return 1.0 / (1.0 + jnp.exp(-x))


def _softplus(x):
    return jnp.maximum(x, 0.0) + jnp.log(1.0 + jnp.exp(-jnp.abs(x)))


def _dot(a, b):
    return jnp.dot(a.astype(BF16), b.astype(BF16), preferred_element_type=F32)


def _dot_nt(a, b):
    return lax.dot_general(a.astype(BF16), b.astype(BF16), (((1,), (1,)), ((), ())),
                           preferred_element_type=F32)


def _dot_tn(a, b):
    return lax.dot_general(a.astype(BF16), b.astype(BF16), (((0,), (0,)), ((), ())),
                           preferred_element_type=F32)


def _inproj_kernel(x_ref, g_ref, w_ref, wab_ref, o_ref, ab_ref, xn_ref):
    @pl.when(pl.program_id(1) == 0)
    def _():
        x = x_ref[...]
        ms = jnp.mean(x * x, axis=-1, keepdims=True)
        xn = (x * lax.rsqrt(ms + RMS_EPS) * g_ref[...]).astype(BF16)
        xn_ref[...] = xn
        ab_ref[...] = jnp.dot(xn, wab_ref[...], preferred_element_type=F32)

    o_ref[...] = jnp.dot(xn_ref[...], w_ref[...], preferred_element_type=F32)


def _in_projection(x2d, g_pre, w_main, w_ab, tm, tn):
    m = x2d.shape[0]
    return pl.pallas_call(
        _inproj_kernel,
        grid=(m // tm, MAIN_WIDTH // tn),
        in_specs=[
            pl.BlockSpec((tm, D_MODEL), lambda i, j: (i, 0)),
            pl.BlockSpec((1, D_MODEL), lambda i, j: (0, 0)),
            pl.BlockSpec((D_MODEL, tn), lambda i, j: (0, j)),
            pl.BlockSpec((D_MODEL, LANES), lambda i, j: (0, 0)),
        ],
        out_specs=[
            pl.BlockSpec((tm, tn), lambda i, j: (i, j)),
            pl.BlockSpec((tm, LANES), lambda i, j: (i, 0)),
        ],
        out_shape=[
            jax.ShapeDtypeStruct((m, MAIN_WIDTH), F32),
            jax.ShapeDtypeStruct((m, LANES), F32),
        ],
        scratch_shapes=[pltpu.VMEM((tm, D_MODEL), BF16)],
        compiler_params=pltpu.CompilerParams(
            dimension_semantics=("parallel", "arbitrary"), vmem_limit_bytes=VMEM_LIMIT),
        name="in_projection",
    )(x2d, g_pre, w_main, w_ab)


GROUP_HEADS = 4


def _block_diag_rhs(x4, half_masks):
    c = x4.shape[0]
    zeros = jnp.zeros((c, LANES), x4.dtype)
    row_blocks = []
    for h in range(GROUP_HEADS):
        piece = x4[:, (h // 2) * LANES:(h // 2 + 1) * LANES] * half_masks[h % 2]
        row_blocks.append(jnp.concatenate([piece if v == h // 2 else zeros for v in range(2)], axis=1))
    return jnp.concatenate(row_blocks, axis=0)


def _gdn_prompt_kernel(qkv_ref, z_ref, ab_ref, gate_ref, convw_ref, alog_ref, dtb_ref, ghn_ref,
                       wproj_ref, ha_ref, sfin_ref, xpad_ref, qkvn_ref, o_ref, s_ref,
                       u_ref, w_ref, qd_ref, kd_ref, qk_ref):
    c = GDN_CHUNK
    tl = GDN_SEQS * c
    gw = GROUP_HEADS * c
    step = pl.program_id(1)

    def conv_stage(i):
        for blk in range(CONV_DIM // LANES):
            cols = slice(blk * LANES, (blk + 1) * LANES)
            xpad_ref[i, blk, 8:8 + c, :] = qkv_ref[i, :, cols]
            for phase in range(2):
                acc = None
                for w in range(CONV_WIDTH):
                    tap = xpad_ref[i, blk, pl.ds(5 + phase + w, c // 2, stride=2), :] * convw_ref[w:w + 1, cols]
                    acc = tap if acc is None else acc + tap
                y = _silu(acc)
                if blk < 2 * GDN_HEADS:
                    y = y * lax.rsqrt(jnp.sum(y * y, axis=-1, keepdims=True) + RMS_EPS)
                    if blk < GDN_HEADS:
                        y = y * (HEAD_DIM ** -0.5)
                qkvn_ref[blk, pl.ds(i * c + phase, c // 2, stride=2), :] = y
            xpad_ref[i, blk, 5:8, :] = xpad_ref[i, blk, c + 5:c + 8, :]

    @pl.when(step == 0)
    def _():
        xpad_ref[:, :, 0:8, :] = jnp.zeros((GDN_SEQS, CONV_DIM // LANES, 8, LANES), F32)
        s_ref[...] = jnp.zeros_like(s_ref)

    ab = ab_ref[...].reshape(tl, LANES)
    g_full = -jnp.exp(alog_ref[...]) * _softplus(ab + dtb_ref[...])
    beta_full = _sigmoid(ab)
    row = lax.broadcasted_iota(jnp.int32, (tl, tl), 0)
    col = lax.broadcasted_iota(jnp.int32, (tl, tl), 1)
    same_chunk = (row // c) == (col // c)
    tri = jnp.where(same_chunk & (col <= row), 1.0, 0.0).astype(BF16)
    g_hi = g_full.astype(BF16)
    g_rest = g_full - g_hi.astype(F32)
    g_mid = g_rest.astype(BF16)
    g_lo = (g_rest - g_mid.astype(F32)).astype(BF16)
    g_cum = jnp.dot(jnp.concatenate([tri, tri, tri], axis=1), jnp.concatenate([g_hi, g_mid, g_lo], axis=0),
                    preferred_element_type=F32)
    g_cum_t = g_cum.T
    g_last = jnp.concatenate(
        [jnp.broadcast_to(g_cum[i * c + c - 1:(i + 1) * c, :], (c, LANES)) for i in range(GDN_SEQS)], axis=0)
    exp_g = jnp.exp(g_cum)
    exp_rest = jnp.exp(g_last - g_cum)
    exp_last = jnp.exp(g_last)

    ci = lax.broadcasted_iota(jnp.int32, (c, gw), 0)
    cj = lax.broadcasted_iota(jnp.int32, (c, gw), 1) % c
    causal4 = cj <= ci
    strict4 = cj < ci
    half_lane = lax.broadcasted_iota(jnp.int32, (c, LANES), 1) // c
    half_masks = [jnp.where(half_lane == k, 1.0, 0.0).astype(BF16) for k in range(2)]

    def lanes_of(col_values, width):
        return jnp.concatenate([jnp.broadcast_to(v, (c, width)) for v in col_values], axis=1)

    n_groups = GDN_HEADS // GROUP_HEADS
    zero_s = jnp.zeros((HEAD_DIM, HEAD_DIM), BF16)
    zero_v = jnp.zeros((c, HEAD_DIM), BF16)

    def group_cols(hg):
        return slice(hg * GROUP_HEADS * LANES, (hg + 1) * GROUP_HEADS * LANES)

    def bd_dot(lhs, x4):
        return jnp.dot(lhs.astype(BF16), _block_diag_rhs(x4.astype(BF16), half_masks),
                       preferred_element_type=F32)

    def seq_stages(i):
        rows = slice(i * c, (i + 1) * c)
        st = {}

        def build():
            a4s = []
            for hg in range(n_groups):
                heads = range(hg * GROUP_HEADS, (hg + 1) * GROUP_HEADS)
                cols = group_cols(hg)

                def head_blocks(first_blk):
                    return jnp.concatenate([qkvn_ref[first_blk + h, rows, :] for h in heads], axis=1)

                q4 = head_blocks(0)
                k4 = head_blocks(GDN_HEADS)
                v4 = head_blocks(2 * GDN_HEADS)
                beta_e = lanes_of([beta_full[rows, GDN_HEADS + h:GDN_HEADS + h + 1] for h in heads], LANES)
                eg_e = lanes_of([exp_g[rows, h:h + 1] for h in heads], LANES)
                er_e = lanes_of([exp_rest[rows, h:h + 1] for h in heads], LANES)
                gi4 = lanes_of([g_cum[rows, h:h + 1] for h in heads], c)
                gj4 = jnp.concatenate([g_cum_t[h:h + 1, rows] for h in heads], axis=1)
                decay4 = jnp.exp(jnp.where(causal4, gi4 - gj4, NEG))
                kb4 = k4 * beta_e
                k4b = k4.astype(BF16)
                k_rhs = jnp.concatenate(
                    [jnp.concatenate([k4b[:, g * LANES:(g + 1) * LANES] if g == h else zero_v
                                      for g in range(GROUP_HEADS)], axis=1)
                     for h in range(GROUP_HEADS)], axis=0)
                kkqk = lax.dot_general(jnp.concatenate([kb4, q4], axis=0).astype(BF16), k_rhs,
                                       (((1,), (1,)), ((), ())), preferred_element_type=F32)
                a4s.append(jnp.where(strict4, kkqk[:c] * decay4, 0.0))
                qk_ref[rows, hg * gw:(hg + 1) * gw] = kkqk[c:] * decay4
                u_ref[rows, cols] = v4 * beta_e
                w_ref[rows, cols] = kb4 * eg_e
                qd_ref[rows, cols] = q4 * eg_e
                kd_ref[rows, cols] = k4 * er_e
            st["a"] = a4s

        def square():
            st["e"] = [-a4 for a4 in st["a"]]
            st["x"] = [bd_dot(a4, a4) for a4 in st["a"]]

        def double():
            boths = [bd_dot(jnp.concatenate([e, x], axis=0), x) for e, x in zip(st["e"], st["x"])]
            st["e"] = [e + x + both[:c] for e, x, both in zip(st["e"], st["x"], boths)]
            st["x"] = [both[c:] for both in boths]

        def last_factor():
            st["e"] = [e + x + bd_dot(e, x) for e, x in zip(st["e"], st["x"])]

        def solve():
            for hg, e4 in enumerate(st["e"]):
                cols = group_cols(hg)
                vb4 = u_ref[rows, cols]
                kbe4 = w_ref[rows, cols]
                us, ws = [], []
                for k in range(GROUP_HEADS):
                    hc = slice(k * LANES, (k + 1) * LANES)
                    rhs = jnp.concatenate([vb4[:, hc], kbe4[:, hc]], axis=1)
                    uw = rhs + _dot(e4[:, k * c:(k + 1) * c], rhs)
                    us.append(uw[:, :LANES])
                    ws.append(uw[:, LANES:])
                u_ref[rows, cols] = jnp.concatenate(us, axis=1)
                w_ref[rows, cols] = jnp.concatenate(ws, axis=1)

        def pair_cols(p):
            return slice(2 * p * LANES, (2 * p + 2) * LANES)

        def state_products():
            wsqs = []
            for p in range(GDN_HEADS // 2):
                cols = pair_cols(p)
                s_bd = jnp.concatenate(
                    [jnp.concatenate([s_ref[i, 2 * p].astype(BF16), zero_s], axis=1),
                     jnp.concatenate([zero_s, s_ref[i, 2 * p + 1].astype(BF16)], axis=1)], axis=0)
                wq = jnp.concatenate([w_ref[rows, cols], qd_ref[rows, cols]], axis=0).astype(BF16)
                wsqs.append(jnp.dot(wq, s_bd, preferred_element_type=F32))
            st["wsq"] = wsqs

        def outputs():
            vnbs = []
            for p, wsq in enumerate(st["wsq"]):
                cols = pair_cols(p)
                vnb = (u_ref[rows, cols] - wsq[:c]).astype(BF16)
                vnbs.append(vnb)
                vn_bd = jnp.concatenate([jnp.concatenate([vnb[:, :LANES], zero_v], axis=1),
                                         jnp.concatenate([zero_v, vnb[:, LANES:]], axis=1)], axis=0)
                qk2 = qk_ref[rows, 2 * p * c:(2 * p + 2) * c]
                o_ref[rows, cols] = wsq[c:] + jnp.dot(qk2.astype(BF16), vn_bd, preferred_element_type=F32)
            st["vn"] = vnbs

        def state_update():
            for p, vnb in enumerate(st["vn"]):
                kd2 = kd_ref[rows, pair_cols(p)]
                for k in range(2):
                    h = 2 * p + k
                    hc = slice(k * LANES, (k + 1) * LANES)
                    el = exp_last[i * c:i * c + 1, h:h + 1]
                    s_ref[i, h] = s_ref[i, h] * el + _dot_tn(kd2[:, hc], vnb[:, hc])

        doublings = int(math.log2(c)) - 2
        return ([functools.partial(conv_stage, i), build, square] + [double] * doublings
                + [last_factor, solve, state_products, outputs, state_update])

    stage_lists = [seq_stages(i) for i in range(GDN_SEQS)]
    n_stages = len(stage_lists[0])
    for t in range(n_stages + GDN_SKEW * (GDN_SEQS - 1)):
        for i in range(GDN_SEQS):
            stage = t - GDN_SKEW * i
            if 0 <= stage < n_stages:
                stage_lists[i][stage]()

    for h in range(GDN_HEADS):
        cols = slice(h * LANES, (h + 1) * LANES)
        o = o_ref[:, cols]
        o = o * lax.rsqrt(jnp.mean(o * o, axis=-1, keepdims=True) + RMS_EPS) * ghn_ref[...]
        o_ref[:, cols] = o * _silu(z_ref[:, :, cols].reshape(tl, LANES))
    p_a = jnp.dot(o_ref[...].astype(BF16), wproj_ref[...], preferred_element_type=F32)
    h_a = _sigmoid(gate_ref[...].reshape(tl, D_MODEL)) * p_a
    ha_ref[...] = h_a.reshape(GDN_SEQS, c, D_MODEL).astype(ha_ref.dtype)

    @pl.when(step == pl.num_programs(1) - 1)
    def _():
        sfin_ref[...] = s_ref[...]


def _gdn_prompt(proj, ab, conv_wt, alog_pad, dtb_pad, ghn, w_proj_a, batch, seq):
    c = GDN_CHUNK
    ns = GDN_SEQS
    tl = ns * c
    proj3 = proj.reshape(batch, seq, MAIN_WIDTH)
    ab3 = ab.reshape(batch, seq, LANES)
    h_a, s_fin = pl.pallas_call(
        _gdn_prompt_kernel,
        grid=(batch // ns, seq // c),
        in_specs=[
            pl.BlockSpec((ns, c, CONV_DIM), lambda b, l: (b, l, COL_QKV_A // CONV_DIM)),
            pl.BlockSpec((ns, c, GDN_QK), lambda b, l: (b, l, COL_Z_A // GDN_QK)),
            pl.BlockSpec((ns, c, LANES), lambda b, l: (b, l, 0)),
            pl.BlockSpec((ns, c, D_MODEL), lambda b, l: (b, l, COL_GATE_A // D_MODEL)),
            pl.BlockSpec((CONV_WIDTH, CONV_DIM), lambda b, l: (0, 0)),
            pl.BlockSpec((1, LANES), lambda b, l: (0, 0)),
            pl.BlockSpec((1, LANES), lambda b, l: (0, 0)),
            pl.BlockSpec((1, LANES), lambda b, l: (0, 0)),
            pl.BlockSpec((GDN_QK, D_MODEL), lambda b, l: (0, 0)),
        ],
        out_specs=[
            pl.BlockSpec((ns, c, D_MODEL), lambda b, l: (b, l, 0)),
            pl.BlockSpec((ns, GDN_HEADS, HEAD_DIM, HEAD_DIM), lambda b, l: (b, 0, 0, 0)),
        ],
        out_shape=[
            jax.ShapeDtypeStruct((batch, seq, D_MODEL), BF16),
            jax.ShapeDtypeStruct((batch, GDN_HEADS, HEAD_DIM, HEAD_DIM), F32),
        ],
        scratch_shapes=[
            pltpu.VMEM((ns, CONV_DIM // LANES, c + 8, LANES), F32),
            pltpu.VMEM((CONV_DIM // LANES, tl, LANES), F32),
            pltpu.VMEM((tl, GDN_QK), F32),
            pltpu.VMEM((ns, GDN_HEADS, HEAD_DIM, HEAD_DIM), F32),
            pltpu.VMEM((tl, GDN_QK), F32),
            pltpu.VMEM((tl, GDN_QK), F32),
            pltpu.VMEM((tl, GDN_QK), F32),
            pltpu.VMEM((tl, GDN_QK), F32),
            pltpu.VMEM((tl, GDN_HEADS * GDN_CHUNK), F32),
        ],
        compiler_params=pltpu.CompilerParams(
            dimension_semantics=("parallel", "arbitrary"), vmem_limit_bytes=VMEM_LIMIT),
        name="gdn_prompt",
    )(proj3, proj3, ab3, proj3, conv_wt, alog_pad, dtb_pad, ghn, w_proj_a)
    return h_a.reshape(batch * seq, D_MODEL), s_fin


def _gdn_sample_kernel(sc_q_ref, sc_k_ref, sc_v_ref, nq_ref, nk_ref, nv_ref, cwq_ref, cwk_ref, cwv_ref,
                       ab_ref, alog_ref, dtb_ref, s_ref, o_ref, snew_ref):
    h = pl.program_id(0)
    nb = nq_ref.shape[0]

    def conv_silu(sc_ref, new_ref, cw_ref):
        acc = new_ref[...] * cw_ref[3:4, :]
        for w in range(CONV_WIDTH - 1):
            acc = acc + sc_ref[w] * cw_ref[w:w + 1, :]
        return _silu(acc)

    q = conv_silu(sc_q_ref, nq_ref, cwq_ref)
    k = conv_silu(sc_k_ref, nk_ref, cwk_ref)
    v = conv_silu(sc_v_ref, nv_ref, cwv_ref)
    q = q * lax.rsqrt(jnp.sum(q * q, axis=-1, keepdims=True) + RMS_EPS) * (HEAD_DIM ** -0.5)
    k = k * lax.rsqrt(jnp.sum(k * k, axis=-1, keepdims=True) + RMS_EPS)

    ab = ab_ref[...]
    lane = lax.broadcasted_iota(jnp.int32, ab.shape, 1)
    g_full = -jnp.exp(alog_ref[...]) * _softplus(ab + dtb_ref[...])
    g = jnp.sum(jnp.where(lane == h, g_full, 0.0), axis=-1, keepdims=True)
    beta = jnp.sum(jnp.where(lane == h + GDN_HEADS, _sigmoid(ab), 0.0), axis=-1, keepdims=True)
    eg = jnp.exp(g)

    eye = (lax.broadcasted_iota(jnp.int32, (HEAD_DIM, HEAD_DIM), 0)
           == lax.broadcasted_iota(jnp.int32, (HEAD_DIM, HEAD_DIM), 1))

    def as_column(row):
        return jnp.sum(jnp.where(eye, row, 0.0), axis=1, keepdims=True)

    for b in range(nb):
        s = s_ref[b]
        k_col = as_column(k[b:b + 1, :])
        q_col = as_column(q[b:b + 1, :])
        e = eg[b:b + 1, :]
        ks = jnp.sum(k_col * s, axis=0, keepdims=True)
        v_new = beta[b:b + 1, :] * (v[b:b + 1, :] - e * ks)
        s_new = s * e + k_col * v_new
        snew_ref[b] = s_new
        o_ref[b:b + 1, :] = jnp.sum(q_col * s_new, axis=0, keepdims=True)


def _gdn_sample(state_conv_t, proj, ab, conv_wt, alog_pad, dtb_pad, state, nb):
    batch = proj.shape[0]
    hq = GDN_QK // LANES
    return pl.pallas_call(
        _gdn_sample_kernel,
        grid=(GDN_HEADS, batch // nb),
        in_specs=[
            pl.BlockSpec((CONV_WIDTH - 1, nb, LANES), lambda h, i: (0, i, h)),
            pl.BlockSpec((CONV_WIDTH - 1, nb, LANES), lambda h, i: (0, i, hq + h)),
            pl.BlockSpec((CONV_WIDTH - 1, nb, LANES), lambda h, i: (0, i, 2 * hq + h)),
            pl.BlockSpec((nb, LANES), lambda h, i: (i, h)),
            pl.BlockSpec((nb, LANES), lambda h, i: (i, hq + h)),
            pl.BlockSpec((nb, LANES), lambda h, i: (i, 2 * hq + h)),
            pl.BlockSpec((CONV_WIDTH, LANES), lambda h, i: (0, h)),
            pl.BlockSpec((CONV_WIDTH, LANES), lambda h, i: (0, hq + h)),
            pl.BlockSpec((CONV_WIDTH, LANES), lambda h, i: (0, 2 * hq + h)),
            pl.BlockSpec((nb, LANES), lambda h, i: (i, 0)),
            pl.BlockSpec((1, LANES), lambda h, i: (0, 0)),
            pl.BlockSpec((1, LANES), lambda h, i: (0, 0)),
            pl.BlockSpec((nb, None, HEAD_DIM, HEAD_DIM), lambda h, i: (i, h, 0, 0)),
        ],
        out_specs=[
            pl.BlockSpec((nb, LANES), lambda h, i: (i, h)),
            pl.BlockSpec((nb, None, HEAD_DIM, HEAD_DIM), lambda h, i: (i, h, 0, 0)),
        ],
        out_shape=[
            jax.ShapeDtypeStruct((batch, GDN_QK), F32),
            jax.ShapeDtypeStruct(state.shape, F32),
        ],
        compiler_params=pltpu.CompilerParams(
            dimension_semantics=("parallel", "parallel"), vmem_limit_bytes=VMEM_LIMIT),
        name="gdn_sample",
    )(state_conv_t, state_conv_t, state_conv_t, proj, proj, proj, conv_wt, conv_wt, conv_wt,
      ab, alog_pad, dtb_pad, state)


def _gdn_sample_out_kernel(o_ref, z_ref, gate_ref, ghn_ref, wproj_ref, ha_ref, y_ref):
    for h in range(GDN_HEADS):
        cols = slice(h * LANES, (h + 1) * LANES)
        o = o_ref[:, cols]
        o = o * lax.rsqrt(jnp.mean(o * o, axis=-1, keepdims=True) + RMS_EPS) * ghn_ref[...]
        y_ref[:, cols] = o * _silu(z_ref[:, cols])
    p_a = jnp.dot(y_ref[...].astype(BF16), wproj_ref[...], preferred_element_type=F32)
    ha_ref[...] = (_sigmoid(gate_ref[...]) * p_a).astype(ha_ref.dtype)


def _gdn_sample_out(o, proj, ghn, w_proj_a):
    batch = o.shape[0]
    return pl.pallas_call(
        _gdn_sample_out_kernel,
        grid=(1,),
        in_specs=[
            pl.BlockSpec((batch, GDN_QK), lambda i: (0, 0)),
            pl.BlockSpec((batch, GDN_QK), lambda i: (0, COL_Z_A // GDN_QK)),
            pl.BlockSpec((batch, D_MODEL), lambda i: (0, COL_GATE_A // D_MODEL)),
            pl.BlockSpec((1, LANES), lambda i: (0, 0)),
            pl.BlockSpec((GDN_QK, D_MODEL), lambda i: (0, 0)),
        ],
        out_specs=pl.BlockSpec((batch, D_MODEL), lambda i: (0, 0)),
        out_shape=jax.ShapeDtypeStruct((batch, D_MODEL), BF16),
        scratch_shapes=[pltpu.VMEM((batch, GDN_QK), F32)],
        compiler_params=pltpu.CompilerParams(vmem_limit_bytes=VMEM_LIMIT),
        name="gdn_sample_out",
    )(o, proj, proj, ghn, w_proj_a)


def _attn_prompt_kernel(*refs):
    ins = refs[:5 * N_GROUPS]
    bias_ref = refs[5 * N_GROUPS]
    o_ref = refs[5 * N_GROUPS + 1]
    acc_ref, m_ref, l_ref, bias_start_ref, split_ref = refs[5 * N_GROUPS + 2:]
    tq = ATTN_TILE
    first_tile = pl.program_id(1) == 0
    scale = (HEAD_DIM ** -0.5) * LOG2_E
    lane2 = lax.broadcasted_iota(jnp.int32, (1, 2 * WIN_BLK), 1)
    start_mask = jnp.where(first_tile & (lane2 < WIN_BLK), NEG, 0.0).astype(F32)
    for gi in range(N_GROUPS):
        bias_start_ref[gi] = bias_ref[gi] + start_mask

    def strided(start, n, stride):
        return pl.ds(start, n, stride=stride) if stride > 1 else pl.ds(start, n)

    for gi, (_, dil) in enumerate(DIL_GROUPS):
        srcs = ins[5 * gi:5 * gi + 5]
        n_blocks = tq // (dil * WIN_BLK)
        inner = min(dil, MAX_ROW_STRIDE)
        outer = dil // inner
        if outer > 1:
            for a, src in enumerate(srcs):
                n = src.shape[0]
                for p in range(outer):
                    split_ref[a, p * (n // outer):(p + 1) * (n // outer), :] = src[strided(p, n // outer, outer), :]
            srcs = [split_ref.at[a] for a in range(5)]
        q_ref, kp_ref, kc_ref, vp_ref, vc_ref = srcs

        def rows_of(ref_rows, r, first, count, inner=inner, outer=outer):
            if outer == 1:
                return strided(r + inner * first, count, inner)
            part = r % outer
            return strided(part * (ref_rows // outer) + r // outer + inner * first, count, inner)

        def run_units(units, q_ref=q_ref, kp_ref=kp_ref, kc_ref=kc_ref, vp_ref=vp_ref, vc_ref=vc_ref,
                      gi=gi, rows_of=rows_of):
            span = kp_ref.shape[0]
            n = len(units)
            st = [{} for _ in units]

            def logits(u):
                r, j = units[u]
                q = (q_ref[rows_of(tq, r, WIN_BLK * j, WIN_BLK), :] * scale).astype(BF16)
                if j == 0:
                    prev = rows_of(span, r, 0, WIN_BLK)
                    cur = rows_of(tq, r, 0, WIN_BLK)
                    kk = jnp.concatenate([kp_ref[prev, :], kc_ref[cur, :]], axis=0).astype(BF16)
                    st[u]["v"] = (vp_ref, prev, vc_ref, cur)
                else:
                    rows = rows_of(tq, r, WIN_BLK * (j - 1), 2 * WIN_BLK)
                    kk = kc_ref[rows, :].astype(BF16)
                    st[u]["v"] = (vc_ref, rows)
                st[u]["s"] = lax.dot_general(q, kk, (((1,), (1,)), ((), ())), preferred_element_type=F32)

            def softmax(u):
                r, j = units[u]
                s = st[u].pop("s") + (bias_start_ref[gi] if j == 0 else bias_ref[gi])
                m = jnp.max(s, axis=-1, keepdims=True)
                p = jnp.exp2(s - m)
                st[u]["m"] = m
                st[u]["l"] = jnp.sum(p, axis=-1, keepdims=True)
                st[u]["p"] = p.astype(BF16)

            def values(u):
                r, j = units[u]
                v = st[u].pop("v")
                if len(v) == 4:
                    vv = jnp.concatenate([v[0][v[1], :], v[2][v[3], :]], axis=0).astype(BF16)
                else:
                    vv = v[0][v[1], :].astype(BF16)
                acc = jnp.dot(st[u].pop("p"), vv, preferred_element_type=F32)
                q_rows = rows_of(tq, r, WIN_BLK * j, WIN_BLK)
                acc_ref[gi, q_rows, :] = acc
                m_ref[gi, q_rows, :] = jnp.broadcast_to(st[u].pop("m"), (WIN_BLK, LANES))
                l_ref[gi, q_rows, :] = jnp.broadcast_to(st[u].pop("l"), (WIN_BLK, LANES))

            for stage in (logits, softmax, values):
                for u in range(n):
                    stage(u)

        if dil == 1:
            for j0 in range(0, n_blocks, ATTN_UNITS):
                run_units([(0, j) for j in range(j0, min(j0 + ATTN_UNITS, n_blocks))])
        else:
            jb = min(n_blocks, ATTN_UNITS)
            rb = max(ATTN_UNITS // n_blocks, 1)
            for j0 in range(0, n_blocks, jb):
                def body(it, carry, j0=j0, jb=jb, rb=rb, run_units=run_units):
                    run_units([(it * rb + k, j) for k in range(rb) for j in range(j0, j0 + jb)])
                    return carry
                if dil == rb:
                    body(0, 0)
                else:
                    lax.fori_loop(0, dil // rb, body, 0)

    outers = [dil // min(dil, MAX_ROW_STRIDE) for _, dil in DIL_GROUPS]
    parts = max(outers)
    cnt = 2 * WIN_BLK
    for p in range(parts):
        for i0 in range(0, tq // parts, cnt):
            def rows_in(gi):
                if outers[gi] == parts:
                    return pl.ds(p * (tq // parts) + i0, cnt)
                assert outers[gi] == 1
                return strided(p + parts * i0, cnt, parts)
            m_all = jnp.maximum(jnp.maximum(m_ref[0, rows_in(0), :], m_ref[1, rows_in(1), :]),
                                m_ref[2, rows_in(2), :])
            num = jnp.zeros((cnt, LANES), F32)
            den = jnp.zeros((cnt, LANES), F32)
            for gi in range(N_GROUPS):
                wgt = jnp.exp2(m_ref[gi, rows_in(gi), :] - m_all)
                num = num + wgt * acc_ref[gi, rows_in(gi), :]
                den = den + wgt * l_ref[gi, rows_in(gi), :]
            o_ref[strided(p + parts * i0, cnt, parts), :] = num / den


def _attn_prompt(proj, bias_tiles, batch, seq):
    tq = ATTN_TILE
    nt = seq // tq
    in_specs = []
    args = []
    for gi, (_, dil) in enumerate(DIL_GROUPS):
        span = dil * WIN_BLK
        cq = (COL_QKV_B + gi * 3 * DIL_W) // LANES
        ck = cq + DIL_HEADS
        cv = ck + DIL_HEADS

        def cur(col):
            return pl.BlockSpec((tq, LANES), lambda b, i, h, col=col: (b * nt + i, col + h))

        def prev(col, span=span):
            per = tq // span
            return pl.BlockSpec(
                (span, LANES),
                lambda b, i, h, col=col, per=per: (jnp.maximum((b * nt + i) * per - 1, 0), col + h))

        in_specs += [cur(cq), prev(ck), cur(ck), prev(cv), cur(cv)]
        args += [proj] * 5
    in_specs.append(pl.BlockSpec((N_GROUPS, None, WIN_BLK, 2 * WIN_BLK), lambda b, i, h: (0, h, 0, 0)))
    args.append(bias_tiles)
    return pl.pallas_call(
        _attn_prompt_kernel,
        grid=(batch, nt, DIL_HEADS),
        in_specs=in_specs,
        out_specs=pl.BlockSpec((tq, LANES), lambda b, i, h: (b * nt + i, h)),
        out_shape=jax.ShapeDtypeStruct((batch * seq, DIL_W), F32),
        scratch_shapes=[pltpu.VMEM((N_GROUPS, tq, LANES), F32)] * 3
        + [pltpu.VMEM((N_GROUPS, WIN_BLK, 2 * WIN_BLK), F32), pltpu.VMEM((5, tq, LANES), F32)],
        compiler_params=pltpu.CompilerParams(
            dimension_semantics=("parallel", "parallel", "parallel"), vmem_limit_bytes=VMEM_LIMIT),
        name="attn_prompt",
    )(*args)


def _attn_sample_kernel(*refs):
    q_refs = refs[0:N_GROUPS]
    kn_refs = refs[N_GROUPS:2 * N_GROUPS]
    vn_refs = refs[2 * N_GROUPS:3 * N_GROUPS]
    kc_refs = refs[3 * N_GROUPS:4 * N_GROUPS]
    vc_refs = refs[4 * N_GROUPS:5 * N_GROUPS]
    bias0_ref, bias_ref, o_ref = refs[5 * N_GROUPS:5 * N_GROUPS + 3]
    m_ref, l_ref, acc_ref = refs[5 * N_GROUPS + 3:]
    t = pl.program_id(0)
    scale = HEAD_DIM ** -0.5

    def logits(gi, k, bias):
        s = jnp.sum(q_refs[gi][...] * k, axis=-1, keepdims=True) * scale + bias
        return jnp.broadcast_to(s, k.shape)

    @pl.when(t == 0)
    def _():
        for gi in range(N_GROUPS):
            s = logits(gi, kn_refs[gi][...], bias0_ref[gi])
            m_ref[gi] = s
            l_ref[gi] = jnp.ones_like(s)
            acc_ref[gi] = vn_refs[gi][...]

    for gi in range(N_GROUPS):
        s = logits(gi, kc_refs[gi][...], bias_ref[gi])
        m_old = m_ref[gi]
        m_new = jnp.maximum(m_old, s)
        alpha = jnp.exp(m_old - m_new)
        p = jnp.exp(s - m_new)
        l_ref[gi] = alpha * l_ref[gi] + p
        acc_ref[gi] = alpha * acc_ref[gi] + p * vc_refs[gi][...]
        m_ref[gi] = m_new

    @pl.when(t == pl.num_programs(0) - 1)
    def _():
        m_all = jnp.maximum(jnp.maximum(m_ref[0], m_ref[1]), m_ref[2])
        num = jnp.zeros(acc_ref.shape[1:], F32)
        den = jnp.zeros(m_ref.shape[1:], F32)
        for gi in range(N_GROUPS):
            wgt = jnp.exp(m_ref[gi] - m_all)
            num = num + wgt * acc_ref[gi]
            den = den + wgt * l_ref[gi]
        o_ref[...] = num / den


def _attn_sample(q_g, kn_g, vn_g, k_caches, v_caches, bias0, bias_steps):
    batch = q_g[0].shape[0]
    small = pl.BlockSpec((batch, DIL_HEADS, HEAD_DIM), lambda t: (0, 0, 0))
    in_specs = [small] * (3 * N_GROUPS)
    for _ in range(2):
        for _, dil in DIL_GROUPS:
            in_specs.append(pl.BlockSpec((None, batch, None, DIL_HEADS, HEAD_DIM),
                                         lambda t, dil=dil: (0, 0, t * dil, 0, 0)))
    in_specs.append(pl.BlockSpec((N_GROUPS, 1, DIL_HEADS, 1), lambda t: (0, 0, 0, 0)))
    in_specs.append(pl.BlockSpec((None, N_GROUPS, 1, DIL_HEADS, 1), lambda t: (t, 0, 0, 0, 0)))
    return pl.pallas_call(
        _attn_sample_kernel,
        grid=(WIN_BLK,),
        in_specs=in_specs,
        out_specs=pl.BlockSpec((batch, DIL_HEADS, HEAD_DIM), lambda t: (0, 0, 0)),
        out_shape=jax.ShapeDtypeStruct((batch, DIL_HEADS, HEAD_DIM), F32),
        scratch_shapes=[
            pltpu.VMEM((N_GROUPS, batch, DIL_HEADS, HEAD_DIM), F32),
            pltpu.VMEM((N_GROUPS, batch, DIL_HEADS, HEAD_DIM), F32),
            pltpu.VMEM((N_GROUPS, batch, DIL_HEADS, HEAD_DIM), F32),
        ],
        compiler_params=pltpu.CompilerParams(
            dimension_semantics=("arbitrary",), vmem_limit_bytes=VMEM_LIMIT),
        name="attn_sample",
    )(*q_g, *kn_g, *vn_g, *k_caches, *v_caches, bias0, bias_steps)


def _output_kernel(ob_ref, zb_ref, gateb_ref, ha_ref, x_ref, wpb_ref, wout_ref, gpost_ref, y_ref):
    yb = (ob_ref[...] * _silu(zb_ref[...])).astype(BF16)
    p_b = jnp.dot(yb, wpb_ref[...], preferred_element_type=F32)
    hid = ha_ref[...].astype(F32) + _sigmoid(gateb_ref[...]) * p_b
    out = jnp.dot(hid.astype(BF16), wout_ref[...], preferred_element_type=F32)
    ms = jnp.mean(out * out, axis=-1, keepdims=True)
    y_ref[...] = x_ref[...] + out * lax.rsqrt(ms + RMS_EPS) * gpost_ref[...]


def _output(o_b, proj, h_a, x2d, w_proj_b, w_out, g_post, tm):
    m = x2d.shape[0]
    return pl.pallas_call(
        _output_kernel,
        grid=(m // tm,),
        in_specs=[
            pl.BlockSpec((tm, DIL_W), lambda i: (i, 0)),
            pl.BlockSpec((tm, DIL_W), lambda i: (i, COL_Z_B // DIL_W)),
            pl.BlockSpec((tm, D_MODEL), lambda i: (i, COL_GATE_B // D_MODEL)),
            pl.BlockSpec((tm, D_MODEL), lambda i: (i, 0)),
            pl.BlockSpec((tm, D_MODEL), lambda i: (i, 0)),
            pl.BlockSpec((DIL_W, D_MODEL), lambda i: (0, 0)),
            pl.BlockSpec((D_MODEL, D_MODEL), lambda i: (0, 0)),
            pl.BlockSpec((1, D_MODEL), lambda i: (0, 0)),
        ],
        out_specs=pl.BlockSpec((tm, D_MODEL), lambda i: (i, 0)),
        out_shape=jax.ShapeDtypeStruct((m, D_MODEL), F32),
        compiler_params=pltpu.CompilerParams(
            dimension_semantics=("parallel",), vmem_limit_bytes=VMEM_LIMIT),
        name="output",
    )(o_b, proj, proj, h_a, x2d, w_proj_b, w_out, g_post)


def _rel_bucket(dist):
    max_exact = REL_BUCKETS // 2
    d = jnp.maximum(dist, 1).astype(F32)
    large = max_exact + (jnp.log(d / max_exact) / math.log(REL_MAX_DIST / max_exact)
                         * (REL_BUCKETS - max_exact)).astype(jnp.int32)
    large = jnp.minimum(large, REL_BUCKETS - 1)
    return jnp.where(dist < max_exact, dist, large)


def _group_biases(rel_table):
    out = []
    for gi, (win, dil) in enumerate(DIL_GROUPS):
        dist = jnp.arange(win // dil + 1, dtype=jnp.int32) * dil
        b = rel_table[_rel_bucket(dist)][:, gi * DIL_HEADS:(gi + 1) * DIL_HEADS]
        out.append(b.T.astype(F32))
    return out


def _bias_tiles(biases):
    period = 3 * WIN_BLK
    tiles = []
    for b in biases:
        u = jnp.concatenate([b[:, ::-1], jnp.full((DIL_HEADS, period - WIN_BLK - 1), NEG, F32)], axis=1)
        flat = jnp.tile(u, (1, WIN_BLK))[:, :WIN_BLK * (period - 1)]
        tiles.append(flat.reshape(DIL_HEADS, WIN_BLK, period - 1)[:, :, :2 * WIN_BLK])
    return jnp.stack(tiles, axis=0).astype(F32)


def _layer_weights(w_in, conv_w, a_log, dt_bias, g_head_norm, w_proj_a, w_proj_b, w_out):
    w_main = jnp.concatenate([w_in[:, :AB_OFFSET], w_in[:, AB_OFFSET + 2 * GDN_HEADS:]], axis=1).astype(BF16)
    w_ab = jnp.pad(w_in[:, AB_OFFSET:AB_OFFSET + 2 * GDN_HEADS],
                   ((0, 0), (0, LANES - 2 * GDN_HEADS))).astype(BF16)
    pad = (0, LANES - GDN_HEADS)
    return dict(
        w_main=w_main, w_ab=w_ab, conv_wt=conv_w.T.astype(F32),
        alog_pad=jnp.pad(a_log.astype(F32), pad)[None, :],
        dtb_pad=jnp.pad(dt_bias.astype(F32), pad)[None, :],
        ghn=g_head_norm.astype(F32)[None, :],
        w_proj_a=w_proj_a.astype(BF16), w_proj_b=w_proj_b.astype(BF16), w_out=w_out.astype(BF16))


def _window_rows_kernel(k_ref, v_ref, ko_ref, vo_ref):
    rows = k_ref.shape[0]
    for src, dst in ((k_ref, ko_ref), (v_ref, vo_ref)):
        for h in range(DIL_HEADS):
            dst[pl.ds(h, rows, stride=DIL_HEADS), :] = src[:, h * HEAD_DIM:(h + 1) * HEAD_DIM]


def _window_rows(proj3, col_k, keep):
    batch, seq, _ = proj3.shape
    rows = min(keep, 1024)
    first = (seq - keep) // rows
    out = jax.ShapeDtypeStruct((batch, keep * DIL_HEADS, HEAD_DIM), F32)
    k_out, v_out = pl.pallas_call(
        _window_rows_kernel,
        grid=(batch, keep // rows),
        in_specs=[
            pl.BlockSpec((None, rows, DIL_W), lambda b, r: (b, first + r, col_k // DIL_W)),
            pl.BlockSpec((None, rows, DIL_W), lambda b, r: (b, first + r, col_k // DIL_W + 1)),
        ],
        out_specs=[
            pl.BlockSpec((None, rows * DIL_HEADS, HEAD_DIM), lambda b, r: (b, r, 0)),
            pl.BlockSpec((None, rows * DIL_HEADS, HEAD_DIM), lambda b, r: (b, r, 0)),
        ],
        out_shape=[out, out],
        compiler_params=pltpu.CompilerParams(
            dimension_semantics=("parallel", "parallel"), vmem_limit_bytes=VMEM_LIMIT),
        name="window_rows",
    )(proj3, proj3)
    shape = (batch, keep, DIL_HEADS, HEAD_DIM)
    return k_out.reshape(shape), v_out.reshape(shape)


def kernel(x_prompt, x_sample, state_gdn, state_conv, cache_k_w128, cache_v_w128, cache_k_w512,
           cache_v_w512, cache_k_w2048, cache_v_w2048, rel_table, g_pre, w_in, conv_w, a_log, dt_bias,
           g_head_norm, w_proj_a, w_proj_b, w_out, g_post):
    batch, seq, _ = x_prompt.shape
    dec_batch = x_sample.shape[0]
    k_caches = (cache_k_w128, cache_k_w512, cache_k_w2048)
    v_caches = (cache_v_w128, cache_v_w512, cache_v_w2048)
    biases = _group_biases(rel_table)
    bias_tiles = _bias_tiles(biases) * LOG2_E
    lw = _layer_weights(w_in[0], conv_w[0], a_log[0], dt_bias[0], g_head_norm[0], w_proj_a[0],
                        w_proj_b[0], w_out[0])
    g_pre_l = g_pre[0][None, :]
    g_post_l = g_post[0][None, :]

    xp = x_prompt.reshape(batch * seq, D_MODEL)
    proj, ab = _in_projection(xp, g_pre_l, lw["w_main"], lw["w_ab"], tm=2048, tn=1024)
    h_a, p_gdn = _gdn_prompt(proj, ab, lw["conv_wt"], lw["alog_pad"], lw["dtb_pad"], lw["ghn"],
                             lw["w_proj_a"], batch, seq)
    o_b = _attn_prompt(proj, bias_tiles, batch, seq)
    y_prompt = _output(o_b, proj, h_a, xp, lw["w_proj_b"], lw["w_out"], g_post_l, tm=1024)
    y_prompt = y_prompt.reshape(batch, seq, D_MODEL)
    p_conv = proj.reshape(batch, seq, MAIN_WIDTH)[:, seq - (CONV_WIDTH - 1):, :CONV_DIM]
    p_rows = []
    for gi, (win, _) in enumerate(DIL_GROUPS):
        keep = min(win, seq)
        ck = COL_QKV_B + gi * 3 * DIL_W + DIL_W
        p_rows += list(_window_rows(proj.reshape(batch, seq, MAIN_WIDTH), ck, keep))

    xs = x_sample.reshape(dec_batch, D_MODEL)
    proj_s, ab_s = _in_projection(xs, g_pre_l, lw["w_main"], lw["w_ab"], tm=dec_batch, tn=1024)
    sc_t = jnp.transpose(state_conv[0], (1, 0, 2))
    o_s, s_gdn = _gdn_sample(sc_t, proj_s, ab_s, lw["conv_wt"], lw["alog_pad"], lw["dtb_pad"],
                             state_gdn[0], nb=32)
    h_a_s = _gdn_sample_out(o_s, proj_s, lw["ghn"], lw["w_proj_a"])
    q_g, kn_g, vn_g = [], [], []
    for gi in range(N_GROUPS):
        cq = COL_QKV_B + gi * 3 * DIL_W
        q_g.append(proj_s[:, cq:cq + DIL_W].reshape(dec_batch, DIL_HEADS, HEAD_DIM))
        kn_g.append(proj_s[:, cq + DIL_W:cq + 2 * DIL_W].reshape(dec_batch, DIL_HEADS, HEAD_DIM))
        vn_g.append(proj_s[:, cq + 2 * DIL_W:cq + 3 * DIL_W].reshape(dec_batch, DIL_HEADS, HEAD_DIM))
    bias_all = jnp.stack(biases, axis=0)
    bias0 = bias_all[:, None, :, 0:1]
    bias_steps = jnp.transpose(bias_all[:, :, :0:-1], (2, 0, 1))[:, :, None, :, None]
    o_b_s = _attn_sample(q_g, kn_g, vn_g, k_caches, v_caches, bias0, bias_steps)
    y_sample = _output(o_b_s.reshape(dec_batch, DIL_W), proj_s, h_a_s, xs, lw["w_proj_b"], lw["w_out"],
                       g_post_l, tm=dec_batch)
    y_sample = y_sample.reshape(dec_batch, 1, D_MODEL)
    s_conv = jnp.concatenate([state_conv[0][:, 1:, :], proj_s[:, None, :CONV_DIM]], axis=1)
    s_rows = []
    for gi in range(N_GROUPS):
        s_rows += [kn_g[gi][:, None], vn_g[gi][:, None]]

    return (y_prompt, y_sample, p_gdn[None], p_conv[None], *[r[None] for r in p_rows],
            s_gdn[None], s_conv[None], *[r[None] for r in s_rows])
```

```python
import functools
import math

import jax
import jax.numpy as jnp
from jax import lax
from jax.experimental import pallas as pl
from jax.experimental.pallas import tpu as pltpu

F32 = jnp.float32
BF16 = jnp.bfloat16

D_MODEL = 1024
GDN_HEADS = 8
HEAD_DIM = 128
GDN_QK = GDN_HEADS * HEAD_DIM
CONV_DIM = 3 * GDN_QK
CONV_WIDTH = 4
DIL_GROUPS = ((128, 1), (512, 4), (2048, 16))
N_GROUPS = 3
DIL_HEADS = 4
DIL_W = DIL_HEADS * HEAD_DIM
WIN_BLK = 128
REL_BUCKETS = 32
REL_MAX_DIST = 2048
RMS_EPS = 1e-6
NEG = -1e30
LOG2_E = math.log2(math.e)

COL_QKV_A = 0
COL_QKV_B = 3072
COL_Z_B = 7680
WIDE_WIDTH = 8192
COL_Z_A = 0
COL_GATE_A = 1024
COL_GATE_B = 2048
GATE_WIDTH = 3072
MAIN_WIDTH = WIDE_WIDTH + GATE_WIDTH
SRC_QKV_A, SRC_Z_A, SRC_AB, SRC_QKV_B, SRC_Z_B, SRC_GATE_A, SRC_GATE_B, SRC_END = (
    0, 3072, 4096, 4112, 8720, 9232, 10256, 11280)
LANES = 128

VMEM_LIMIT = 56 * 1024 * 1024
INPROJ_VMEM_LIMIT = 60 * 1024 * 1024

GDN_CHUNK = 64
GDN_SEQS = 4
ATTN_TILE = 2048
ATTN_UNITS = 16
MAX_ROW_STRIDE = 4


def _silu(x):
    half = 0.5 * x
    return half * jnp.tanh(half) + half


def _sigmoid(x):
    return 1.0 / (1.0 + jnp.exp(-x))


def _softplus(x):
    return jnp.maximum(x, 0.0) + jnp.log(1.0 + jnp.exp(-jnp.abs(x)))


def _dot(a, b):
    return jnp.dot(a.astype(BF16), b.astype(BF16), preferred_element_type=F32)


def _dot_nt(a, b):
    return lax.dot_general(a.astype(BF16), b.astype(BF16), (((1,), (1,)), ((), ())),
                           preferred_element_type=F32)


def _dot_tn(a, b):
    return lax.dot_general(a.astype(BF16), b.astype(BF16), (((0,), (0,)), ((), ())),
                           preferred_element_type=F32)


def _inproj_kernel(x_ref, g_ref, w_ref, wab_ref, wide_ref, gate_ref, ab_ref, xn_ref, *, wide_tiles):
    j = pl.program_id(1)

    @pl.when(j == 0)
    def _():
        x = x_ref[...]
        ms = jnp.mean(x * x, axis=-1, keepdims=True)
        xn = (x * lax.rsqrt(ms + RMS_EPS) * g_ref[...]).astype(BF16)
        xn_ref[...] = xn
        ab_ref[...] = jnp.dot(xn, wab_ref[...], preferred_element_type=F32)

    @pl.when(j < wide_tiles)
    def _():
        wide_ref[...] = jnp.dot(xn_ref[...], w_ref[...], preferred_element_type=F32)

    @pl.when(j >= wide_tiles)
    def _():
        gate_ref[...] = jnp.dot(xn_ref[...], w_ref[...], preferred_element_type=F32).astype(gate_ref.dtype)


def _in_projection(x2d, g_pre, w_main, w_ab, tm, tn):
    m = x2d.shape[0]
    wide_tiles = WIDE_WIDTH // tn
    return pl.pallas_call(
        functools.partial(_inproj_kernel, wide_tiles=wide_tiles),
        grid=(m // tm, MAIN_WIDTH // tn),
        in_specs=[
            pl.BlockSpec((tm, D_MODEL), lambda i, j: (i, 0)),
            pl.BlockSpec((1, D_MODEL), lambda i, j: (0, 0)),
            pl.BlockSpec((D_MODEL, tn), lambda i, j: (0, j)),
            pl.BlockSpec((D_MODEL, LANES), lambda i, j: (0, 0)),
        ],
        out_specs=[
            pl.BlockSpec((tm, tn), lambda i, j: (i, jnp.minimum(j, wide_tiles - 1))),
            pl.BlockSpec((tm, tn), lambda i, j: (i, jnp.maximum(j - wide_tiles, 0))),
            pl.BlockSpec((tm, LANES), lambda i, j: (i, 0)),
        ],
        out_shape=[
            jax.ShapeDtypeStruct((m, WIDE_WIDTH), F32),
            jax.ShapeDtypeStruct((m, GATE_WIDTH), BF16),
            jax.ShapeDtypeStruct((m, LANES), F32),
        ],
        scratch_shapes=[pltpu.VMEM((tm, D_MODEL), BF16)],
        compiler_params=pltpu.CompilerParams(
            dimension_semantics=("parallel", "arbitrary"), vmem_limit_bytes=INPROJ_VMEM_LIMIT),
        name="in_projection",
    )(x2d, g_pre, w_main, w_ab)


GROUP_HEADS = 4


def _block_diag_rhs(x4, half_masks):
    c = x4.shape[0]
    zeros = jnp.zeros((c, LANES), x4.dtype)
    row_blocks = []
    for h in range(GROUP_HEADS):
        piece = x4[:, (h // 2) * LANES:(h // 2 + 1) * LANES] * half_masks[h % 2]
        row_blocks.append(jnp.concatenate([piece if v == h // 2 else zeros for v in range(2)], axis=1))
    return jnp.concatenate(row_blocks, axis=0)


def _gdn_prompt_kernel(qkv_ref, z_ref, ab_ref, gate_ref, convw_ref, alog_ref, dtb_ref, ghn_ref,
                       wproj_ref, ha_ref, sfin_ref, xpad_ref, qkvn_ref, o_ref, s_ref,
                       u_ref, w_ref, qd_ref, kd_ref, qk_ref):
    c = GDN_CHUNK
    tl = GDN_SEQS * c
    gw = GROUP_HEADS * c
    step = pl.program_id(1)

    def conv_stage(i):
        for blk in range(CONV_DIM // LANES):
            cols = slice(blk * LANES, (blk + 1) * LANES)
            xpad_ref[i, blk, 8:8 + c, :] = qkv_ref[i, :, cols]
            for phase in range(2):
                acc = None
                for w in range(CONV_WIDTH):
                    tap = xpad_ref[i, blk, pl.ds(5 + phase + w, c // 2, stride=2), :] * convw_ref[w:w + 1, cols]
                    acc = tap if acc is None else acc + tap
                y = _silu(acc)
                if blk < 2 * GDN_HEADS:
                    y = y * lax.rsqrt(jnp.sum(y * y, axis=-1, keepdims=True) + RMS_EPS)
                    if blk < GDN_HEADS:
                        y = y * (HEAD_DIM ** -0.5)
                qkvn_ref[blk, pl.ds(i * c + phase, c // 2, stride=2), :] = y
            xpad_ref[i, blk, 5:8, :] = xpad_ref[i, blk, c + 5:c + 8, :]

    @pl.when(step == 0)
    def _():
        xpad_ref[:, :, 0:8, :] = jnp.zeros((GDN_SEQS, CONV_DIM // LANES, 8, LANES), F32)
        s_ref[...] = jnp.zeros_like(s_ref)

    ab = ab_ref[...].reshape(tl, LANES)
    g_full = -jnp.exp(alog_ref[...]) * _softplus(ab + dtb_ref[...])
    beta_full = _sigmoid(ab)
    row = lax.broadcasted_iota(jnp.int32, (tl, tl), 0)
    col = lax.broadcasted_iota(jnp.int32, (tl, tl), 1)
    same_chunk = (row // c) == (col // c)
    tri = jnp.where(same_chunk & (col <= row), 1.0, 0.0).astype(BF16)
    g_hi = g_full.astype(BF16)
    g_rest = g_full - g_hi.astype(F32)
    g_mid = g_rest.astype(BF16)
    g_lo = (g_rest - g_mid.astype(F32)).astype(BF16)
    g_cum = jnp.dot(jnp.concatenate([tri, tri, tri], axis=1), jnp.concatenate([g_hi, g_mid, g_lo], axis=0),
                    preferred_element_type=F32)
    g_cum_t = g_cum.T
    g_last = jnp.concatenate(
        [jnp.broadcast_to(g_cum[i * c + c - 1:(i + 1) * c, :], (c, LANES)) for i in range(GDN_SEQS)], axis=0)
    exp_g = jnp.exp(g_cum)
    exp_rest = jnp.exp(g_last - g_cum)
    exp_last = jnp.exp(g_last)

    ci = lax.broadcasted_iota(jnp.int32, (c, gw), 0)
    cj = lax.broadcasted_iota(jnp.int32, (c, gw), 1) % c
    causal4 = cj <= ci
    strict4 = cj < ci
    half_lane = lax.broadcasted_iota(jnp.int32, (c, LANES), 1) // c
    half_masks = [jnp.where(half_lane == k, 1.0, 0.0).astype(BF16) for k in range(2)]

    def lanes_of(col_values, width):
        return jnp.concatenate([jnp.broadcast_to(v, (c, width)) for v in col_values], axis=1)

    n_groups = GDN_HEADS // GROUP_HEADS
    zero_s = jnp.zeros((HEAD_DIM, HEAD_DIM), BF16)
    zero_v = jnp.zeros((c, HEAD_DIM), BF16)

    def group_cols(hg):
        return slice(hg * GROUP_HEADS * LANES, (hg + 1) * GROUP_HEADS * LANES)

    def bd_dot(lhs, x4):
        return jnp.dot(lhs.astype(BF16), _block_diag_rhs(x4.astype(BF16), half_masks),
                       preferred_element_type=F32)

    def seq_stages(i):
        rows = slice(i * c, (i + 1) * c)
        st = {}

        def build():
            a4s = []
            for hg in range(n_groups):
                heads = range(hg * GROUP_HEADS, (hg + 1) * GROUP_HEADS)
                cols = group_cols(hg)

                def head_blocks(first_blk):
                    return jnp.concatenate([qkvn_ref[first_blk + h, rows, :] for h in heads], axis=1)

                q4 = head_blocks(0)
                k4 = head_blocks(GDN_HEADS)
                v4 = head_blocks(2 * GDN_HEADS)
                beta_e = lanes_of([beta_full[rows, GDN_HEADS + h:GDN_HEADS + h + 1] for h in heads], LANES)
                eg_e = lanes_of([exp_g[rows, h:h + 1] for h in heads], LANES)
                er_e = lanes_of([exp_rest[rows, h:h + 1] for h in heads], LANES)
                gi4 = lanes_of([g_cum[rows, h:h + 1] for h in heads], c)
                gj4 = jnp.concatenate([g_cum_t[h:h + 1, rows] for h in heads], axis=1)
                decay4 = jnp.exp(jnp.where(causal4, gi4 - gj4, NEG))
                kb4 = k4 * beta_e
                k4b = k4.astype(BF16)
                k_rhs = jnp.concatenate(
                    [jnp.concatenate([k4b[:, g * LANES:(g + 1) * LANES] if g == h else zero_v
                                      for g in range(GROUP_HEADS)], axis=1)
                     for h in range(GROUP_HEADS)], axis=0)
                kkqk = lax.dot_general(jnp.concatenate([kb4, q4], axis=0).astype(BF16), k_rhs,
                                       (((1,), (1,)), ((), ())), preferred_element_type=F32)
                a4s.append(jnp.where(strict4, kkqk[:c] * decay4, 0.0))
                qk_ref[rows, hg * gw:(hg + 1) * gw] = kkqk[c:] * decay4
                u_ref[rows, cols] = v4 * beta_e
                w_ref[rows, cols] = kb4 * eg_e
                qd_ref[rows, cols] = q4 * eg_e
                kd_ref[rows, cols] = k4 * er_e
            st["a"] = a4s

        def square():
            st["e"] = [-a4 for a4 in st["a"]]
            st["x"] = [bd_dot(a4, a4) for a4 in st["a"]]

        def double():
            boths = [bd_dot(jnp.concatenate([e, x], axis=0), x) for e, x in zip(st["e"], st["x"])]
            st["e"] = [e + x + both[:c] for e, x, both in zip(st["e"], st["x"], boths)]
            st["x"] = [both[c:] for both in boths]

        def last_factor():
            st["e"] = [e + x + bd_dot(e, x) for e, x in zip(st["e"], st["x"])]

        def solve():
            for hg, e4 in enumerate(st["e"]):
                cols = group_cols(hg)
                vb4 = u_ref[rows, cols]
                kbe4 = w_ref[rows, cols]
                us, ws = [], []
                for k in range(GROUP_HEADS):
                    hc = slice(k * LANES, (k + 1) * LANES)
                    rhs = jnp.concatenate([vb4[:, hc], kbe4[:, hc]], axis=1)
                    uw = rhs + _dot(e4[:, k * c:(k + 1) * c], rhs)
                    us.append(uw[:, :LANES])
                    ws.append(uw[:, LANES:])
                u_ref[rows, cols] = jnp.concatenate(us, axis=1)
                w_ref[rows, cols] = jnp.concatenate(ws, axis=1)

        def pair_cols(p):
            return slice(2 * p * LANES, (2 * p + 2) * LANES)

        def state_products():
            wsqs = []
            for p in range(GDN_HEADS // 2):
                cols = pair_cols(p)
                s_bd = jnp.concatenate(
                    [jnp.concatenate([s_ref[i, 2 * p].astype(BF16), zero_s], axis=1),
                     jnp.concatenate([zero_s, s_ref[i, 2 * p + 1].astype(BF16)], axis=1)], axis=0)
                wq = jnp.concatenate([w_ref[rows, cols], qd_ref[rows, cols]], axis=0).astype(BF16)
                wsqs.append(jnp.dot(wq, s_bd, preferred_element_type=F32))
            st["wsq"] = wsqs

        def outputs():
            vnbs = []
            for p, wsq in enumerate(st["wsq"]):
                cols = pair_cols(p)
                vnb = (u_ref[rows, cols] - wsq[:c]).astype(BF16)
                vnbs.append(vnb)
                vn_bd = jnp.concatenate([jnp.concatenate([vnb[:, :LANES], zero_v], axis=1),
                                         jnp.concatenate([zero_v, vnb[:, LANES:]], axis=1)], axis=0)
                qk2 = qk_ref[rows, 2 * p * c:(2 * p + 2) * c]
                o_ref[rows, cols] = wsq[c:] + jnp.dot(qk2.astype(BF16), vn_bd, preferred_element_type=F32)
            st["vn"] = vnbs

        def state_update():
            for p, vnb in enumerate(st["vn"]):
                kd2 = kd_ref[rows, pair_cols(p)]
                for k in range(2):
                    h = 2 * p + k
                    hc = slice(k * LANES, (k + 1) * LANES)
                    el = exp_last[i * c:i * c + 1, h:h + 1]
                    s_ref[i, h] = s_ref[i, h] * el + _dot_tn(kd2[:, hc], vnb[:, hc])

        doublings = int(math.log2(c)) - 2
        return ([functools.partial(conv_stage, i), build, square] + [double] * doublings
                + [last_factor, solve, state_products, outputs, state_update])

    for stages in zip(*[seq_stages(i) for i in range(GDN_SEQS)]):
        for stage in stages:
            stage()

    for h in range(GDN_HEADS):
        cols = slice(h * LANES, (h + 1) * LANES)
        o = o_ref[:, cols]
        o = o * lax.rsqrt(jnp.mean(o * o, axis=-1, keepdims=True) + RMS_EPS) * ghn_ref[...]
        o_ref[:, cols] = o * _silu(z_ref[:, :, cols].astype(F32).reshape(tl, LANES))
    p_a = jnp.dot(o_ref[...].astype(BF16), wproj_ref[...], preferred_element_type=F32)
    h_a = _sigmoid(gate_ref[...].astype(F32).reshape(tl, D_MODEL)) * p_a
    ha_ref[...] = h_a.reshape(GDN_SEQS, c, D_MODEL).astype(ha_ref.dtype)

    @pl.when(step == pl.num_programs(1) - 1)
    def _():
        sfin_ref[...] = s_ref[...]


def _gdn_prompt(proj, gates, ab, conv_wt, alog_pad, dtb_pad, ghn, w_proj_a, batch, seq):
    c = GDN_CHUNK
    ns = GDN_SEQS
    tl = ns * c
    proj3 = proj.reshape(batch, seq, WIDE_WIDTH)
    gates3 = gates.reshape(batch, seq, GATE_WIDTH)
    ab3 = ab.reshape(batch, seq, LANES)
    h_a, s_fin = pl.pallas_call(
        _gdn_prompt_kernel,
        grid=(batch // ns, seq // c),
        in_specs=[
            pl.BlockSpec((ns, c, CONV_DIM), lambda b, l: (b, l, COL_QKV_A // CONV_DIM)),
            pl.BlockSpec((ns, c, GDN_QK), lambda b, l: (b, l, COL_Z_A // GDN_QK)),
            pl.BlockSpec((ns, c, LANES), lambda b, l: (b, l, 0)),
            pl.BlockSpec((ns, c, D_MODEL), lambda b, l: (b, l, COL_GATE_A // D_MODEL)),
            pl.BlockSpec((CONV_WIDTH, CONV_DIM), lambda b, l: (0, 0)),
            pl.BlockSpec((1, LANES), lambda b, l: (0, 0)),
            pl.BlockSpec((1, LANES), lambda b, l: (0, 0)),
            pl.BlockSpec((1, LANES), lambda b, l: (0, 0)),
            pl.BlockSpec((GDN_QK, D_MODEL), lambda b, l: (0, 0)),
        ],
        out_specs=[
            pl.BlockSpec((ns, c, D_MODEL), lambda b, l: (b, l, 0)),
            pl.BlockSpec((ns, GDN_HEADS, HEAD_DIM, HEAD_DIM), lambda b, l: (b, 0, 0, 0)),
        ],
        out_shape=[
            jax.ShapeDtypeStruct((batch, seq, D_MODEL), BF16),
            jax.ShapeDtypeStruct((batch, GDN_HEADS, HEAD_DIM, HEAD_DIM), F32),
        ],
        scratch_shapes=[
            pltpu.VMEM((ns, CONV_DIM // LANES, c + 8, LANES), F32),
            pltpu.VMEM((CONV_DIM // LANES, tl, LANES), F32),
            pltpu.VMEM((tl, GDN_QK), F32),
            pltpu.VMEM((ns, GDN_HEADS, HEAD_DIM, HEAD_DIM), F32),
            pltpu.VMEM((tl, GDN_QK), F32),
            pltpu.VMEM((tl, GDN_QK), F32),
            pltpu.VMEM((tl, GDN_QK), F32),
            pltpu.VMEM((tl, GDN_QK), F32),
            pltpu.VMEM((tl, GDN_HEADS * GDN_CHUNK), F32),
        ],
        compiler_params=pltpu.CompilerParams(
            dimension_semantics=("parallel", "arbitrary"), vmem_limit_bytes=VMEM_LIMIT),
        name="gdn_prompt",
    )(proj3, gates3, ab3, gates3, conv_wt, alog_pad, dtb_pad, ghn, w_proj_a)
    return h_a.reshape(batch * seq, D_MODEL), s_fin


def _gdn_sample_kernel(sc_q_ref, sc_k_ref, sc_v_ref, nq_ref, nk_ref, nv_ref, cwq_ref, cwk_ref, cwv_ref,
                       ab_ref, alog_ref, dtb_ref, s_ref, o_ref, snew_ref):
    h = pl.program_id(0)
    nb = nq_ref.shape[0]

    def conv_silu(sc_ref, new_ref, cw_ref):
        acc = new_ref[...] * cw_ref[3:4, :]
        for w in range(CONV_WIDTH - 1):
            acc = acc + sc_ref[w] * cw_ref[w:w + 1, :]
        return _silu(acc)

    q = conv_silu(sc_q_ref, nq_ref, cwq_ref)
    k = conv_silu(sc_k_ref, nk_ref, cwk_ref)
    v = conv_silu(sc_v_ref, nv_ref, cwv_ref)
    q = q * lax.rsqrt(jnp.sum(q * q, axis=-1, keepdims=True) + RMS_EPS) * (HEAD_DIM ** -0.5)
    k = k * lax.rsqrt(jnp.sum(k * k, axis=-1, keepdims=True) + RMS_EPS)

    ab = ab_ref[...]
    lane = lax.broadcasted_iota(jnp.int32, ab.shape, 1)
    g_full = -jnp.exp(alog_ref[...]) * _softplus(ab + dtb_ref[...])
    g = jnp.sum(jnp.where(lane == h, g_full, 0.0), axis=-1, keepdims=True)
    beta = jnp.sum(jnp.where(lane == h + GDN_HEADS, _sigmoid(ab), 0.0), axis=-1, keepdims=True)
    eg = jnp.exp(g)

    eye = (lax.broadcasted_iota(jnp.int32, (HEAD_DIM, HEAD_DIM), 0)
           == lax.broadcasted_iota(jnp.int32, (HEAD_DIM, HEAD_DIM), 1))

    def as_column(row):
        return jnp.sum(jnp.where(eye, row, 0.0), axis=1, keepdims=True)

    for b in range(nb):
        s = s_ref[b]
        k_col = as_column(k[b:b + 1, :])
        q_col = as_column(q[b:b + 1, :])
        e = eg[b:b + 1, :]
        ks = jnp.sum(k_col * s, axis=0, keepdims=True)
        v_new = beta[b:b + 1, :] * (v[b:b + 1, :] - e * ks)
        s_new = s * e + k_col * v_new
        snew_ref[b] = s_new
        o_ref[b:b + 1, :] = jnp.sum(q_col * s_new, axis=0, keepdims=True)


def _gdn_sample(state_conv_t, proj, ab, conv_wt, alog_pad, dtb_pad, state, nb):
    batch = proj.shape[0]
    hq = GDN_QK // LANES
    return pl.pallas_call(
        _gdn_sample_kernel,
        grid=(GDN_HEADS, batch // nb),
        in_specs=[
            pl.BlockSpec((CONV_WIDTH - 1, nb, LANES), lambda h, i: (0, i, h)),
            pl.BlockSpec((CONV_WIDTH - 1, nb, LANES), lambda h, i: (0, i, hq + h)),
            pl.BlockSpec((CONV_WIDTH - 1, nb, LANES), lambda h, i: (0, i, 2 * hq + h)),
            pl.BlockSpec((nb, LANES), lambda h, i: (i, h)),
            pl.BlockSpec((nb, LANES), lambda h, i: (i, hq + h)),
            pl.BlockSpec((nb, LANES), lambda h, i: (i, 2 * hq + h)),
            pl.BlockSpec((CONV_WIDTH, LANES), lambda h, i: (0, h)),
            pl.BlockSpec((CONV_WIDTH, LANES), lambda h, i: (0, hq + h)),
            pl.BlockSpec((CONV_WIDTH, LANES), lambda h, i: (0, 2 * hq + h)),
            pl.BlockSpec((nb, LANES), lambda h, i: (i, 0)),
            pl.BlockSpec((1, LANES), lambda h, i: (0, 0)),
            pl.BlockSpec((1, LANES), lambda h, i: (0, 0)),
            pl.BlockSpec((nb, None, HEAD_DIM, HEAD_DIM), lambda h, i: (i, h, 0, 0)),
        ],
        out_specs=[
            pl.BlockSpec((nb, LANES), lambda h, i: (i, h)),
            pl.BlockSpec((nb, None, HEAD_DIM, HEAD_DIM), lambda h, i: (i, h, 0, 0)),
        ],
        out_shape=[
            jax.ShapeDtypeStruct((batch, GDN_QK), F32),
            jax.ShapeDtypeStruct(state.shape, F32),
        ],
        compiler_params=pltpu.CompilerParams(
            dimension_semantics=("parallel", "parallel"), vmem_limit_bytes=VMEM_LIMIT),
        name="gdn_sample",
    )(state_conv_t, state_conv_t, state_conv_t, proj, proj, proj, conv_wt, conv_wt, conv_wt,
      ab, alog_pad, dtb_pad, state)


def _gdn_sample_out_kernel(o_ref, z_ref, gate_ref, ghn_ref, wproj_ref, ha_ref, y_ref):
    for h in range(GDN_HEADS):
        cols = slice(h * LANES, (h + 1) * LANES)
        o = o_ref[:, cols]
        o = o * lax.rsqrt(jnp.mean(o * o, axis=-1, keepdims=True) + RMS_EPS) * ghn_ref[...]
        y_ref[:, cols] = o * _silu(z_ref[:, cols].astype(F32))
    p_a = jnp.dot(y_ref[...].astype(BF16), wproj_ref[...], preferred_element_type=F32)
    ha_ref[...] = (_sigmoid(gate_ref[...].astype(F32)) * p_a).astype(ha_ref.dtype)


def _gdn_sample_out(o, gates, ghn, w_proj_a):
    batch = o.shape[0]
    return pl.pallas_call(
        _gdn_sample_out_kernel,
        grid=(1,),
        in_specs=[
            pl.BlockSpec((batch, GDN_QK), lambda i: (0, 0)),
            pl.BlockSpec((batch, GDN_QK), lambda i: (0, COL_Z_A // GDN_QK)),
            pl.BlockSpec((batch, D_MODEL), lambda i: (0, COL_GATE_A // D_MODEL)),
            pl.BlockSpec((1, LANES), lambda i: (0, 0)),
            pl.BlockSpec((GDN_QK, D_MODEL), lambda i: (0, 0)),
        ],
        out_specs=pl.BlockSpec((batch, D_MODEL), lambda i: (0, 0)),
        out_shape=jax.ShapeDtypeStruct((batch, D_MODEL), BF16),
        scratch_shapes=[pltpu.VMEM((batch, GDN_QK), F32)],
        compiler_params=pltpu.CompilerParams(vmem_limit_bytes=VMEM_LIMIT),
        name="gdn_sample_out",
    )(o, gates, gates, ghn, w_proj_a)


def _attn_prompt_kernel(*refs):
    ins = refs[:5 * N_GROUPS]
    bias_ref = refs[5 * N_GROUPS]
    o_ref = refs[5 * N_GROUPS + 1]
    acc_ref, m_ref, l_ref, bias_start_ref, split_ref = refs[5 * N_GROUPS + 2:]
    tq = ATTN_TILE
    first_tile = pl.program_id(1) == 0
    scale = (HEAD_DIM ** -0.5) * LOG2_E
    lane2 = lax.broadcasted_iota(jnp.int32, (1, 2 * WIN_BLK), 1)
    start_mask = jnp.where(first_tile & (lane2 < WIN_BLK), NEG, 0.0).astype(F32)
    for gi in range(N_GROUPS):
        bias_start_ref[gi] = bias_ref[gi] + start_mask

    def strided(start, n, stride):
        return pl.ds(start, n, stride=stride) if stride > 1 else pl.ds(start, n)

    for gi, (_, dil) in enumerate(DIL_GROUPS):
        srcs = ins[5 * gi:5 * gi + 5]
        n_blocks = tq // (dil * WIN_BLK)
        inner = min(dil, MAX_ROW_STRIDE)
        outer = dil // inner
        if outer > 1:
            for a, src in enumerate(srcs):
                n = src.shape[0]
                for p in range(outer):
                    split_ref[a, p * (n // outer):(p + 1) * (n // outer), :] = src[strided(p, n // outer, outer), :]
            srcs = [split_ref.at[a] for a in range(5)]
        q_ref, kp_ref, kc_ref, vp_ref, vc_ref = srcs

        def rows_of(ref_rows, r, first, count, inner=inner, outer=outer):
            if outer == 1:
                return strided(r + inner * first, count, inner)
            part = r % outer
            return strided(part * (ref_rows // outer) + r // outer + inner * first, count, inner)

        def run_units(units, q_ref=q_ref, kp_ref=kp_ref, kc_ref=kc_ref, vp_ref=vp_ref, vc_ref=vc_ref,
                      gi=gi, rows_of=rows_of):
            span = kp_ref.shape[0]
            n = len(units)
            st = [{} for _ in units]

            def logits(u):
                r, j = units[u]
                q = (q_ref[rows_of(tq, r, WIN_BLK * j, WIN_BLK), :] * scale).astype(BF16)
                if j == 0:
                    prev = rows_of(span, r, 0, WIN_BLK)
                    cur = rows_of(tq, r, 0, WIN_BLK)
                    kk = jnp.concatenate([kp_ref[prev, :], kc_ref[cur, :]], axis=0).astype(BF16)
                    st[u]["v"] = (vp_ref, prev, vc_ref, cur)
                else:
                    rows = rows_of(tq, r, WIN_BLK * (j - 1), 2 * WIN_BLK)
                    kk = kc_ref[rows, :].astype(BF16)
                    st[u]["v"] = (vc_ref, rows)
                st[u]["s"] = lax.dot_general(q, kk, (((1,), (1,)), ((), ())), preferred_element_type=F32)

            def softmax(u):
                r, j = units[u]
                s = st[u].pop("s") + (bias_start_ref[gi] if j == 0 else bias_ref[gi])
                m = jnp.max(s, axis=-1, keepdims=True)
                p = jnp.exp2(s - m)
                st[u]["m"] = m
                st[u]["l"] = jnp.sum(p, axis=-1, keepdims=True)
                st[u]["p"] = p.astype(BF16)

            def values(u):
                r, j = units[u]
                v = st[u].pop("v")
                if len(v) == 4:
                    vv = jnp.concatenate([v[0][v[1], :], v[2][v[3], :]], axis=0).astype(BF16)
                else:
                    vv = v[0][v[1], :].astype(BF16)
                acc = jnp.dot(st[u].pop("p"), vv, preferred_element_type=F32)
                q_rows = rows_of(tq, r, WIN_BLK * j, WIN_BLK)
                acc_ref[gi, q_rows, :] = acc
                m_ref[gi, q_rows, :] = jnp.broadcast_to(st[u].pop("m"), (WIN_BLK, LANES))
                l_ref[gi, q_rows, :] = jnp.broadcast_to(st[u].pop("l"), (WIN_BLK, LANES))

            for stage in (logits, softmax, values):
                for u in range(n):
                    stage(u)

        if dil == 1:
            for j0 in range(0, n_blocks, ATTN_UNITS):
                run_units([(0, j) for j in range(j0, min(j0 + ATTN_UNITS, n_blocks))])
        else:
            jb = min(n_blocks, ATTN_UNITS)
            rb = max(ATTN_UNITS // n_blocks, 1)
            for j0 in range(0, n_blocks, jb):
                def body(it, carry, j0=j0, jb=jb, rb=rb, run_units=run_units):
                    run_units([(it * rb + k, j) for k in range(rb) for j in range(j0, j0 + jb)])
                    return carry
                if dil == rb:
                    body(0, 0)
                else:
                    lax.fori_loop(0, dil // rb, body, 0)

    outers = [dil // min(dil, MAX_ROW_STRIDE) for _, dil in DIL_GROUPS]
    parts = max(outers)
    cnt = 2 * WIN_BLK
    for p in range(parts):
        for i0 in range(0, tq // parts, cnt):
            def rows_in(gi):
                if outers[gi] == parts:
                    return pl.ds(p * (tq // parts) + i0, cnt)
                assert outers[gi] == 1
                return strided(p + parts * i0, cnt, parts)
            m_all = jnp.maximum(jnp.maximum(m_ref[0, rows_in(0), :], m_ref[1, rows_in(1), :]),
                                m_ref[2, rows_in(2), :])
            num = jnp.zeros((cnt, LANES), F32)
            den = jnp.zeros((cnt, LANES), F32)
            for gi in range(N_GROUPS):
                wgt = jnp.exp2(m_ref[gi, rows_in(gi), :] - m_all)
                num = num + wgt * acc_ref[gi, rows_in(gi), :]
                den = den + wgt * l_ref[gi, rows_in(gi), :]
            o_ref[strided(p + parts * i0, cnt, parts), :] = num / den


def _attn_prompt(proj, bias_tiles, batch, seq):
    tq = ATTN_TILE
    nt = seq // tq
    in_specs = []
    args = []
    for gi, (_, dil) in enumerate(DIL_GROUPS):
        span = dil * WIN_BLK
        cq = (COL_QKV_B + gi * 3 * DIL_W) // LANES
        ck = cq + DIL_HEADS
        cv = ck + DIL_HEADS

        def cur(col):
            return pl.BlockSpec((tq, LANES), lambda b, i, h, col=col: (b * nt + i, col + h))

        def prev(col, span=span):
            per = tq // span
            return pl.BlockSpec(
                (span, LANES),
                lambda b, i, h, col=col, per=per: (jnp.maximum((b * nt + i) * per - 1, 0), col + h))

        in_specs += [cur(cq), prev(ck), cur(ck), prev(cv), cur(cv)]
        args += [proj] * 5
    in_specs.append(pl.BlockSpec((N_GROUPS, None, WIN_BLK, 2 * WIN_BLK), lambda b, i, h: (0, h, 0, 0)))
    args.append(bias_tiles)
    return pl.pallas_call(
        _attn_prompt_kernel,
        grid=(batch, nt, DIL_HEADS),
        in_specs=in_specs,
        out_specs=pl.BlockSpec((tq, LANES), lambda b, i, h: (b * nt + i, h)),
        out_shape=jax.ShapeDtypeStruct((batch * seq, DIL_W), F32),
        scratch_shapes=[pltpu.VMEM((N_GROUPS, tq, LANES), F32)] * 3
        + [pltpu.VMEM((N_GROUPS, WIN_BLK, 2 * WIN_BLK), F32), pltpu.VMEM((5, tq, LANES), F32)],
        compiler_params=pltpu.CompilerParams(
            dimension_semantics=("parallel", "parallel", "parallel"), vmem_limit_bytes=VMEM_LIMIT),
        name="attn_prompt",
    )(*args)


def _attn_sample_kernel(*refs):
    q_refs = refs[0:N_GROUPS]
    kn_refs = refs[N_GROUPS:2 * N_GROUPS]
    vn_refs = refs[2 * N_GROUPS:3 * N_GROUPS]
    kc_refs = refs[3 * N_GROUPS:4 * N_GROUPS]
    vc_refs = refs[4 * N_GROUPS:5 * N_GROUPS]
    bias0_ref, bias_ref, o_ref = refs[5 * N_GROUPS:5 * N_GROUPS + 3]
    m_ref, l_ref, acc_ref = refs[5 * N_GROUPS + 3:]
    t = pl.program_id(0)
    scale = HEAD_DIM ** -0.5

    def logits(gi, k, bias):
        s = jnp.sum(q_refs[gi][...] * k, axis=-1, keepdims=True) * scale + bias
        return jnp.broadcast_to(s, k.shape)

    @pl.when(t == 0)
    def _():
        for gi in range(N_GROUPS):
            s = logits(gi, kn_refs[gi][...], bias0_ref[gi])
            m_ref[gi] = s
            l_ref[gi] = jnp.ones_like(s)
            acc_ref[gi] = vn_refs[gi][...]

    for gi in range(N_GROUPS):
        s = logits(gi, kc_refs[gi][...], bias_ref[gi])
        m_old = m_ref[gi]
        m_new = jnp.maximum(m_old, s)
        alpha = jnp.exp(m_old - m_new)
        p = jnp.exp(s - m_new)
        l_ref[gi] = alpha * l_ref[gi] + p
        acc_ref[gi] = alpha * acc_ref[gi] + p * vc_refs[gi][...]
        m_ref[gi] = m_new

    @pl.when(t == pl.num_programs(0) - 1)
    def _():
        m_all = jnp.maximum(jnp.maximum(m_ref[0], m_ref[1]), m_ref[2])
        num = jnp.zeros(acc_ref.shape[1:], F32)
        den = jnp.zeros(m_ref.shape[1:], F32)
        for gi in range(N_GROUPS):
            wgt = jnp.exp(m_ref[gi] - m_all)
            num = num + wgt * acc_ref[gi]
            den = den + wgt * l_ref[gi]
        o_ref[...] = num / den


def _attn_sample(q_g, kn_g, vn_g, k_caches, v_caches, bias0, bias_steps):
    batch = q_g[0].shape[0]
    small = pl.BlockSpec((batch, DIL_HEADS, HEAD_DIM), lambda t: (0, 0, 0))
    in_specs = [small] * (3 * N_GROUPS)
    for _ in range(2):
        for _, dil in DIL_GROUPS:
            in_specs.append(pl.BlockSpec((None, batch, None, DIL_HEADS, HEAD_DIM),
                                         lambda t, dil=dil: (0, 0, t * dil, 0, 0)))
    in_specs.append(pl.BlockSpec((N_GROUPS, 1, DIL_HEADS, 1), lambda t: (0, 0, 0, 0)))
    in_specs.append(pl.BlockSpec((None, N_GROUPS, 1, DIL_HEADS, 1), lambda t: (t, 0, 0, 0, 0)))
    return pl.pallas_call(
        _attn_sample_kernel,
        grid=(WIN_BLK,),
        in_specs=in_specs,
        out_specs=pl.BlockSpec((batch, DIL_HEADS, HEAD_DIM), lambda t: (0, 0, 0)),
        out_shape=jax.ShapeDtypeStruct((batch, DIL_HEADS, HEAD_DIM), F32),
        scratch_shapes=[
            pltpu.VMEM((N_GROUPS, batch, DIL_HEADS, HEAD_DIM), F32),
            pltpu.VMEM((N_GROUPS, batch, DIL_HEADS, HEAD_DIM), F32),
            pltpu.VMEM((N_GROUPS, batch, DIL_HEADS, HEAD_DIM), F32),
        ],
        compiler_params=pltpu.CompilerParams(
            dimension_semantics=("arbitrary",), vmem_limit_bytes=VMEM_LIMIT),
        name="attn_sample",
    )(*q_g, *kn_g, *vn_g, *k_caches, *v_caches, bias0, bias_steps)


def _output_kernel(ob_ref, zb_ref, gateb_ref, ha_ref, x_ref, wpb_ref, wout_ref, gpost_ref, y_ref):
    yb = (ob_ref[...] * _silu(zb_ref[...])).astype(BF16)
    p_b = jnp.dot(yb, wpb_ref[...], preferred_element_type=F32)
    hid = ha_ref[...].astype(F32) + _sigmoid(gateb_ref[...].astype(F32)) * p_b
    out = jnp.dot(hid.astype(BF16), wout_ref[...], preferred_element_type=F32)
    ms = jnp.mean(out * out, axis=-1, keepdims=True)
    y_ref[...] = x_ref[...] + out * lax.rsqrt(ms + RMS_EPS) * gpost_ref[...]


def _output(o_b, proj, gates, h_a, x2d, w_proj_b, w_out, g_post, tm):
    m = x2d.shape[0]
    return pl.pallas_call(
        _output_kernel,
        grid=(m // tm,),
        in_specs=[
            pl.BlockSpec((tm, DIL_W), lambda i: (i, 0)),
            pl.BlockSpec((tm, DIL_W), lambda i: (i, COL_Z_B // DIL_W)),
            pl.BlockSpec((tm, D_MODEL), lambda i: (i, COL_GATE_B // D_MODEL)),
            pl.BlockSpec((tm, D_MODEL), lambda i: (i, 0)),
            pl.BlockSpec((tm, D_MODEL), lambda i: (i, 0)),
            pl.BlockSpec((DIL_W, D_MODEL), lambda i: (0, 0)),
            pl.BlockSpec((D_MODEL, D_MODEL), lambda i: (0, 0)),
            pl.BlockSpec((1, D_MODEL), lambda i: (0, 0)),
        ],
        out_specs=pl.BlockSpec((tm, D_MODEL), lambda i: (i, 0)),
        out_shape=jax.ShapeDtypeStruct((m, D_MODEL), F32),
        compiler_params=pltpu.CompilerParams(
            dimension_semantics=("parallel",), vmem_limit_bytes=VMEM_LIMIT),
        name="output",
    )(o_b, proj, gates, h_a, x2d, w_proj_b, w_out, g_post)


def _rel_bucket(dist):
    max_exact = REL_BUCKETS // 2
    d = jnp.maximum(dist, 1).astype(F32)
    large = max_exact + (jnp.log(d / max_exact) / math.log(REL_MAX_DIST / max_exact)
                         * (REL_BUCKETS - max_exact)).astype(jnp.int32)
    large = jnp.minimum(large, REL_BUCKETS - 1)
    return jnp.where(dist < max_exact, dist, large)


def _group_biases(rel_table):
    out = []
    for gi, (win, dil) in enumerate(DIL_GROUPS):
        dist = jnp.arange(win // dil + 1, dtype=jnp.int32) * dil
        b = rel_table[_rel_bucket(dist)][:, gi * DIL_HEADS:(gi + 1) * DIL_HEADS]
        out.append(b.T.astype(F32))
    return out


def _bias_tiles(biases):
    period = 3 * WIN_BLK
    tiles = []
    for b in biases:
        u = jnp.concatenate([b[:, ::-1], jnp.full((DIL_HEADS, period - WIN_BLK - 1), NEG, F32)], axis=1)
        flat = jnp.tile(u, (1, WIN_BLK))[:, :WIN_BLK * (period - 1)]
        tiles.append(flat.reshape(DIL_HEADS, WIN_BLK, period - 1)[:, :, :2 * WIN_BLK])
    return jnp.stack(tiles, axis=0).astype(F32)


def _layer_weights(w_in, conv_w, a_log, dt_bias, g_head_norm, w_proj_a, w_proj_b, w_out):
    w_main = jnp.concatenate(
        [w_in[:, SRC_QKV_A:SRC_Z_A], w_in[:, SRC_QKV_B:SRC_Z_B], w_in[:, SRC_Z_B:SRC_GATE_A],
         w_in[:, SRC_Z_A:SRC_AB], w_in[:, SRC_GATE_A:SRC_GATE_B], w_in[:, SRC_GATE_B:SRC_END]],
        axis=1).astype(BF16)
    w_ab = jnp.pad(w_in[:, SRC_AB:SRC_QKV_B], ((0, 0), (0, LANES - 2 * GDN_HEADS))).astype(BF16)
    pad = (0, LANES - GDN_HEADS)
    return dict(
        w_main=w_main, w_ab=w_ab, conv_wt=conv_w.T.astype(F32),
        alog_pad=jnp.pad(a_log.astype(F32), pad)[None, :],
        dtb_pad=jnp.pad(dt_bias.astype(F32), pad)[None, :],
        ghn=g_head_norm.astype(F32)[None, :],
        w_proj_a=w_proj_a.astype(BF16), w_proj_b=w_proj_b.astype(BF16), w_out=w_out.astype(BF16))


def _window_rows_kernel(k_ref, v_ref, ko_ref, vo_ref):
    rows = k_ref.shape[0]
    for src, dst in ((k_ref, ko_ref), (v_ref, vo_ref)):
        for h in range(DIL_HEADS):
            dst[pl.ds(h, rows, stride=DIL_HEADS), :] = src[:, h * HEAD_DIM:(h + 1) * HEAD_DIM]


def _window_rows(proj3, col_k, keep):
    batch, seq, _ = proj3.shape
    rows = min(keep, 1024)
    first = (seq - keep) // rows
    out = jax.ShapeDtypeStruct((batch, keep * DIL_HEADS, HEAD_DIM), F32)
    k_out, v_out = pl.pallas_call(
        _window_rows_kernel,
        grid=(batch, keep // rows),
        in_specs=[
            pl.BlockSpec((None, rows, DIL_W), lambda b, r: (b, first + r, col_k // DIL_W)),
            pl.BlockSpec((None, rows, DIL_W), lambda b, r: (b, first + r, col_k // DIL_W + 1)),
        ],
        out_specs=[
            pl.BlockSpec((None, rows * DIL_HEADS, HEAD_DIM), lambda b, r: (b, r, 0)),
            pl.BlockSpec((None, rows * DIL_HEADS, HEAD_DIM), lambda b, r: (b, r, 0)),
        ],
        out_shape=[out, out],
        compiler_params=pltpu.CompilerParams(
            dimension_semantics=("parallel", "parallel"), vmem_limit_bytes=VMEM_LIMIT),
        name="window_rows",
    )(proj3, proj3)
    shape = (batch, keep, DIL_HEADS, HEAD_DIM)
    return k_out.reshape(shape), v_out.reshape(shape)


def kernel(x_prompt, x_sample, state_gdn, state_conv, cache_k_w128, cache_v_w128, cache_k_w512,
           cache_v_w512, cache_k_w2048, cache_v_w2048, rel_table, g_pre, w_in, conv_w, a_log, dt_bias,
           g_head_norm, w_proj_a, w_proj_b, w_out, g_post):
    batch, seq, _ = x_prompt.shape
    dec_batch = x_sample.shape[0]
    k_caches = (cache_k_w128, cache_k_w512, cache_k_w2048)
    v_caches = (cache_v_w128, cache_v_w512, cache_v_w2048)
    biases = _group_biases(rel_table)
    bias_tiles = _bias_tiles(biases) * LOG2_E
    lw = _layer_weights(w_in[0], conv_w[0], a_log[0], dt_bias[0], g_head_norm[0], w_proj_a[0],
                        w_proj_b[0], w_out[0])
    g_pre_l = g_pre[0][None, :]
    g_post_l = g_post[0][None, :]

    xp = x_prompt.reshape(batch * seq, D_MODEL)
    proj, gates, ab = _in_projection(xp, g_pre_l, lw["w_main"], lw["w_ab"], tm=2048, tn=1024)
    h_a, p_gdn = _gdn_prompt(proj, gates, ab, lw["conv_wt"], lw["alog_pad"], lw["dtb_pad"], lw["ghn"],
                             lw["w_proj_a"], batch, seq)
    o_b = _attn_prompt(proj, bias_tiles, batch, seq)
    y_prompt = _output(o_b, proj, gates, h_a, xp, lw["w_proj_b"], lw["w_out"], g_post_l, tm=1024)
    y_prompt = y_prompt.reshape(batch, seq, D_MODEL)
    proj3 = proj.reshape(batch, seq, WIDE_WIDTH)
    p_conv = proj3[:, seq - (CONV_WIDTH - 1):, :CONV_DIM]
    p_rows = []
    for gi, (win, _) in enumerate(DIL_GROUPS):
        keep = min(win, seq)
        ck = COL_QKV_B + gi * 3 * DIL_W + DIL_W
        p_rows += list(_window_rows(proj3, ck, keep))

    xs = x_sample.reshape(dec_batch, D_MODEL)
    proj_s, gates_s, ab_s = _in_projection(xs, g_pre_l, lw["w_main"], lw["w_ab"], tm=dec_batch, tn=1024)
    sc_t = jnp.transpose(state_conv[0], (1, 0, 2))
    o_s, s_gdn = _gdn_sample(sc_t, proj_s, ab_s, lw["conv_wt"], lw["alog_pad"], lw["dtb_pad"],
                             state_gdn[0], nb=32)
    h_a_s = _gdn_sample_out(o_s, gates_s, lw["ghn"], lw["w_proj_a"])
    q_g, kn_g, vn_g = [], [], []
    for gi in range(N_GROUPS):
        cq = COL_QKV_B + gi * 3 * DIL_W
        q_g.append(proj_s[:, cq:cq + DIL_W].reshape(dec_batch, DIL_HEADS, HEAD_DIM))
        kn_g.append(proj_s[:, cq + DIL_W:cq + 2 * DIL_W].reshape(dec_batch, DIL_HEADS, HEAD_DIM))
        vn_g.append(proj_s[:, cq + 2 * DIL_W:cq + 3 * DIL_W].reshape(dec_batch, DIL_HEADS, HEAD_DIM))
    bias_all = jnp.stack(biases, axis=0)
    bias0 = bias_all[:, None, :, 0:1]
    bias_steps = jnp.transpose(bias_all[:, :, :0:-1], (2, 0, 1))[:, :, None, :, None]
    o_b_s = _attn_sample(q_g, kn_g, vn_g, k_caches, v_caches, bias0, bias_steps)
    y_sample = _output(o_b_s.reshape(dec_batch, DIL_W), proj_s, gates_s, h_a_s, xs, lw["w_proj_b"],
                       lw["w_out"], g_post_l, tm=dec_batch)
    y_sample = y_sample.reshape(dec_batch, 1, D_MODEL)
    s_conv = jnp.concatenate([state_conv[0][:, 1:, :], proj_s[:, None, :CONV_DIM]], axis=1)
    s_rows = []
    for gi in range(N_GROUPS):
        s_rows += [kn_g[gi][:, None], vn_g[gi][:, None]]

    return (y_prompt, y_sample, p_gdn[None], p_conv[None], *[r[None] for r in p_rows],
            s_gdn[None], s_conv[None], *[r[None] for r in s_rows])
```

```python
import functools
import math

import jax
import jax.numpy as jnp
from jax import lax
from jax.experimental import pallas as pl
from jax.experimental.pallas import tpu as pltpu

F32 = jnp.float32
BF16 = jnp.bfloat16

D_MODEL = 1024
GDN_HEADS = 8
HEAD_DIM = 128
GDN_QK = GDN_HEADS * HEAD_DIM
CONV_DIM = 3 * GDN_QK
CONV_WIDTH = 4
DIL_GROUPS = ((128, 1), (512, 4), (2048, 16))
N_GROUPS = 3
DIL_HEADS = 4
DIL_W = DIL_HEADS * HEAD_DIM
WIN_BLK = 128
REL_BUCKETS = 32
REL_MAX_DIST = 2048
RMS_EPS = 1e-6
NEG = -1e30
LOG2_E = math.log2(math.e)

COL_QKV_A = 0
COL_QKV_B = 3072
COL_Z_B = 7680
WIDE_WIDTH = 8192
COL_Z_A = 0
COL_GATE_A = 1024
COL_GATE_B = 2048
GATE_WIDTH = 3072
MAIN_WIDTH = WIDE_WIDTH + GATE_WIDTH
SRC_QKV_A, SRC_Z_A, SRC_AB, SRC_QKV_B, SRC_Z_B, SRC_GATE_A, SRC_GATE_B, SRC_END = (
    0, 3072, 4096, 4112, 8720, 9232, 10256, 11280)
LANES = 128

VMEM_LIMIT = 56 * 1024 * 1024
INPROJ_VMEM_LIMIT = 60 * 1024 * 1024

GDN_CHUNK = 64
GDN_SEQS = 4
ATTN_TILE = 2048
ATTN_UNITS = 16
MAX_ROW_STRIDE = 4


def _silu(x):
    half = 0.5 * x
    return half * jnp.tanh(half) + half


def _sigmoid(x):
    return 1.0 / (1.0 + jnp.exp(-x))


def _softplus(x):
    return jnp.maximum(x, 0.0) + jnp.log(1.0 + jnp.exp(-jnp.abs(x)))


def _dot(a, b):
    return jnp.dot(a.astype(BF16), b.astype(BF16), preferred_element_type=F32)


def _dot_nt(a, b):
    return lax.dot_general(a.astype(BF16), b.astype(BF16), (((1,), (1,)), ((), ())),
                           preferred_element_type=F32)


def _dot_tn(a, b):
    return lax.dot_general(a.astype(BF16), b.astype(BF16), (((0,), (0,)), ((), ())),
                           preferred_element_type=F32)


def _inproj_kernel(x_ref, g_ref, w_ref, wab_ref, wide_ref, gate_ref, ab_ref, xn_ref, *, wide_tiles):
    j = pl.program_id(1)

    @pl.when(j == 0)
    def _():
        x = x_ref[...]
        ms = jnp.mean(x * x, axis=-1, keepdims=True)
        xn = (x * lax.rsqrt(ms + RMS_EPS) * g_ref[...]).astype(BF16)
        xn_ref[...] = xn
        ab_ref[...] = jnp.dot(xn, wab_ref[...], preferred_element_type=F32)

    @pl.when(j < wide_tiles)
    def _():
        wide_ref[...] = jnp.dot(xn_ref[...], w_ref[...], preferred_element_type=F32)

    @pl.when(j >= wide_tiles)
    def _():
        gate_ref[...] = jnp.dot(xn_ref[...], w_ref[...], preferred_element_type=F32).astype(gate_ref.dtype)


def _in_projection(x2d, g_pre, w_main, w_ab, tm, tn):
    m = x2d.shape[0]
    wide_tiles = WIDE_WIDTH // tn
    return pl.pallas_call(
        functools.partial(_inproj_kernel, wide_tiles=wide_tiles),
        grid=(m // tm, MAIN_WIDTH // tn),
        in_specs=[
            pl.BlockSpec((tm, D_MODEL), lambda i, j: (i, 0)),
            pl.BlockSpec((1, D_MODEL), lambda i, j: (0, 0)),
            pl.BlockSpec((D_MODEL, tn), lambda i, j: (0, j)),
            pl.BlockSpec((D_MODEL, LANES), lambda i, j: (0, 0)),
        ],
        out_specs=[
            pl.BlockSpec((tm, tn), lambda i, j: (i, jnp.minimum(j, wide_tiles - 1))),
            pl.BlockSpec((tm, tn), lambda i, j: (i, jnp.maximum(j - wide_tiles, 0))),
            pl.BlockSpec((tm, LANES), lambda i, j: (i, 0)),
        ],
        out_shape=[
            jax.ShapeDtypeStruct((m, WIDE_WIDTH), F32),
            jax.ShapeDtypeStruct((m, GATE_WIDTH), BF16),
            jax.ShapeDtypeStruct((m, LANES), F32),
        ],
        scratch_shapes=[pltpu.VMEM((tm, D_MODEL), BF16)],
        compiler_params=pltpu.CompilerParams(
            dimension_semantics=("parallel", "arbitrary"), vmem_limit_bytes=INPROJ_VMEM_LIMIT),
        name="in_projection",
    )(x2d, g_pre, w_main, w_ab)


GROUP_HEADS = 4


def _block_diag_rhs(x4, half_masks):
    c = x4.shape[0]
    zeros = jnp.zeros((c, LANES), x4.dtype)
    row_blocks = []
    for h in range(GROUP_HEADS):
        piece = x4[:, (h // 2) * LANES:(h // 2 + 1) * LANES] * half_masks[h % 2]
        row_blocks.append(jnp.concatenate([piece if v == h // 2 else zeros for v in range(2)], axis=1))
    return jnp.concatenate(row_blocks, axis=0)


def _gdn_prompt_kernel(qkv_ref, z_ref, ab_ref, gate_ref, convw_ref, alog_ref, dtb_ref, ghn_ref,
                       wproj_ref, ha_ref, sfin_ref, xpad_ref, qkvn_ref, o_ref, s_ref,
                       u_ref, w_ref, qd_ref, kd_ref, qk_ref):
    c = GDN_CHUNK
    tl = GDN_SEQS * c
    gw = GROUP_HEADS * c
    step = pl.program_id(1)

    def conv_stage(i):
        for blk in range(CONV_DIM // LANES):
            cols = slice(blk * LANES, (blk + 1) * LANES)
            xpad_ref[i, blk, 8:8 + c, :] = qkv_ref[i, :, cols]
            for phase in range(2):
                acc = None
                for w in range(CONV_WIDTH):
                    tap = xpad_ref[i, blk, pl.ds(5 + phase + w, c // 2, stride=2), :] * convw_ref[w:w + 1, cols]
                    acc = tap if acc is None else acc + tap
                y = _silu(acc)
                if blk < 2 * GDN_HEADS:
                    y = y * lax.rsqrt(jnp.sum(y * y, axis=-1, keepdims=True) + RMS_EPS)
                    if blk < GDN_HEADS:
                        y = y * (HEAD_DIM ** -0.5)
                qkvn_ref[blk, pl.ds(i * c + phase, c // 2, stride=2), :] = y
            xpad_ref[i, blk, 5:8, :] = xpad_ref[i, blk, c + 5:c + 8, :]

    @pl.when(step == 0)
    def _():
        xpad_ref[:, :, 0:8, :] = jnp.zeros((GDN_SEQS, CONV_DIM // LANES, 8, LANES), F32)
        s_ref[...] = jnp.zeros_like(s_ref)

    ab = ab_ref[...].reshape(tl, LANES)
    g_full = -jnp.exp(alog_ref[...]) * _softplus(ab + dtb_ref[...])
    beta_full = _sigmoid(ab)
    row = lax.broadcasted_iota(jnp.int32, (tl, tl), 0)
    col = lax.broadcasted_iota(jnp.int32, (tl, tl), 1)
    same_chunk = (row // c) == (col // c)
    tri = jnp.where(same_chunk & (col <= row), 1.0, 0.0).astype(BF16)
    g_hi = g_full.astype(BF16)
    g_rest = g_full - g_hi.astype(F32)
    g_mid = g_rest.astype(BF16)
    g_lo = (g_rest - g_mid.astype(F32)).astype(BF16)
    g_cum = jnp.dot(jnp.concatenate([tri, tri, tri], axis=1), jnp.concatenate([g_hi, g_mid, g_lo], axis=0),
                    preferred_element_type=F32)
    g_cum_t = g_cum.T
    g_last = jnp.concatenate(
        [jnp.broadcast_to(g_cum[i * c + c - 1:(i + 1) * c, :], (c, LANES)) for i in range(GDN_SEQS)], axis=0)
    exp_g = jnp.exp(g_cum)
    exp_rest = jnp.exp(g_last - g_cum)
    exp_last = jnp.exp(g_last)

    ci = lax.broadcasted_iota(jnp.int32, (c, gw), 0)
    cj = lax.broadcasted_iota(jnp.int32, (c, gw), 1) % c
    causal4 = cj <= ci
    strict4 = cj < ci
    half_lane = lax.broadcasted_iota(jnp.int32, (c, LANES), 1) // c
    half_masks = [jnp.where(half_lane == k, 1.0, 0.0).astype(BF16) for k in range(2)]

    def lanes_of(col_values, width):
        return jnp.concatenate([jnp.broadcast_to(v, (c, width)) for v in col_values], axis=1)

    n_groups = GDN_HEADS // GROUP_HEADS
    zero_s = jnp.zeros((HEAD_DIM, HEAD_DIM), BF16)
    zero_v = jnp.zeros((c, HEAD_DIM), BF16)

    def group_cols(hg):
        return slice(hg * GROUP_HEADS * LANES, (hg + 1) * GROUP_HEADS * LANES)

    def bd_dot(lhs, x4):
        return jnp.dot(lhs.astype(BF16), _block_diag_rhs(x4.astype(BF16), half_masks),
                       preferred_element_type=F32)

    def seq_stages(i):
        rows = slice(i * c, (i + 1) * c)
        st = {}

        def build():
            a4s = []
            for hg in range(n_groups):
                heads = range(hg * GROUP_HEADS, (hg + 1) * GROUP_HEADS)
                cols = group_cols(hg)

                def head_blocks(first_blk):
                    return jnp.concatenate([qkvn_ref[first_blk + h, rows, :] for h in heads], axis=1)

                q4 = head_blocks(0)
                k4 = head_blocks(GDN_HEADS)
                v4 = head_blocks(2 * GDN_HEADS)
                beta_e = lanes_of([beta_full[rows, GDN_HEADS + h:GDN_HEADS + h + 1] for h in heads], LANES)
                eg_e = lanes_of([exp_g[rows, h:h + 1] for h in heads], LANES)
                er_e = lanes_of([exp_rest[rows, h:h + 1] for h in heads], LANES)
                gi4 = lanes_of([g_cum[rows, h:h + 1] for h in heads], c)
                gj4 = jnp.concatenate([g_cum_t[h:h + 1, rows] for h in heads], axis=1)
                decay4 = jnp.exp(jnp.where(causal4, gi4 - gj4, NEG))
                kb4 = k4 * beta_e
                k4b = k4.astype(BF16)
                k_rhs = jnp.concatenate(
                    [jnp.concatenate([k4b[:, g * LANES:(g + 1) * LANES] if g == h else zero_v
                                      for g in range(GROUP_HEADS)], axis=1)
                     for h in range(GROUP_HEADS)], axis=0)
                kkqk = lax.dot_general(jnp.concatenate([kb4, q4], axis=0).astype(BF16), k_rhs,
                                       (((1,), (1,)), ((), ())), preferred_element_type=F32)
                a4s.append(jnp.where(strict4, kkqk[:c] * decay4, 0.0))
                qk_ref[rows, hg * gw:(hg + 1) * gw] = kkqk[c:] * decay4
                u_ref[rows, cols] = v4 * beta_e
                w_ref[rows, cols] = kb4 * eg_e
                qd_ref[rows, cols] = q4 * eg_e
                kd_ref[rows, cols] = k4 * er_e
            st["a"] = a4s

        def square():
            st["e"] = [-a4 for a4 in st["a"]]
            st["x"] = [bd_dot(a4, a4) for a4 in st["a"]]

        def double():
            boths = [bd_dot(jnp.concatenate([e, x], axis=0), x) for e, x in zip(st["e"], st["x"])]
            st["e"] = [e + x + both[:c] for e, x, both in zip(st["e"], st["x"], boths)]
            st["x"] = [both[c:] for both in boths]

        def last_factor():
            st["e"] = [e + x + bd_dot(e, x) for e, x in zip(st["e"], st["x"])]

        def solve():
            for hg, e4 in enumerate(st["e"]):
                cols = group_cols(hg)
                vb4 = u_ref[rows, cols]
                kbe4 = w_ref[rows, cols]
                us, ws = [], []
                for k in range(GROUP_HEADS):
                    hc = slice(k * LANES, (k + 1) * LANES)
                    rhs = jnp.concatenate([vb4[:, hc], kbe4[:, hc]], axis=1)
                    uw = rhs + _dot(e4[:, k * c:(k + 1) * c], rhs)
                    us.append(uw[:, :LANES])
                    ws.append(uw[:, LANES:])
                u_ref[rows, cols] = jnp.concatenate(us, axis=1)
                w_ref[rows, cols] = jnp.concatenate(ws, axis=1)

        def pair_cols(p):
            return slice(2 * p * LANES, (2 * p + 2) * LANES)

        def state_products():
            wsqs = []
            for p in range(GDN_HEADS // 2):
                cols = pair_cols(p)
                s_bd = jnp.concatenate(
                    [jnp.concatenate([s_ref[i, 2 * p].astype(BF16), zero_s], axis=1),
                     jnp.concatenate([zero_s, s_ref[i, 2 * p + 1].astype(BF16)], axis=1)], axis=0)
                wq = jnp.concatenate([w_ref[rows, cols], qd_ref[rows, cols]], axis=0).astype(BF16)
                wsqs.append(jnp.dot(wq, s_bd, preferred_element_type=F32))
            st["wsq"] = wsqs

        def outputs():
            vnbs = []
            for p, wsq in enumerate(st["wsq"]):
                cols = pair_cols(p)
                vnb = (u_ref[rows, cols] - wsq[:c]).astype(BF16)
                vnbs.append(vnb)
                vn_bd = jnp.concatenate([jnp.concatenate([vnb[:, :LANES], zero_v], axis=1),
                                         jnp.concatenate([zero_v, vnb[:, LANES:]], axis=1)], axis=0)
                qk2 = qk_ref[rows, 2 * p * c:(2 * p + 2) * c]
                o_ref[rows, cols] = wsq[c:] + jnp.dot(qk2.astype(BF16), vn_bd, preferred_element_type=F32)
            st["vn"] = vnbs

        def state_update():
            for p, vnb in enumerate(st["vn"]):
                kd2 = kd_ref[rows, pair_cols(p)]
                for k in range(2):
                    h = 2 * p + k
                    hc = slice(k * LANES, (k + 1) * LANES)
                    el = exp_last[i * c:i * c + 1, h:h + 1]
                    s_ref[i, h] = s_ref[i, h] * el + _dot_tn(kd2[:, hc], vnb[:, hc])

        doublings = int(math.log2(c)) - 2
        return ([functools.partial(conv_stage, i), build, square] + [double] * doublings
                + [last_factor, solve, state_products, outputs, state_update])

    for stages in zip(*[seq_stages(i) for i in range(GDN_SEQS)]):
        for stage in stages:
            stage()

    for h in range(GDN_HEADS):
        cols = slice(h * LANES, (h + 1) * LANES)
        o = o_ref[:, cols]
        o = o * lax.rsqrt(jnp.mean(o * o, axis=-1, keepdims=True) + RMS_EPS) * ghn_ref[...]
        o_ref[:, cols] = o * _silu(z_ref[:, :, cols].astype(F32).reshape(tl, LANES))
    p_a = jnp.dot(o_ref[...].astype(BF16), wproj_ref[...], preferred_element_type=F32)
    h_a = _sigmoid(gate_ref[...].astype(F32).reshape(tl, D_MODEL)) * p_a
    ha_ref[...] = h_a.reshape(GDN_SEQS, c, D_MODEL).astype(ha_ref.dtype)

    @pl.when(step == pl.num_programs(1) - 1)
    def _():
        sfin_ref[...] = s_ref[...]


def _gdn_prompt(proj, gates, ab, conv_wt, alog_pad, dtb_pad, ghn, w_proj_a, batch, seq):
    c = GDN_CHUNK
    ns = GDN_SEQS
    tl = ns * c
    proj3 = proj.reshape(batch, seq, WIDE_WIDTH)
    gates3 = gates.reshape(batch, seq, GATE_WIDTH)
    ab3 = ab.reshape(batch, seq, LANES)
    h_a, s_fin = pl.pallas_call(
        _gdn_prompt_kernel,
        grid=(batch // ns, seq // c),
        in_specs=[
            pl.BlockSpec((ns, c, CONV_DIM), lambda b, l: (b, l, COL_QKV_A // CONV_DIM)),
            pl.BlockSpec((ns, c, GDN_QK), lambda b, l: (b, l, COL_Z_A // GDN_QK)),
            pl.BlockSpec((ns, c, LANES), lambda b, l: (b, l, 0)),
            pl.BlockSpec((ns, c, D_MODEL), lambda b, l: (b, l, COL_GATE_A // D_MODEL)),
            pl.BlockSpec((CONV_WIDTH, CONV_DIM), lambda b, l: (0, 0)),
            pl.BlockSpec((1, LANES), lambda b, l: (0, 0)),
            pl.BlockSpec((1, LANES), lambda b, l: (0, 0)),
            pl.BlockSpec((1, LANES), lambda b, l: (0, 0)),
            pl.BlockSpec((GDN_QK, D_MODEL), lambda b, l: (0, 0)),
        ],
        out_specs=[
            pl.BlockSpec((ns, c, D_MODEL), lambda b, l: (b, l, 0)),
            pl.BlockSpec((ns, GDN_HEADS, HEAD_DIM, HEAD_DIM), lambda b, l: (b, 0, 0, 0)),
        ],
        out_shape=[
            jax.ShapeDtypeStruct((batch, seq, D_MODEL), BF16),
            jax.ShapeDtypeStruct((batch, GDN_HEADS, HEAD_DIM, HEAD_DIM), F32),
        ],
        scratch_shapes=[
            pltpu.VMEM((ns, CONV_DIM // LANES, c + 8, LANES), F32),
            pltpu.VMEM((CONV_DIM // LANES, tl, LANES), F32),
            pltpu.VMEM((tl, GDN_QK), F32),
            pltpu.VMEM((ns, GDN_HEADS, HEAD_DIM, HEAD_DIM), F32),
            pltpu.VMEM((tl, GDN_QK), F32),
            pltpu.VMEM((tl, GDN_QK), F32),
            pltpu.VMEM((tl, GDN_QK), F32),
            pltpu.VMEM((tl, GDN_QK), F32),
            pltpu.VMEM((tl, GDN_HEADS * GDN_CHUNK), F32),
        ],
        compiler_params=pltpu.CompilerParams(
            dimension_semantics=("parallel", "arbitrary"), vmem_limit_bytes=VMEM_LIMIT),
        name="gdn_prompt",
    )(proj3, gates3, ab3, gates3, conv_wt, alog_pad, dtb_pad, ghn, w_proj_a)
    return h_a.reshape(batch * seq, D_MODEL), s_fin


def _gdn_sample_kernel(sc_q_ref, sc_k_ref, sc_v_ref, nq_ref, nk_ref, nv_ref, cwq_ref, cwk_ref, cwv_ref,
                       ab_ref, alog_ref, dtb_ref, s_ref, o_ref, snew_ref):
    h = pl.program_id(0)
    nb = nq_ref.shape[0]

    def conv_silu(sc_ref, new_ref, cw_ref):
        acc = new_ref[...] * cw_ref[3:4, :]
        for w in range(CONV_WIDTH - 1):
            acc = acc + sc_ref[w] * cw_ref[w:w + 1, :]
        return _silu(acc)

    q = conv_silu(sc_q_ref, nq_ref, cwq_ref)
    k = conv_silu(sc_k_ref, nk_ref, cwk_ref)
    v = conv_silu(sc_v_ref, nv_ref, cwv_ref)
    q = q * lax.rsqrt(jnp.sum(q * q, axis=-1, keepdims=True) + RMS_EPS) * (HEAD_DIM ** -0.5)
    k = k * lax.rsqrt(jnp.sum(k * k, axis=-1, keepdims=True) + RMS_EPS)

    ab = ab_ref[...]
    lane = lax.broadcasted_iota(jnp.int32, ab.shape, 1)
    g_full = -jnp.exp(alog_ref[...]) * _softplus(ab + dtb_ref[...])
    g = jnp.sum(jnp.where(lane == h, g_full, 0.0), axis=-1, keepdims=True)
    beta = jnp.sum(jnp.where(lane == h + GDN_HEADS, _sigmoid(ab), 0.0), axis=-1, keepdims=True)
    eg = jnp.exp(g)

    eye = (lax.broadcasted_iota(jnp.int32, (HEAD_DIM, HEAD_DIM), 0)
           == lax.broadcasted_iota(jnp.int32, (HEAD_DIM, HEAD_DIM), 1))

    def as_column(row):
        return jnp.sum(jnp.where(eye, row, 0.0), axis=1, keepdims=True)

    for b in range(nb):
        s = s_ref[b]
        k_col = as_column(k[b:b + 1, :])
        q_col = as_column(q[b:b + 1, :])
        e = eg[b:b + 1, :]
        ks = jnp.sum(k_col * s, axis=0, keepdims=True)
        v_new = beta[b:b + 1, :] * (v[b:b + 1, :] - e * ks)
        s_new = s * e + k_col * v_new
        snew_ref[b] = s_new
        o_ref[b:b + 1, :] = jnp.sum(q_col * s_new, axis=0, keepdims=True)


def _gdn_sample(state_conv_t, proj, ab, conv_wt, alog_pad, dtb_pad, state, nb):
    batch = proj.shape[0]
    hq = GDN_QK // LANES
    return pl.pallas_call(
        _gdn_sample_kernel,
        grid=(GDN_HEADS, batch // nb),
        in_specs=[
            pl.BlockSpec((CONV_WIDTH - 1, nb, LANES), lambda h, i: (0, i, h)),
            pl.BlockSpec((CONV_WIDTH - 1, nb, LANES), lambda h, i: (0, i, hq + h)),
            pl.BlockSpec((CONV_WIDTH - 1, nb, LANES), lambda h, i: (0, i, 2 * hq + h)),
            pl.BlockSpec((nb, LANES), lambda h, i: (i, h)),
            pl.BlockSpec((nb, LANES), lambda h, i: (i, hq + h)),
            pl.BlockSpec((nb, LANES), lambda h, i: (i, 2 * hq + h)),
            pl.BlockSpec((CONV_WIDTH, LANES), lambda h, i: (0, h)),
            pl.BlockSpec((CONV_WIDTH, LANES), lambda h, i: (0, hq + h)),
            pl.BlockSpec((CONV_WIDTH, LANES), lambda h, i: (0, 2 * hq + h)),
            pl.BlockSpec((nb, LANES), lambda h, i: (i, 0)),
            pl.BlockSpec((1, LANES), lambda h, i: (0, 0)),
            pl.BlockSpec((1, LANES), lambda h, i: (0, 0)),
            pl.BlockSpec((nb, None, HEAD_DIM, HEAD_DIM), lambda h, i: (i, h, 0, 0)),
        ],
        out_specs=[
            pl.BlockSpec((nb, LANES), lambda h, i: (i, h)),
            pl.BlockSpec((nb, None, HEAD_DIM, HEAD_DIM), lambda h, i: (i, h, 0, 0)),
        ],
        out_shape=[
            jax.ShapeDtypeStruct((batch, GDN_QK), F32),
            jax.ShapeDtypeStruct(state.shape, F32),
        ],
        compiler_params=pltpu.CompilerParams(
            dimension_semantics=("parallel", "parallel"), vmem_limit_bytes=VMEM_LIMIT),
        name="gdn_sample",
    )(state_conv_t, state_conv_t, state_conv_t, proj, proj, proj, conv_wt, conv_wt, conv_wt,
      ab, alog_pad, dtb_pad, state)


def _gdn_sample_out_kernel(o_ref, z_ref, gate_ref, ghn_ref, wproj_ref, ha_ref, y_ref):
    for h in range(GDN_HEADS):
        cols = slice(h * LANES, (h + 1) * LANES)
        o = o_ref[:, cols]
        o = o * lax.rsqrt(jnp.mean(o * o, axis=-1, keepdims=True) + RMS_EPS) * ghn_ref[...]
        y_ref[:, cols] = o * _silu(z_ref[:, cols].astype(F32))
    p_a = jnp.dot(y_ref[...].astype(BF16), wproj_ref[...], preferred_element_type=F32)
    ha_ref[...] = (_sigmoid(gate_ref[...].astype(F32)) * p_a).astype(ha_ref.dtype)


def _gdn_sample_out(o, gates, ghn, w_proj_a):
    batch = o.shape[0]
    return pl.pallas_call(
        _gdn_sample_out_kernel,
        grid=(1,),
        in_specs=[
            pl.BlockSpec((batch, GDN_QK), lambda i: (0, 0)),
            pl.BlockSpec((batch, GDN_QK), lambda i: (0, COL_Z_A // GDN_QK)),
            pl.BlockSpec((batch, D_MODEL), lambda i: (0, COL_GATE_A // D_MODEL)),
            pl.BlockSpec((1, LANES), lambda i: (0, 0)),
            pl.BlockSpec((GDN_QK, D_MODEL), lambda i: (0, 0)),
        ],
        out_specs=pl.BlockSpec((batch, D_MODEL), lambda i: (0, 0)),
        out_shape=jax.ShapeDtypeStruct((batch, D_MODEL), BF16),
        scratch_shapes=[pltpu.VMEM((batch, GDN_QK), F32)],
        compiler_params=pltpu.CompilerParams(vmem_limit_bytes=VMEM_LIMIT),
        name="gdn_sample_out",
    )(o, gates, gates, ghn, w_proj_a)


def _attn_prompt_kernel(*refs):
    ins = refs[:5 * N_GROUPS]
    bias_ref = refs[5 * N_GROUPS]
    o_ref = refs[5 * N_GROUPS + 1]
    acc_ref, m_ref, l_ref, bias_start_ref, split_ref, merged_ref = refs[5 * N_GROUPS + 2:]
    tq = ATTN_TILE
    first_tile = pl.program_id(1) == 0
    scale = (HEAD_DIM ** -0.5) * LOG2_E
    lane2 = lax.broadcasted_iota(jnp.int32, (1, 2 * WIN_BLK), 1)
    start_mask = jnp.where(first_tile & (lane2 < WIN_BLK), NEG, 0.0).astype(F32)
    for gi in range(N_GROUPS):
        bias_start_ref[gi] = bias_ref[gi] + start_mask

    def strided(start, n, stride):
        return pl.ds(start, n, stride=stride) if stride > 1 else pl.ds(start, n)

    for gi, (_, dil) in enumerate(DIL_GROUPS):
        srcs = ins[5 * gi:5 * gi + 5]
        n_blocks = tq // (dil * WIN_BLK)
        inner = min(dil, MAX_ROW_STRIDE)
        outer = dil // inner
        if outer > 1:
            for a, src in enumerate(srcs):
                n = src.shape[0]
                for p in range(outer):
                    split_ref[a, p * (n // outer):(p + 1) * (n // outer), :] = src[strided(p, n // outer, outer), :]
            srcs = [split_ref.at[a] for a in range(5)]
        q_ref, kp_ref, kc_ref, vp_ref, vc_ref = srcs

        def rows_of(ref_rows, r, first, count, inner=inner, outer=outer):
            if outer == 1:
                return strided(r + inner * first, count, inner)
            part = r % outer
            return strided(part * (ref_rows // outer) + r // outer + inner * first, count, inner)

        def run_units(units, q_ref=q_ref, kp_ref=kp_ref, kc_ref=kc_ref, vp_ref=vp_ref, vc_ref=vc_ref,
                      gi=gi, rows_of=rows_of):
            span = kp_ref.shape[0]
            n = len(units)
            st = [{} for _ in units]

            def logits(u):
                r, j = units[u]
                q = (q_ref[rows_of(tq, r, WIN_BLK * j, WIN_BLK), :] * scale).astype(BF16)
                if j == 0:
                    prev = rows_of(span, r, 0, WIN_BLK)
                    cur = rows_of(tq, r, 0, WIN_BLK)
                    kk = jnp.concatenate([kp_ref[prev, :], kc_ref[cur, :]], axis=0).astype(BF16)
                    st[u]["v"] = (vp_ref, prev, vc_ref, cur)
                else:
                    rows = rows_of(tq, r, WIN_BLK * (j - 1), 2 * WIN_BLK)
                    kk = kc_ref[rows, :].astype(BF16)
                    st[u]["v"] = (vc_ref, rows)
                st[u]["s"] = lax.dot_general(q, kk, (((1,), (1,)), ((), ())), preferred_element_type=F32)

            def softmax(u):
                r, j = units[u]
                s = st[u].pop("s") + (bias_start_ref[gi] if j == 0 else bias_ref[gi])
                m = jnp.max(s, axis=-1, keepdims=True)
                p = jnp.exp2(s - m)
                st[u]["m"] = m
                st[u]["l"] = jnp.sum(p, axis=-1, keepdims=True)
                st[u]["p"] = p.astype(BF16)

            def values(u):
                r, j = units[u]
                v = st[u].pop("v")
                if len(v) == 4:
                    vv = jnp.concatenate([v[0][v[1], :], v[2][v[3], :]], axis=0).astype(BF16)
                else:
                    vv = v[0][v[1], :].astype(BF16)
                acc = jnp.dot(st[u].pop("p"), vv, preferred_element_type=F32)
                q_rows = rows_of(tq, r, WIN_BLK * j, WIN_BLK)
                acc_ref[gi, q_rows, :] = acc
                m_ref[gi, q_rows, :] = jnp.broadcast_to(st[u].pop("m"), (WIN_BLK, LANES))
                l_ref[gi, q_rows, :] = jnp.broadcast_to(st[u].pop("l"), (WIN_BLK, LANES))

            for stage in (logits, softmax, values):
                for u in range(n):
                    stage(u)

        if dil == 1:
            for j0 in range(0, n_blocks, ATTN_UNITS):
                run_units([(0, j) for j in range(j0, min(j0 + ATTN_UNITS, n_blocks))])
        else:
            jb = min(n_blocks, ATTN_UNITS)
            rb = max(ATTN_UNITS // n_blocks, 1)
            for j0 in range(0, n_blocks, jb):
                def body(it, carry, j0=j0, jb=jb, rb=rb, run_units=run_units):
                    run_units([(it * rb + k, j) for k in range(rb) for j in range(j0, j0 + jb)])
                    return carry
                for it in range(dil // rb):
                    body(it, 0)

    outers = [dil // min(dil, MAX_ROW_STRIDE) for _, dil in DIL_GROUPS]
    parts = max(outers)
    cnt = 2 * WIN_BLK
    for p in range(parts):
        for i0 in range(0, tq // parts, cnt):
            def rows_in(gi):
                if outers[gi] == parts:
                    return pl.ds(p * (tq // parts) + i0, cnt)
                assert outers[gi] == 1
                return strided(p + parts * i0, cnt, parts)
            m_all = jnp.maximum(jnp.maximum(m_ref[0, rows_in(0), :], m_ref[1, rows_in(1), :]),
                                m_ref[2, rows_in(2), :])
            num = jnp.zeros((cnt, LANES), F32)
            den = jnp.zeros((cnt, LANES), F32)
            for gi in range(N_GROUPS):
                wgt = jnp.exp2(m_ref[gi, rows_in(gi), :] - m_all)
                num = num + wgt * acc_ref[gi, rows_in(gi), :]
                den = den + wgt * l_ref[gi, rows_in(gi), :]
            merged_ref[strided(p + parts * i0, cnt, parts), :] = num / den
    o_ref[...] = merged_ref[...].astype(o_ref.dtype)


def _attn_prompt(proj, bias_tiles, batch, seq):
    tq = ATTN_TILE
    nt = seq // tq
    in_specs = []
    args = []
    for gi, (_, dil) in enumerate(DIL_GROUPS):
        span = dil * WIN_BLK
        cq = (COL_QKV_B + gi * 3 * DIL_W) // LANES
        ck = cq + DIL_HEADS
        cv = ck + DIL_HEADS

        def cur(col):
            return pl.BlockSpec((tq, LANES), lambda b, i, h, col=col: (b * nt + i, col + h))

        def prev(col, span=span):
            per = tq // span
            return pl.BlockSpec(
                (span, LANES),
                lambda b, i, h, col=col, per=per: (jnp.maximum((b * nt + i) * per - 1, 0), col + h))

        in_specs += [cur(cq), prev(ck), cur(ck), prev(cv), cur(cv)]
        args += [proj] * 5
    in_specs.append(pl.BlockSpec((N_GROUPS, None, WIN_BLK, 2 * WIN_BLK), lambda b, i, h: (0, h, 0, 0)))
    args.append(bias_tiles)
    return pl.pallas_call(
        _attn_prompt_kernel,
        grid=(batch, nt, DIL_HEADS),
        in_specs=in_specs,
        out_specs=pl.BlockSpec((tq, LANES), lambda b, i, h: (b * nt + i, h)),
        out_shape=jax.ShapeDtypeStruct((batch * seq, DIL_W), BF16),
        scratch_shapes=[pltpu.VMEM((N_GROUPS, tq, LANES), F32)] * 3
        + [pltpu.VMEM((N_GROUPS, WIN_BLK, 2 * WIN_BLK), F32), pltpu.VMEM((5, tq, LANES), F32),
           pltpu.VMEM((tq, LANES), F32)],
        compiler_params=pltpu.CompilerParams(
            dimension_semantics=("parallel", "parallel", "parallel"), vmem_limit_bytes=VMEM_LIMIT),
        name="attn_prompt",
    )(*args)


def _attn_sample_kernel(*refs):
    q_refs = refs[0:N_GROUPS]
    kn_refs = refs[N_GROUPS:2 * N_GROUPS]
    vn_refs = refs[2 * N_GROUPS:3 * N_GROUPS]
    kc_refs = refs[3 * N_GROUPS:4 * N_GROUPS]
    vc_refs = refs[4 * N_GROUPS:5 * N_GROUPS]
    bias0_ref, bias_ref, o_ref = refs[5 * N_GROUPS:5 * N_GROUPS + 3]
    m_ref, l_ref, acc_ref = refs[5 * N_GROUPS + 3:]
    t = pl.program_id(0)
    scale = HEAD_DIM ** -0.5

    def logits(gi, k, bias):
        s = jnp.sum(q_refs[gi][...] * k, axis=-1, keepdims=True) * scale + bias
        return jnp.broadcast_to(s, k.shape)

    @pl.when(t == 0)
    def _():
        for gi in range(N_GROUPS):
            s = logits(gi, kn_refs[gi][...], bias0_ref[gi])
            m_ref[gi] = s
            l_ref[gi] = jnp.ones_like(s)
            acc_ref[gi] = vn_refs[gi][...]

    for gi in range(N_GROUPS):
        s = logits(gi, kc_refs[gi][...], bias_ref[gi])
        m_old = m_ref[gi]
        m_new = jnp.maximum(m_old, s)
        alpha = jnp.exp(m_old - m_new)
        p = jnp.exp(s - m_new)
        l_ref[gi] = alpha * l_ref[gi] + p
        acc_ref[gi] = alpha * acc_ref[gi] + p * vc_refs[gi][...]
        m_ref[gi] = m_new

    @pl.when(t == pl.num_programs(0) - 1)
    def _():
        m_all = jnp.maximum(jnp.maximum(m_ref[0], m_ref[1]), m_ref[2])
        num = jnp.zeros(acc_ref.shape[1:], F32)
        den = jnp.zeros(m_ref.shape[1:], F32)
        for gi in range(N_GROUPS):
            wgt = jnp.exp(m_ref[gi] - m_all)
            num = num + wgt * acc_ref[gi]
            den = den + wgt * l_ref[gi]
        o_ref[...] = num / den


def _attn_sample(q_g, kn_g, vn_g, k_caches, v_caches, bias0, bias_steps):
    batch = q_g[0].shape[0]
    small = pl.BlockSpec((batch, DIL_HEADS, HEAD_DIM), lambda t: (0, 0, 0))
    in_specs = [small] * (3 * N_GROUPS)
    for _ in range(2):
        for _, dil in DIL_GROUPS:
            in_specs.append(pl.BlockSpec((None, batch, None, DIL_HEADS, HEAD_DIM),
                                         lambda t, dil=dil: (0, 0, t * dil, 0, 0)))
    in_specs.append(pl.BlockSpec((N_GROUPS, 1, DIL_HEADS, 1), lambda t: (0, 0, 0, 0)))
    in_specs.append(pl.BlockSpec((None, N_GROUPS, 1, DIL_HEADS, 1), lambda t: (t, 0, 0, 0, 0)))
    return pl.pallas_call(
        _attn_sample_kernel,
        grid=(WIN_BLK,),
        in_specs=in_specs,
        out_specs=pl.BlockSpec((batch, DIL_HEADS, HEAD_DIM), lambda t: (0, 0, 0)),
        out_shape=jax.ShapeDtypeStruct((batch, DIL_HEADS, HEAD_DIM), F32),
        scratch_shapes=[
            pltpu.VMEM((N_GROUPS, batch, DIL_HEADS, HEAD_DIM), F32),
            pltpu.VMEM((N_GROUPS, batch, DIL_HEADS, HEAD_DIM), F32),
            pltpu.VMEM((N_GROUPS, batch, DIL_HEADS, HEAD_DIM), F32),
        ],
        compiler_params=pltpu.CompilerParams(
            dimension_semantics=("arbitrary",), vmem_limit_bytes=VMEM_LIMIT),
        name="attn_sample",
    )(*q_g, *kn_g, *vn_g, *k_caches, *v_caches, bias0, bias_steps)


def _output_kernel(ob_ref, zb_ref, gateb_ref, ha_ref, x_ref, wpb_ref, wout_ref, gpost_ref, y_ref):
    yb = (ob_ref[...].astype(F32) * _silu(zb_ref[...])).astype(BF16)
    p_b = jnp.dot(yb, wpb_ref[...], preferred_element_type=F32)
    hid = ha_ref[...].astype(F32) + _sigmoid(gateb_ref[...].astype(F32)) * p_b
    out = jnp.dot(hid.astype(BF16), wout_ref[...], preferred_element_type=F32)
    ms = jnp.mean(out * out, axis=-1, keepdims=True)
    y_ref[...] = x_ref[...] + out * lax.rsqrt(ms + RMS_EPS) * gpost_ref[...]


def _output(o_b, proj, gates, h_a, x2d, w_proj_b, w_out, g_post, tm):
    m = x2d.shape[0]
    return pl.pallas_call(
        _output_kernel,
        grid=(m // tm,),
        in_specs=[
            pl.BlockSpec((tm, DIL_W), lambda i: (i, 0)),
            pl.BlockSpec((tm, DIL_W), lambda i: (i, COL_Z_B // DIL_W)),
            pl.BlockSpec((tm, D_MODEL), lambda i: (i, COL_GATE_B // D_MODEL)),
            pl.BlockSpec((tm, D_MODEL), lambda i: (i, 0)),
            pl.BlockSpec((tm, D_MODEL), lambda i: (i, 0)),
            pl.BlockSpec((DIL_W, D_MODEL), lambda i: (0, 0)),
            pl.BlockSpec((D_MODEL, D_MODEL), lambda i: (0, 0)),
            pl.BlockSpec((1, D_MODEL), lambda i: (0, 0)),
        ],
        out_specs=pl.BlockSpec((tm, D_MODEL), lambda i: (i, 0)),
        out_shape=jax.ShapeDtypeStruct((m, D_MODEL), F32),
        compiler_params=pltpu.CompilerParams(
            dimension_semantics=("parallel",), vmem_limit_bytes=VMEM_LIMIT),
        name="output",
    )(o_b, proj, gates, h_a, x2d, w_proj_b, w_out, g_post)


def _rel_bucket(dist):
    max_exact = REL_BUCKETS // 2
    d = jnp.maximum(dist, 1).astype(F32)
    large = max_exact + (jnp.log(d / max_exact) / math.log(REL_MAX_DIST / max_exact)
                         * (REL_BUCKETS - max_exact)).astype(jnp.int32)
    large = jnp.minimum(large, REL_BUCKETS - 1)
    return jnp.where(dist < max_exact, dist, large)


def _group_biases(rel_table):
    out = []
    for gi, (win, dil) in enumerate(DIL_GROUPS):
        dist = jnp.arange(win // dil + 1, dtype=jnp.int32) * dil
        b = rel_table[_rel_bucket(dist)][:, gi * DIL_HEADS:(gi + 1) * DIL_HEADS]
        out.append(b.T.astype(F32))
    return out


def _bias_tiles(biases):
    period = 3 * WIN_BLK
    tiles = []
    for b in biases:
        u = jnp.concatenate([b[:, ::-1], jnp.full((DIL_HEADS, period - WIN_BLK - 1), NEG, F32)], axis=1)
        flat = jnp.tile(u, (1, WIN_BLK))[:, :WIN_BLK * (period - 1)]
        tiles.append(flat.reshape(DIL_HEADS, WIN_BLK, period - 1)[:, :, :2 * WIN_BLK])
    return jnp.stack(tiles, axis=0).astype(F32)


WEIGHT_PREP_ROWS = 128


def _weight_prep_kernel(w_ref, main_ref, ab_ref):
    dst = 0
    for lo, hi in ((SRC_QKV_A, SRC_Z_A), (SRC_QKV_B, SRC_Z_B), (SRC_Z_B, SRC_GATE_A),
                   (SRC_Z_A, SRC_AB), (SRC_GATE_A, SRC_GATE_B), (SRC_GATE_B, SRC_END)):
        main_ref[:, dst:dst + hi - lo] = w_ref[:, lo:hi].astype(BF16)
        dst += hi - lo
    ab = w_ref[:, SRC_AB:SRC_QKV_B].astype(BF16)
    ab_ref[...] = jnp.concatenate([ab, jnp.zeros((ab.shape[0], LANES - ab.shape[1]), BF16)], axis=1)


def _weight_prep(w_in):
    rows = WEIGHT_PREP_ROWS
    return pl.pallas_call(
        _weight_prep_kernel,
        grid=(D_MODEL // rows,),
        in_specs=[pl.BlockSpec((rows, SRC_END), lambda i: (i, 0))],
        out_specs=[pl.BlockSpec((rows, MAIN_WIDTH), lambda i: (i, 0)),
                   pl.BlockSpec((rows, LANES), lambda i: (i, 0))],
        out_shape=[jax.ShapeDtypeStruct((D_MODEL, MAIN_WIDTH), BF16),
                   jax.ShapeDtypeStruct((D_MODEL, LANES), BF16)],
        compiler_params=pltpu.CompilerParams(
            dimension_semantics=("parallel",), vmem_limit_bytes=VMEM_LIMIT),
        name="weight_prep",
    )(w_in)


def _layer_weights(w_in, conv_w, a_log, dt_bias, g_head_norm, w_proj_a, w_proj_b, w_out):
    w_main, w_ab = _weight_prep(w_in)
    pad = (0, LANES - GDN_HEADS)
    return dict(
        w_main=w_main, w_ab=w_ab, conv_wt=conv_w.T.astype(F32),
        alog_pad=jnp.pad(a_log.astype(F32), pad)[None, :],
        dtb_pad=jnp.pad(dt_bias.astype(F32), pad)[None, :],
        ghn=g_head_norm.astype(F32)[None, :],
        w_proj_a=w_proj_a.astype(BF16), w_proj_b=w_proj_b.astype(BF16), w_out=w_out.astype(BF16))


def _window_rows_kernel(k_ref, v_ref, ko_ref, vo_ref):
    rows = k_ref.shape[0]
    for src, dst in ((k_ref, ko_ref), (v_ref, vo_ref)):
        for h in range(DIL_HEADS):
            dst[pl.ds(h, rows, stride=DIL_HEADS), :] = src[:, h * HEAD_DIM:(h + 1) * HEAD_DIM]


def _window_rows(proj3, col_k, keep):
    batch, seq, _ = proj3.shape
    rows = min(keep, 1024)
    first = (seq - keep) // rows
    out = jax.ShapeDtypeStruct((batch, keep * DIL_HEADS, HEAD_DIM), F32)
    k_out, v_out = pl.pallas_call(
        _window_rows_kernel,
        grid=(batch, keep // rows),
        in_specs=[
            pl.BlockSpec((None, rows, DIL_W), lambda b, r: (b, first + r, col_k // DIL_W)),
            pl.BlockSpec((None, rows, DIL_W), lambda b, r: (b, first + r, col_k // DIL_W + 1)),
        ],
        out_specs=[
            pl.BlockSpec((None, rows * DIL_HEADS, HEAD_DIM), lambda b, r: (b, r, 0)),
            pl.BlockSpec((None, rows * DIL_HEADS, HEAD_DIM), lambda b, r: (b, r, 0)),
        ],
        out_shape=[out, out],
        compiler_params=pltpu.CompilerParams(
            dimension_semantics=("parallel", "parallel"), vmem_limit_bytes=VMEM_LIMIT),
        name="window_rows",
    )(proj3, proj3)
    shape = (batch, keep, DIL_HEADS, HEAD_DIM)
    return k_out.reshape(shape), v_out.reshape(shape)


def kernel(x_prompt, x_sample, state_gdn, state_conv, cache_k_w128, cache_v_w128, cache_k_w512,
           cache_v_w512, cache_k_w2048, cache_v_w2048, rel_table, g_pre, w_in, conv_w, a_log, dt_bias,
           g_head_norm, w_proj_a, w_proj_b, w_out, g_post):
    batch, seq, _ = x_prompt.shape
    dec_batch = x_sample.shape[0]
    k_caches = (cache_k_w128, cache_k_w512, cache_k_w2048)
    v_caches = (cache_v_w128, cache_v_w512, cache_v_w2048)
    biases = _group_biases(rel_table)
    bias_tiles = _bias_tiles(biases) * LOG2_E
    lw = _layer_weights(w_in[0], conv_w[0], a_log[0], dt_bias[0], g_head_norm[0], w_proj_a[0],
                        w_proj_b[0], w_out[0])
    g_pre_l = g_pre[0][None, :]
    g_post_l = g_post[0][None, :]

    xp = x_prompt.reshape(batch * seq, D_MODEL)
    proj, gates, ab = _in_projection(xp, g_pre_l, lw["w_main"], lw["w_ab"], tm=2048, tn=1024)
    h_a, p_gdn = _gdn_prompt(proj, gates, ab, lw["conv_wt"], lw["alog_pad"], lw["dtb_pad"], lw["ghn"],
                             lw["w_proj_a"], batch, seq)
    o_b = _attn_prompt(proj, bias_tiles, batch, seq)
    y_prompt = _output(o_b, proj, gates, h_a, xp, lw["w_proj_b"], lw["w_out"], g_post_l, tm=1024)
    y_prompt = y_prompt.reshape(batch, seq, D_MODEL)
    proj3 = proj.reshape(batch, seq, WIDE_WIDTH)
    p_conv = proj3[:, seq - (CONV_WIDTH - 1):, :CONV_DIM]
    p_rows = []
    for gi, (win, _) in enumerate(DIL_GROUPS):
        keep = min(win, seq)
        ck = COL_QKV_B + gi * 3 * DIL_W + DIL_W
        p_rows += list(_window_rows(proj3, ck, keep))

    xs = x_sample.reshape(dec_batch, D_MODEL)
    proj_s, gates_s, ab_s = _in_projection(xs, g_pre_l, lw["w_main"], lw["w_ab"], tm=dec_batch, tn=1024)
    sc_t = jnp.transpose(state_conv[0], (1, 0, 2))
    o_s, s_gdn = _gdn_sample(sc_t, proj_s, ab_s, lw["conv_wt"], lw["alog_pad"], lw["dtb_pad"],
                             state_gdn[0], nb=32)
    h_a_s = _gdn_sample_out(o_s, gates_s, lw["ghn"], lw["w_proj_a"])
    q_g, kn_g, vn_g = [], [], []
    for gi in range(N_GROUPS):
        cq = COL_QKV_B + gi * 3 * DIL_W
        q_g.append(proj_s[:, cq:cq + DIL_W].reshape(dec_batch, DIL_HEADS, HEAD_DIM))
        kn_g.append(proj_s[:, cq + DIL_W:cq + 2 * DIL_W].reshape(dec_batch, DIL_HEADS, HEAD_DIM))
        vn_g.append(proj_s[:, cq + 2 * DIL_W:cq + 3 * DIL_W].reshape(dec_batch, DIL_HEADS, HEAD_DIM))
    bias_all = jnp.stack(biases, axis=0)
    bias0 = bias_all[:, None, :, 0:1]
    bias_steps = jnp.transpose(bias_all[:, :, :0:-1], (2, 0, 1))[:, :, None, :, None]
    o_b_s = _attn_sample(q_g, kn_g, vn_g, k_caches, v_caches, bias0, bias_steps)
    y_sample = _output(o_b_s.reshape(dec_batch, DIL_W), proj_s, gates_s, h_a_s, xs, lw["w_proj_b"],
                       lw["w_out"], g_post_l, tm=dec_batch)
    y_sample = y_sample.reshape(dec_batch, 1, D_MODEL)
    s_conv = jnp.concatenate([state_conv[0][:, 1:, :], proj_s[:, None, :CONV_DIM]], axis=1)
    s_rows = []
    for gi in range(N_GROUPS):
        s_rows += [kn_g[gi][:, None], vn_g[gi][:, None]]

    return (y_prompt, y_sample, p_gdn[None], p_conv[None], *[r[None] for r in p_rows],
            s_gdn[None], s_conv[None], *[r[None] for r in s_rows])
```

```python
import functools
import math

import jax
import jax.numpy as jnp
from jax import lax
from jax.experimental import pallas as pl
from jax.experimental.pallas import tpu as pltpu

F32 = jnp.float32
BF16 = jnp.bfloat16

D_MODEL = 1024
GDN_HEADS = 8
HEAD_DIM = 128
GDN_QK = GDN_HEADS * HEAD_DIM
CONV_DIM = 3 * GDN_QK
CONV_WIDTH = 4
DIL_GROUPS = ((128, 1), (512, 4), (2048, 16))
N_GROUPS = 3
DIL_HEADS = 4
DIL_W = DIL_HEADS * HEAD_DIM
WIN_BLK = 128
REL_BUCKETS = 32
REL_MAX_DIST = 2048
RMS_EPS = 1e-6
NEG = -1e30
LOG2_E = math.log2(math.e)

COL_QKV_A = 0
COL_QKV_B = 3072
COL_Z_B = 7680
WIDE_WIDTH = 8192
COL_Z_A = 0
COL_GATE_A = 1024
COL_GATE_B = 2048
GATE_WIDTH = 3072
MAIN_WIDTH = WIDE_WIDTH + GATE_WIDTH
SRC_QKV_A, SRC_Z_A, SRC_AB, SRC_QKV_B, SRC_Z_B, SRC_GATE_A, SRC_GATE_B, SRC_END = (
    0, 3072, 4096, 4112, 8720, 9232, 10256, 11280)
LANES = 128

VMEM_LIMIT = 56 * 1024 * 1024
INPROJ_VMEM_LIMIT = 60 * 1024 * 1024

INPROJ_ROWS = 2048
INPROJ_COLS = 1024
OUTPUT_ROWS = 1024
GDN_CHUNK = 64
GDN_SEQS = 4
GDN_SAMPLE_SEQS = 32
ATTN_TILE = 2048
ATTN_UNITS = 16
MAX_ROW_STRIDE = 4


def _silu(x):
    half = 0.5 * x
    return half * jnp.tanh(half) + half


def _sigmoid(x):
    return 1.0 / (1.0 + jnp.exp(-x))


def _softplus(x):
    return jnp.maximum(x, 0.0) + jnp.log(1.0 + jnp.exp(-jnp.abs(x)))


def _dot(a, b):
    return jnp.dot(a.astype(BF16), b.astype(BF16), preferred_element_type=F32)


def _dot_tn(a, b):
    return lax.dot_general(a.astype(BF16), b.astype(BF16), (((0,), (0,)), ((), ())),
                           preferred_element_type=F32)


def _inproj_kernel(x_ref, g_ref, w_ref, wab_ref, wide_ref, gate_ref, ab_ref, xn_ref, *, wide_tiles):
    j = pl.program_id(1)

    @pl.when(j == 0)
    def _():
        x = x_ref[...]
        ms = jnp.mean(x * x, axis=-1, keepdims=True)
        xn = (x * lax.rsqrt(ms + RMS_EPS) * g_ref[...]).astype(BF16)
        xn_ref[...] = xn
        ab_ref[...] = jnp.dot(xn, wab_ref[...], preferred_element_type=F32)

    @pl.when(j < wide_tiles)
    def _():
        wide_ref[...] = jnp.dot(xn_ref[...], w_ref[...], preferred_element_type=F32)

    @pl.when(j >= wide_tiles)
    def _():
        gate_ref[...] = jnp.dot(xn_ref[...], w_ref[...], preferred_element_type=F32).astype(gate_ref.dtype)


def _in_projection(x2d, g_pre, w_main, w_ab, tm, tn):
    m = x2d.shape[0]
    wide_tiles = WIDE_WIDTH // tn
    return pl.pallas_call(
        functools.partial(_inproj_kernel, wide_tiles=wide_tiles),
        grid=(m // tm, MAIN_WIDTH // tn),
        in_specs=[
            pl.BlockSpec((tm, D_MODEL), lambda i, j: (i, 0)),
            pl.BlockSpec((1, D_MODEL), lambda i, j: (0, 0)),
            pl.BlockSpec((D_MODEL, tn), lambda i, j: (0, j)),
            pl.BlockSpec((D_MODEL, LANES), lambda i, j: (0, 0)),
        ],
        out_specs=[
            pl.BlockSpec((tm, tn), lambda i, j: (i, jnp.minimum(j, wide_tiles - 1))),
            pl.BlockSpec((tm, tn), lambda i, j: (i, jnp.maximum(j - wide_tiles, 0))),
            pl.BlockSpec((tm, LANES), lambda i, j: (i, 0)),
        ],
        out_shape=[
            jax.ShapeDtypeStruct((m, WIDE_WIDTH), F32),
            jax.ShapeDtypeStruct((m, GATE_WIDTH), BF16),
            jax.ShapeDtypeStruct((m, LANES), F32),
        ],
        scratch_shapes=[pltpu.VMEM((tm, D_MODEL), BF16)],
        compiler_params=pltpu.CompilerParams(
            dimension_semantics=("parallel", "arbitrary"), vmem_limit_bytes=INPROJ_VMEM_LIMIT),
        name="in_projection",
    )(x2d, g_pre, w_main, w_ab)


GROUP_HEADS = 4


def _block_diag_rhs(x4, half_masks):
    c = x4.shape[0]
    zeros = jnp.zeros((c, LANES), x4.dtype)
    row_blocks = []
    for h in range(GROUP_HEADS):
        piece = x4[:, (h // 2) * LANES:(h // 2 + 1) * LANES] * half_masks[h % 2]
        row_blocks.append(jnp.concatenate([piece if v == h // 2 else zeros for v in range(2)], axis=1))
    return jnp.concatenate(row_blocks, axis=0)


def _gdn_prompt_kernel(qkv_ref, z_ref, ab_ref, gate_ref, convw_ref, alog_ref, dtb_ref, ghn_ref,
                       wproj_ref, ha_ref, sfin_ref, xpad_ref, qkvn_ref, o_ref, s_ref,
                       u_ref, w_ref, qd_ref, kd_ref, qk_ref):
    c = GDN_CHUNK
    tl = GDN_SEQS * c
    gw = GROUP_HEADS * c
    step = pl.program_id(1)

    def conv_stage(i):
        for blk in range(CONV_DIM // LANES):
            cols = slice(blk * LANES, (blk + 1) * LANES)
            xpad_ref[i, blk, 8:8 + c, :] = qkv_ref[i, :, cols]
            for phase in range(2):
                acc = None
                for w in range(CONV_WIDTH):
                    tap = xpad_ref[i, blk, pl.ds(5 + phase + w, c // 2, stride=2), :] * convw_ref[w:w + 1, cols]
                    acc = tap if acc is None else acc + tap
                y = _silu(acc)
                if blk < 2 * GDN_HEADS:
                    y = y * lax.rsqrt(jnp.sum(y * y, axis=-1, keepdims=True) + RMS_EPS)
                    if blk < GDN_HEADS:
                        y = y * (HEAD_DIM ** -0.5)
                qkvn_ref[blk, pl.ds(i * c + phase, c // 2, stride=2), :] = y
            xpad_ref[i, blk, 5:8, :] = xpad_ref[i, blk, c + 5:c + 8, :]

    @pl.when(step == 0)
    def _():
        xpad_ref[:, :, 0:8, :] = jnp.zeros((GDN_SEQS, CONV_DIM // LANES, 8, LANES), F32)
        s_ref[...] = jnp.zeros_like(s_ref)

    ab = ab_ref[...].reshape(tl, LANES)
    g_full = -jnp.exp(alog_ref[...]) * _softplus(ab + dtb_ref[...])
    beta_full = _sigmoid(ab)
    row = lax.broadcasted_iota(jnp.int32, (tl, tl), 0)
    col = lax.broadcasted_iota(jnp.int32, (tl, tl), 1)
    same_chunk = (row // c) == (col // c)
    tri = jnp.where(same_chunk & (col <= row), 1.0, 0.0).astype(BF16)
    g_hi = g_full.astype(BF16)
    g_rest = g_full - g_hi.astype(F32)
    g_mid = g_rest.astype(BF16)
    g_lo = (g_rest - g_mid.astype(F32)).astype(BF16)
    g_cum = jnp.dot(jnp.concatenate([tri, tri, tri], axis=1), jnp.concatenate([g_hi, g_mid, g_lo], axis=0),
                    preferred_element_type=F32)
    g_cum_t = g_cum.T
    g_last = jnp.concatenate(
        [jnp.broadcast_to(g_cum[i * c + c - 1:(i + 1) * c, :], (c, LANES)) for i in range(GDN_SEQS)], axis=0)
    exp_g = jnp.exp(g_cum)
    exp_rest = jnp.exp(g_last - g_cum)
    exp_last = jnp.exp(g_last)

    ci = lax.broadcasted_iota(jnp.int32, (c, gw), 0)
    cj = lax.broadcasted_iota(jnp.int32, (c, gw), 1) % c
    causal4 = cj <= ci
    strict4 = cj < ci
    half_lane = lax.broadcasted_iota(jnp.int32, (c, LANES), 1) // c
    half_masks = [jnp.where(half_lane == k, 1.0, 0.0).astype(BF16) for k in range(2)]

    def lanes_of(col_values, width):
        return jnp.concatenate([jnp.broadcast_to(v, (c, width)) for v in col_values], axis=1)

    n_groups = GDN_HEADS // GROUP_HEADS
    zero_s = jnp.zeros((HEAD_DIM, HEAD_DIM), BF16)
    zero_v = jnp.zeros((c, HEAD_DIM), BF16)

    def group_cols(hg):
        return slice(hg * GROUP_HEADS * LANES, (hg + 1) * GROUP_HEADS * LANES)

    def bd_dot(lhs, x4):
        return jnp.dot(lhs.astype(BF16), _block_diag_rhs(x4.astype(BF16), half_masks),
                       preferred_element_type=F32)

    def seq_stages(i):
        rows = slice(i * c, (i + 1) * c)
        st = {}

        def build():
            a4s = []
            for hg in range(n_groups):
                heads = range(hg * GROUP_HEADS, (hg + 1) * GROUP_HEADS)
                cols = group_cols(hg)

                def head_blocks(first_blk):
                    return jnp.concatenate([qkvn_ref[first_blk + h, rows, :] for h in heads], axis=1)

                q4 = head_blocks(0)
                k4 = head_blocks(GDN_HEADS)
                v4 = head_blocks(2 * GDN_HEADS)
                beta_e = lanes_of([beta_full[rows, GDN_HEADS + h:GDN_HEADS + h + 1] for h in heads], LANES)
                eg_e = lanes_of([exp_g[rows, h:h + 1] for h in heads], LANES)
                er_e = lanes_of([exp_rest[rows, h:h + 1] for h in heads], LANES)
                gi4 = lanes_of([g_cum[rows, h:h + 1] for h in heads], c)
                gj4 = jnp.concatenate([g_cum_t[h:h + 1, rows] for h in heads], axis=1)
                decay4 = jnp.exp(jnp.where(causal4, gi4 - gj4, NEG))
                kb4 = k4 * beta_e
                k4b = k4.astype(BF16)
                k_rhs = jnp.concatenate(
                    [jnp.concatenate([k4b[:, g * LANES:(g + 1) * LANES] if g == h else zero_v
                                      for g in range(GROUP_HEADS)], axis=1)
                     for h in range(GROUP_HEADS)], axis=0)
                kkqk = lax.dot_general(jnp.concatenate([kb4, q4], axis=0).astype(BF16), k_rhs,
                                       (((1,), (1,)), ((), ())), preferred_element_type=F32)
                a4s.append(jnp.where(strict4, kkqk[:c] * decay4, 0.0))
                qk_ref[rows, hg * gw:(hg + 1) * gw] = kkqk[c:] * decay4
                u_ref[rows, cols] = v4 * beta_e
                w_ref[rows, cols] = kb4 * eg_e
                qd_ref[rows, cols] = q4 * eg_e
                kd_ref[rows, cols] = k4 * er_e
            st["a"] = a4s

        def square():
            st["e"] = [-a4 for a4 in st["a"]]
            st["x"] = [bd_dot(a4, a4) for a4 in st["a"]]

        def double():
            boths = [bd_dot(jnp.concatenate([e, x], axis=0), x) for e, x in zip(st["e"], st["x"])]
            st["e"] = [e + x + both[:c] for e, x, both in zip(st["e"], st["x"], boths)]
            st["x"] = [both[c:] for both in boths]

        def last_factor():
            st["e"] = [e + x + bd_dot(e, x) for e, x in zip(st["e"], st["x"])]

        def solve():
            for hg, e4 in enumerate(st["e"]):
                cols = group_cols(hg)
                vb4 = u_ref[rows, cols]
                kbe4 = w_ref[rows, cols]
                us, ws = [], []
                for k in range(GROUP_HEADS):
                    hc = slice(k * LANES, (k + 1) * LANES)
                    rhs = jnp.concatenate([vb4[:, hc], kbe4[:, hc]], axis=1)
                    uw = rhs + _dot(e4[:, k * c:(k + 1) * c], rhs)
                    us.append(uw[:, :LANES])
                    ws.append(uw[:, LANES:])
                u_ref[rows, cols] = jnp.concatenate(us, axis=1)
                w_ref[rows, cols] = jnp.concatenate(ws, axis=1)

        def pair_cols(p):
            return slice(2 * p * LANES, (2 * p + 2) * LANES)

        def state_products():
            wsqs = []
            for p in range(GDN_HEADS // 2):
                cols = pair_cols(p)
                s_bd = jnp.concatenate(
                    [jnp.concatenate([s_ref[i, 2 * p].astype(BF16), zero_s], axis=1),
                     jnp.concatenate([zero_s, s_ref[i, 2 * p + 1].astype(BF16)], axis=1)], axis=0)
                wq = jnp.concatenate([w_ref[rows, cols], qd_ref[rows, cols]], axis=0).astype(BF16)
                wsqs.append(jnp.dot(wq, s_bd, preferred_element_type=F32))
            st["wsq"] = wsqs

        def outputs():
            vnbs = []
            for p, wsq in enumerate(st["wsq"]):
                cols = pair_cols(p)
                vnb = (u_ref[rows, cols] - wsq[:c]).astype(BF16)
                vnbs.append(vnb)
                vn_bd = jnp.concatenate([jnp.concatenate([vnb[:, :LANES], zero_v], axis=1),
                                         jnp.concatenate([zero_v, vnb[:, LANES:]], axis=1)], axis=0)
                qk2 = qk_ref[rows, 2 * p * c:(2 * p + 2) * c]
                o_ref[rows, cols] = wsq[c:] + jnp.dot(qk2.astype(BF16), vn_bd, preferred_element_type=F32)
            st["vn"] = vnbs

        def state_update():
            for p, vnb in enumerate(st["vn"]):
                kd2 = kd_ref[rows, pair_cols(p)]
                for k in range(2):
                    h = 2 * p + k
                    hc = slice(k * LANES, (k + 1) * LANES)
                    el = exp_last[i * c:i * c + 1, h:h + 1]
                    s_ref[i, h] = s_ref[i, h] * el + _dot_tn(kd2[:, hc], vnb[:, hc])

        doublings = int(math.log2(c)) - 2
        return ([functools.partial(conv_stage, i), build, square] + [double] * doublings
                + [last_factor, solve, state_products, outputs, state_update])

    for stages in zip(*[seq_stages(i) for i in range(GDN_SEQS)]):
        for stage in stages:
            stage()

    for h in range(GDN_HEADS):
        cols = slice(h * LANES, (h + 1) * LANES)
        o = o_ref[:, cols]
        o = o * lax.rsqrt(jnp.mean(o * o, axis=-1, keepdims=True) + RMS_EPS) * ghn_ref[...]
        o_ref[:, cols] = o * _silu(z_ref[:, :, cols].astype(F32).reshape(tl, LANES))
    p_a = jnp.dot(o_ref[...].astype(BF16), wproj_ref[...], preferred_element_type=F32)
    h_a = _sigmoid(gate_ref[...].astype(F32).reshape(tl, D_MODEL)) * p_a
    ha_ref[...] = h_a.reshape(GDN_SEQS, c, D_MODEL).astype(ha_ref.dtype)

    @pl.when(step == pl.num_programs(1) - 1)
    def _():
        sfin_ref[...] = s_ref[...]


def _gdn_prompt(proj, gates, ab, conv_wt, alog_pad, dtb_pad, ghn, w_proj_a, batch, seq):
    c = GDN_CHUNK
    ns = GDN_SEQS
    tl = ns * c
    proj3 = proj.reshape(batch, seq, WIDE_WIDTH)
    gates3 = gates.reshape(batch, seq, GATE_WIDTH)
    ab3 = ab.reshape(batch, seq, LANES)
    h_a, s_fin = pl.pallas_call(
        _gdn_prompt_kernel,
        grid=(batch // ns, seq // c),
        in_specs=[
            pl.BlockSpec((ns, c, CONV_DIM), lambda b, l: (b, l, COL_QKV_A // CONV_DIM)),
            pl.BlockSpec((ns, c, GDN_QK), lambda b, l: (b, l, COL_Z_A // GDN_QK)),
            pl.BlockSpec((ns, c, LANES), lambda b, l: (b, l, 0)),
            pl.BlockSpec((ns, c, D_MODEL), lambda b, l: (b, l, COL_GATE_A // D_MODEL)),
            pl.BlockSpec((CONV_WIDTH, CONV_DIM), lambda b, l: (0, 0)),
            pl.BlockSpec((1, LANES), lambda b, l: (0, 0)),
            pl.BlockSpec((1, LANES), lambda b, l: (0, 0)),
            pl.BlockSpec((1, LANES), lambda b, l: (0, 0)),
            pl.BlockSpec((GDN_QK, D_MODEL), lambda b, l: (0, 0)),
        ],
        out_specs=[
            pl.BlockSpec((ns, c, D_MODEL), lambda b, l: (b, l, 0)),
            pl.BlockSpec((ns, GDN_HEADS, HEAD_DIM, HEAD_DIM), lambda b, l: (b, 0, 0, 0)),
        ],
        out_shape=[
            jax.ShapeDtypeStruct((batch, seq, D_MODEL), BF16),
            jax.ShapeDtypeStruct((batch, GDN_HEADS, HEAD_DIM, HEAD_DIM), F32),
        ],
        scratch_shapes=[
            pltpu.VMEM((ns, CONV_DIM // LANES, c + 8, LANES), F32),
            pltpu.VMEM((CONV_DIM // LANES, tl, LANES), F32),
            pltpu.VMEM((tl, GDN_QK), F32),
            pltpu.VMEM((ns, GDN_HEADS, HEAD_DIM, HEAD_DIM), F32),
            pltpu.VMEM((tl, GDN_QK), F32),
            pltpu.VMEM((tl, GDN_QK), F32),
            pltpu.VMEM((tl, GDN_QK), F32),
            pltpu.VMEM((tl, GDN_QK), F32),
            pltpu.VMEM((tl, GDN_HEADS * GDN_CHUNK), F32),
        ],
        compiler_params=pltpu.CompilerParams(
            dimension_semantics=("parallel", "arbitrary"), vmem_limit_bytes=VMEM_LIMIT),
        name="gdn_prompt",
    )(proj3, gates3, ab3, gates3, conv_wt, alog_pad, dtb_pad, ghn, w_proj_a)
    return h_a.reshape(batch * seq, D_MODEL), s_fin


def _gdn_sample_kernel(sc_q_ref, sc_k_ref, sc_v_ref, nq_ref, nk_ref, nv_ref, cwq_ref, cwk_ref, cwv_ref,
                       ab_ref, alog_ref, dtb_ref, s_ref, o_ref, snew_ref):
    h = pl.program_id(0)
    nb = nq_ref.shape[0]

    def conv_silu(sc_ref, new_ref, cw_ref):
        acc = new_ref[...] * cw_ref[3:4, :]
        for w in range(CONV_WIDTH - 1):
            acc = acc + sc_ref[w] * cw_ref[w:w + 1, :]
        return _silu(acc)

    q = conv_silu(sc_q_ref, nq_ref, cwq_ref)
    k = conv_silu(sc_k_ref, nk_ref, cwk_ref)
    v = conv_silu(sc_v_ref, nv_ref, cwv_ref)
    q = q * lax.rsqrt(jnp.sum(q * q, axis=-1, keepdims=True) + RMS_EPS) * (HEAD_DIM ** -0.5)
    k = k * lax.rsqrt(jnp.sum(k * k, axis=-1, keepdims=True) + RMS_EPS)

    ab = ab_ref[...]
    lane = lax.broadcasted_iota(jnp.int32, ab.shape, 1)
    g_full = -jnp.exp(alog_ref[...]) * _softplus(ab + dtb_ref[...])
    g = jnp.sum(jnp.where(lane == h, g_full, 0.0), axis=-1, keepdims=True)
    beta = jnp.sum(jnp.where(lane == h + GDN_HEADS, _sigmoid(ab), 0.0), axis=-1, keepdims=True)
    eg = jnp.exp(g)

    eye = (lax.broadcasted_iota(jnp.int32, (HEAD_DIM, HEAD_DIM), 0)
           == lax.broadcasted_iota(jnp.int32, (HEAD_DIM, HEAD_DIM), 1))

    def as_column(row):
        return jnp.sum(jnp.where(eye, row, 0.0), axis=1, keepdims=True)

    for b in range(nb):
        s = s_ref[b]
        k_col = as_column(k[b:b + 1, :])
        q_col = as_column(q[b:b + 1, :])
        e = eg[b:b + 1, :]
        ks = jnp.sum(k_col * s, axis=0, keepdims=True)
        v_new = beta[b:b + 1, :] * (v[b:b + 1, :] - e * ks)
        s_new = s * e + k_col * v_new
        snew_ref[b] = s_new
        o_ref[b:b + 1, :] = jnp.sum(q_col * s_new, axis=0, keepdims=True)


def _gdn_sample(state_conv_t, proj, ab, conv_wt, alog_pad, dtb_pad, state, nb):
    batch = proj.shape[0]
    hq = GDN_QK // LANES
    return pl.pallas_call(
        _gdn_sample_kernel,
        grid=(GDN_HEADS, batch // nb),
        in_specs=[
            pl.BlockSpec((CONV_WIDTH - 1, nb, LANES), lambda h, i: (0, i, h)),
            pl.BlockSpec((CONV_WIDTH - 1, nb, LANES), lambda h, i: (0, i, hq + h)),
            pl.BlockSpec((CONV_WIDTH - 1, nb, LANES), lambda h, i: (0, i, 2 * hq + h)),
            pl.BlockSpec((nb, LANES), lambda h, i: (i, h)),
            pl.BlockSpec((nb, LANES), lambda h, i: (i, hq + h)),
            pl.BlockSpec((nb, LANES), lambda h, i: (i, 2 * hq + h)),
            pl.BlockSpec((CONV_WIDTH, LANES), lambda h, i: (0, h)),
            pl.BlockSpec((CONV_WIDTH, LANES), lambda h, i: (0, hq + h)),
            pl.BlockSpec((CONV_WIDTH, LANES), lambda h, i: (0, 2 * hq + h)),
            pl.BlockSpec((nb, LANES), lambda h, i: (i, 0)),
            pl.BlockSpec((1, LANES), lambda h, i: (0, 0)),
            pl.BlockSpec((1, LANES), lambda h, i: (0, 0)),
            pl.BlockSpec((nb, None, HEAD_DIM, HEAD_DIM), lambda h, i: (i, h, 0, 0)),
        ],
        out_specs=[
            pl.BlockSpec((nb, LANES), lambda h, i: (i, h)),
            pl.BlockSpec((nb, None, HEAD_DIM, HEAD_DIM), lambda h, i: (i, h, 0, 0)),
        ],
        out_shape=[
            jax.ShapeDtypeStruct((batch, GDN_QK), F32),
            jax.ShapeDtypeStruct(state.shape, F32),
        ],
        compiler_params=pltpu.CompilerParams(
            dimension_semantics=("parallel", "parallel"), vmem_limit_bytes=VMEM_LIMIT),
        name="gdn_sample",
    )(state_conv_t, state_conv_t, state_conv_t, proj, proj, proj, conv_wt, conv_wt, conv_wt,
      ab, alog_pad, dtb_pad, state)


def _gdn_sample_out_kernel(o_ref, z_ref, gate_ref, ghn_ref, wproj_ref, ha_ref, y_ref):
    for h in range(GDN_HEADS):
        cols = slice(h * LANES, (h + 1) * LANES)
        o = o_ref[:, cols]
        o = o * lax.rsqrt(jnp.mean(o * o, axis=-1, keepdims=True) + RMS_EPS) * ghn_ref[...]
        y_ref[:, cols] = o * _silu(z_ref[:, cols].astype(F32))
    p_a = jnp.dot(y_ref[...].astype(BF16), wproj_ref[...], preferred_element_type=F32)
    ha_ref[...] = (_sigmoid(gate_ref[...].astype(F32)) * p_a).astype(ha_ref.dtype)


def _gdn_sample_out(o, gates, ghn, w_proj_a):
    batch = o.shape[0]
    return pl.pallas_call(
        _gdn_sample_out_kernel,
        grid=(1,),
        in_specs=[
            pl.BlockSpec((batch, GDN_QK), lambda i: (0, 0)),
            pl.BlockSpec((batch, GDN_QK), lambda i: (0, COL_Z_A // GDN_QK)),
            pl.BlockSpec((batch, D_MODEL), lambda i: (0, COL_GATE_A // D_MODEL)),
            pl.BlockSpec((1, LANES), lambda i: (0, 0)),
            pl.BlockSpec((GDN_QK, D_MODEL), lambda i: (0, 0)),
        ],
        out_specs=pl.BlockSpec((batch, D_MODEL), lambda i: (0, 0)),
        out_shape=jax.ShapeDtypeStruct((batch, D_MODEL), BF16),
        scratch_shapes=[pltpu.VMEM((batch, GDN_QK), F32)],
        compiler_params=pltpu.CompilerParams(vmem_limit_bytes=VMEM_LIMIT),
        name="gdn_sample_out",
    )(o, gates, gates, ghn, w_proj_a)


def _attn_prompt_kernel(*refs):
    ins = refs[:5 * N_GROUPS]
    bias_ref = refs[5 * N_GROUPS]
    o_ref = refs[5 * N_GROUPS + 1]
    acc_ref, m_ref, l_ref, bias_start_ref, split_ref, merged_ref = refs[5 * N_GROUPS + 2:]
    tq = ATTN_TILE
    first_tile = pl.program_id(1) == 0
    scale = (HEAD_DIM ** -0.5) * LOG2_E
    lane2 = lax.broadcasted_iota(jnp.int32, (1, 2 * WIN_BLK), 1)
    start_mask = jnp.where(first_tile & (lane2 < WIN_BLK), NEG, 0.0).astype(F32)
    for gi in range(N_GROUPS):
        bias_start_ref[gi] = bias_ref[gi] + start_mask

    def strided(start, n, stride):
        return pl.ds(start, n, stride=stride) if stride > 1 else pl.ds(start, n)

    for gi, (_, dil) in enumerate(DIL_GROUPS):
        srcs = ins[5 * gi:5 * gi + 5]
        n_blocks = tq // (dil * WIN_BLK)
        inner = min(dil, MAX_ROW_STRIDE)
        outer = dil // inner
        if outer > 1:
            for a, src in enumerate(srcs):
                n = src.shape[0]
                for p in range(outer):
                    split_ref[a, p * (n // outer):(p + 1) * (n // outer), :] = src[strided(p, n // outer, outer), :]
            srcs = [split_ref.at[a] for a in range(5)]
        q_ref, kp_ref, kc_ref, vp_ref, vc_ref = srcs

        def rows_of(ref_rows, r, first, count, inner=inner, outer=outer):
            if outer == 1:
                return strided(r + inner * first, count, inner)
            part = r % outer
            return strided(part * (ref_rows // outer) + r // outer + inner * first, count, inner)

        def run_units(units, q_ref=q_ref, kp_ref=kp_ref, kc_ref=kc_ref, vp_ref=vp_ref, vc_ref=vc_ref,
                      gi=gi, rows_of=rows_of):
            span = kp_ref.shape[0]
            n = len(units)
            st = [{} for _ in units]

            def logits(u):
                r, j = units[u]
                q = (q_ref[rows_of(tq, r, WIN_BLK * j, WIN_BLK), :] * scale).astype(BF16)
                if j == 0:
                    prev = rows_of(span, r, 0, WIN_BLK)
                    cur = rows_of(tq, r, 0, WIN_BLK)
                    kk = jnp.concatenate([kp_ref[prev, :], kc_ref[cur, :]], axis=0).astype(BF16)
                    st[u]["v"] = (vp_ref, prev, vc_ref, cur)
                else:
                    rows = rows_of(tq, r, WIN_BLK * (j - 1), 2 * WIN_BLK)
                    kk = kc_ref[rows, :].astype(BF16)
                    st[u]["v"] = (vc_ref, rows)
                st[u]["s"] = lax.dot_general(q, kk, (((1,), (1,)), ((), ())), preferred_element_type=F32)

            def softmax(u):
                r, j = units[u]
                s = st[u].pop("s") + (bias_start_ref[gi] if j == 0 else bias_ref[gi])
                m = jnp.max(s, axis=-1, keepdims=True)
                p = jnp.exp2(s - m)
                st[u]["m"] = m
                st[u]["l"] = jnp.sum(p, axis=-1, keepdims=True)
                st[u]["p"] = p.astype(BF16)

            def values(u):
                r, j = units[u]
                v = st[u].pop("v")
                if len(v) == 4:
                    vv = jnp.concatenate([v[0][v[1], :], v[2][v[3], :]], axis=0).astype(BF16)
                else:
                    vv = v[0][v[1], :].astype(BF16)
                acc = jnp.dot(st[u].pop("p"), vv, preferred_element_type=F32)
                q_rows = rows_of(tq, r, WIN_BLK * j, WIN_BLK)
                acc_ref[gi, q_rows, :] = acc
                m_ref[gi, q_rows, :] = jnp.broadcast_to(st[u].pop("m"), (WIN_BLK, LANES))
                l_ref[gi, q_rows, :] = jnp.broadcast_to(st[u].pop("l"), (WIN_BLK, LANES))

            for stage in (logits, softmax, values):
                for u in range(n):
                    stage(u)

        if dil == 1:
            for j0 in range(0, n_blocks, ATTN_UNITS):
                run_units([(0, j) for j in range(j0, min(j0 + ATTN_UNITS, n_blocks))])
        else:
            jb = min(n_blocks, ATTN_UNITS)
            rb = max(ATTN_UNITS // n_blocks, 1)
            for j0 in range(0, n_blocks, jb):
                def body(it, carry, j0=j0, jb=jb, rb=rb, run_units=run_units):
                    run_units([(it * rb + k, j) for k in range(rb) for j in range(j0, j0 + jb)])
                    return carry
                for it in range(dil // rb):
                    body(it, 0)

    outers = [dil // min(dil, MAX_ROW_STRIDE) for _, dil in DIL_GROUPS]
    parts = max(outers)
    cnt = 2 * WIN_BLK
    for p in range(parts):
        for i0 in range(0, tq // parts, cnt):
            def rows_in(gi):
                if outers[gi] == parts:
                    return pl.ds(p * (tq // parts) + i0, cnt)
                assert outers[gi] == 1
                return strided(p + parts * i0, cnt, parts)
            m_all = jnp.maximum(jnp.maximum(m_ref[0, rows_in(0), :], m_ref[1, rows_in(1), :]),
                                m_ref[2, rows_in(2), :])
            num = jnp.zeros((cnt, LANES), F32)
            den = jnp.zeros((cnt, LANES), F32)
            for gi in range(N_GROUPS):
                wgt = jnp.exp2(m_ref[gi, rows_in(gi), :] - m_all)
                num = num + wgt * acc_ref[gi, rows_in(gi), :]
                den = den + wgt * l_ref[gi, rows_in(gi), :]
            merged_ref[strided(p + parts * i0, cnt, parts), :] = num / den
    o_ref[...] = merged_ref[...].astype(o_ref.dtype)


def _attn_prompt(proj, bias_tiles, batch, seq):
    tq = ATTN_TILE
    nt = seq // tq
    in_specs = []
    args = []
    for gi, (_, dil) in enumerate(DIL_GROUPS):
        span = dil * WIN_BLK
        cq = (COL_QKV_B + gi * 3 * DIL_W) // LANES
        ck = cq + DIL_HEADS
        cv = ck + DIL_HEADS

        def cur(col):
            return pl.BlockSpec((tq, LANES), lambda b, i, h, col=col: (b * nt + i, col + h))

        def prev(col, span=span):
            per = tq // span
            return pl.BlockSpec(
                (span, LANES),
                lambda b, i, h, col=col, per=per: (jnp.maximum((b * nt + i) * per - 1, 0), col + h))

        in_specs += [cur(cq), prev(ck), cur(ck), prev(cv), cur(cv)]
        args += [proj] * 5
    in_specs.append(pl.BlockSpec((N_GROUPS, None, WIN_BLK, 2 * WIN_BLK), lambda b, i, h: (0, h, 0, 0)))
    args.append(bias_tiles)
    return pl.pallas_call(
        _attn_prompt_kernel,
        grid=(batch, nt, DIL_HEADS),
        in_specs=in_specs,
        out_specs=pl.BlockSpec((tq, LANES), lambda b, i, h: (b * nt + i, h)),
        out_shape=jax.ShapeDtypeStruct((batch * seq, DIL_W), BF16),
        scratch_shapes=[pltpu.VMEM((N_GROUPS, tq, LANES), F32)] * 3
        + [pltpu.VMEM((N_GROUPS, WIN_BLK, 2 * WIN_BLK), F32), pltpu.VMEM((5, tq, LANES), F32),
           pltpu.VMEM((tq, LANES), F32)],
        compiler_params=pltpu.CompilerParams(
            dimension_semantics=("parallel", "parallel", "parallel"), vmem_limit_bytes=VMEM_LIMIT),
        name="attn_prompt",
    )(*args)


def _attn_sample_kernel(*refs):
    q_refs = refs[0:N_GROUPS]
    kn_refs = refs[N_GROUPS:2 * N_GROUPS]
    vn_refs = refs[2 * N_GROUPS:3 * N_GROUPS]
    kc_refs = refs[3 * N_GROUPS:4 * N_GROUPS]
    vc_refs = refs[4 * N_GROUPS:5 * N_GROUPS]
    bias0_ref, bias_ref, o_ref = refs[5 * N_GROUPS:5 * N_GROUPS + 3]
    m_ref, l_ref, acc_ref = refs[5 * N_GROUPS + 3:]
    t = pl.program_id(0)
    scale = HEAD_DIM ** -0.5

    def logits(gi, k, bias):
        s = jnp.sum(q_refs[gi][...] * k, axis=-1, keepdims=True) * scale + bias
        return jnp.broadcast_to(s, k.shape)

    @pl.when(t == 0)
    def _():
        for gi in range(N_GROUPS):
            s = logits(gi, kn_refs[gi][...], bias0_ref[gi])
            m_ref[gi] = s
            l_ref[gi] = jnp.ones_like(s)
            acc_ref[gi] = vn_refs[gi][...]

    for gi in range(N_GROUPS):
        s = logits(gi, kc_refs[gi][...], bias_ref[gi])
        m_old = m_ref[gi]
        m_new = jnp.maximum(m_old, s)
        alpha = jnp.exp(m_old - m_new)
        p = jnp.exp(s - m_new)
        l_ref[gi] = alpha * l_ref[gi] + p
        acc_ref[gi] = alpha * acc_ref[gi] + p * vc_refs[gi][...]
        m_ref[gi] = m_new

    @pl.when(t == pl.num_programs(0) - 1)
    def _():
        m_all = jnp.maximum(jnp.maximum(m_ref[0], m_ref[1]), m_ref[2])
        num = jnp.zeros(acc_ref.shape[1:], F32)
        den = jnp.zeros(m_ref.shape[1:], F32)
        for gi in range(N_GROUPS):
            wgt = jnp.exp(m_ref[gi] - m_all)
            num = num + wgt * acc_ref[gi]
            den = den + wgt * l_ref[gi]
        o_ref[...] = num / den


def _attn_sample(q_g, kn_g, vn_g, k_caches, v_caches, bias0, bias_steps):
    batch = q_g[0].shape[0]
    small = pl.BlockSpec((batch, DIL_HEADS, HEAD_DIM), lambda t: (0, 0, 0))
    in_specs = [small] * (3 * N_GROUPS)
    for _ in range(2):
        for _, dil in DIL_GROUPS:
            in_specs.append(pl.BlockSpec((None, batch, None, DIL_HEADS, HEAD_DIM),
                                         lambda t, dil=dil: (0, 0, t * dil, 0, 0)))
    in_specs.append(pl.BlockSpec((N_GROUPS, 1, DIL_HEADS, 1), lambda t: (0, 0, 0, 0)))
    in_specs.append(pl.BlockSpec((None, N_GROUPS, 1, DIL_HEADS, 1), lambda t: (t, 0, 0, 0, 0)))
    return pl.pallas_call(
        _attn_sample_kernel,
        grid=(WIN_BLK,),
        in_specs=in_specs,
        out_specs=pl.BlockSpec((batch, DIL_HEADS, HEAD_DIM), lambda t: (0, 0, 0)),
        out_shape=jax.ShapeDtypeStruct((batch, DIL_HEADS, HEAD_DIM), F32),
        scratch_shapes=[
            pltpu.VMEM((N_GROUPS, batch, DIL_HEADS, HEAD_DIM), F32),
            pltpu.VMEM((N_GROUPS, batch, DIL_HEADS, HEAD_DIM), F32),
            pltpu.VMEM((N_GROUPS, batch, DIL_HEADS, HEAD_DIM), F32),
        ],
        compiler_params=pltpu.CompilerParams(
            dimension_semantics=("arbitrary",), vmem_limit_bytes=VMEM_LIMIT),
        name="attn_sample",
    )(*q_g, *kn_g, *vn_g, *k_caches, *v_caches, bias0, bias_steps)


def _output_kernel(ob_ref, zb_ref, gateb_ref, ha_ref, x_ref, wpb_ref, wout_ref, gpost_ref, y_ref):
    yb = (ob_ref[...].astype(F32) * _silu(zb_ref[...])).astype(BF16)
    p_b = jnp.dot(yb, wpb_ref[...], preferred_element_type=F32)
    hid = ha_ref[...].astype(F32) + _sigmoid(gateb_ref[...].astype(F32)) * p_b
    out = jnp.dot(hid.astype(BF16), wout_ref[...], preferred_element_type=F32)
    ms = jnp.mean(out * out, axis=-1, keepdims=True)
    y_ref[...] = x_ref[...] + out * lax.rsqrt(ms + RMS_EPS) * gpost_ref[...]


def _output(o_b, proj, gates, h_a, x2d, w_proj_b, w_out, g_post, tm):
    m = x2d.shape[0]
    return pl.pallas_call(
        _output_kernel,
        grid=(m // tm,),
        in_specs=[
            pl.BlockSpec((tm, DIL_W), lambda i: (i, 0)),
            pl.BlockSpec((tm, DIL_W), lambda i: (i, COL_Z_B // DIL_W)),
            pl.BlockSpec((tm, D_MODEL), lambda i: (i, COL_GATE_B // D_MODEL)),
            pl.BlockSpec((tm, D_MODEL), lambda i: (i, 0)),
            pl.BlockSpec((tm, D_MODEL), lambda i: (i, 0)),
            pl.BlockSpec((DIL_W, D_MODEL), lambda i: (0, 0)),
            pl.BlockSpec((D_MODEL, D_MODEL), lambda i: (0, 0)),
            pl.BlockSpec((1, D_MODEL), lambda i: (0, 0)),
        ],
        out_specs=pl.BlockSpec((tm, D_MODEL), lambda i: (i, 0)),
        out_shape=jax.ShapeDtypeStruct((m, D_MODEL), F32),
        compiler_params=pltpu.CompilerParams(
            dimension_semantics=("parallel",), vmem_limit_bytes=VMEM_LIMIT),
        name="output",
    )(o_b, proj, gates, h_a, x2d, w_proj_b, w_out, g_post)


def _rel_bucket(dist):
    max_exact = REL_BUCKETS // 2
    d = jnp.maximum(dist, 1).astype(F32)
    large = max_exact + (jnp.log(d / max_exact) / math.log(REL_MAX_DIST / max_exact)
                         * (REL_BUCKETS - max_exact)).astype(jnp.int32)
    large = jnp.minimum(large, REL_BUCKETS - 1)
    return jnp.where(dist < max_exact, dist, large)


def _group_biases(rel_table):
    out = []
    for gi, (win, dil) in enumerate(DIL_GROUPS):
        dist = jnp.arange(win // dil + 1, dtype=jnp.int32) * dil
        b = rel_table[_rel_bucket(dist)][:, gi * DIL_HEADS:(gi + 1) * DIL_HEADS]
        out.append(b.T.astype(F32))
    return out


def _bias_tiles(biases):
    period = 3 * WIN_BLK
    tiles = []
    for b in biases:
        u = jnp.concatenate([b[:, ::-1], jnp.full((DIL_HEADS, period - WIN_BLK - 1), NEG, F32)], axis=1)
        flat = jnp.tile(u, (1, WIN_BLK))[:, :WIN_BLK * (period - 1)]
        tiles.append(flat.reshape(DIL_HEADS, WIN_BLK, period - 1)[:, :, :2 * WIN_BLK])
    return jnp.stack(tiles, axis=0).astype(F32)


WEIGHT_PREP_ROWS = 128


def _weight_prep_kernel(w_ref, main_ref, ab_ref):
    dst = 0
    for lo, hi in ((SRC_QKV_A, SRC_Z_A), (SRC_QKV_B, SRC_Z_B), (SRC_Z_B, SRC_GATE_A),
                   (SRC_Z_A, SRC_AB), (SRC_GATE_A, SRC_GATE_B), (SRC_GATE_B, SRC_END)):
        main_ref[:, dst:dst + hi - lo] = w_ref[:, lo:hi].astype(BF16)
        dst += hi - lo
    ab = w_ref[:, SRC_AB:SRC_QKV_B].astype(BF16)
    ab_ref[...] = jnp.concatenate([ab, jnp.zeros((ab.shape[0], LANES - ab.shape[1]), BF16)], axis=1)


def _weight_prep(w_in):
    rows = WEIGHT_PREP_ROWS
    return pl.pallas_call(
        _weight_prep_kernel,
        grid=(D_MODEL // rows,),
        in_specs=[pl.BlockSpec((rows, SRC_END), lambda i: (i, 0))],
        out_specs=[pl.BlockSpec((rows, MAIN_WIDTH), lambda i: (i, 0)),
                   pl.BlockSpec((rows, LANES), lambda i: (i, 0))],
        out_shape=[jax.ShapeDtypeStruct((D_MODEL, MAIN_WIDTH), BF16),
                   jax.ShapeDtypeStruct((D_MODEL, LANES), BF16)],
        compiler_params=pltpu.CompilerParams(
            dimension_semantics=("parallel",), vmem_limit_bytes=VMEM_LIMIT),
        name="weight_prep",
    )(w_in)


def _layer_weights(w_in, conv_w, a_log, dt_bias, g_head_norm, w_proj_a, w_proj_b, w_out):
    w_main, w_ab = _weight_prep(w_in)
    pad = (0, LANES - GDN_HEADS)
    return dict(
        w_main=w_main, w_ab=w_ab, conv_wt=conv_w.T.astype(F32),
        alog_pad=jnp.pad(a_log.astype(F32), pad)[None, :],
        dtb_pad=jnp.pad(dt_bias.astype(F32), pad)[None, :],
        ghn=g_head_norm.astype(F32)[None, :],
        w_proj_a=w_proj_a.astype(BF16), w_proj_b=w_proj_b.astype(BF16), w_out=w_out.astype(BF16))


def _window_rows_kernel(k_ref, v_ref, ko_ref, vo_ref):
    rows = k_ref.shape[0]
    for src, dst in ((k_ref, ko_ref), (v_ref, vo_ref)):
        for h in range(DIL_HEADS):
            dst[pl.ds(h, rows, stride=DIL_HEADS), :] = src[:, h * HEAD_DIM:(h + 1) * HEAD_DIM]


def _window_rows(proj3, col_k, keep):
    batch, seq, _ = proj3.shape
    rows = min(keep, 1024)
    first = (seq - keep) // rows
    out = jax.ShapeDtypeStruct((batch, keep * DIL_HEADS, HEAD_DIM), F32)
    k_out, v_out = pl.pallas_call(
        _window_rows_kernel,
        grid=(batch, keep // rows),
        in_specs=[
            pl.BlockSpec((None, rows, DIL_W), lambda b, r: (b, first + r, col_k // DIL_W)),
            pl.BlockSpec((None, rows, DIL_W), lambda b, r: (b, first + r, col_k // DIL_W + 1)),
        ],
        out_specs=[
            pl.BlockSpec((None, rows * DIL_HEADS, HEAD_DIM), lambda b, r: (b, r, 0)),
            pl.BlockSpec((None, rows * DIL_HEADS, HEAD_DIM), lambda b, r: (b, r, 0)),
        ],
        out_shape=[out, out],
        compiler_params=pltpu.CompilerParams(
            dimension_semantics=("parallel", "parallel"), vmem_limit_bytes=VMEM_LIMIT),
        name="window_rows",
    )(proj3, proj3)
    shape = (batch, keep, DIL_HEADS, HEAD_DIM)
    return k_out.reshape(shape), v_out.reshape(shape)


def kernel(x_prompt, x_sample, state_gdn, state_conv, cache_k_w128, cache_v_w128, cache_k_w512,
           cache_v_w512, cache_k_w2048, cache_v_w2048, rel_table, g_pre, w_in, conv_w, a_log, dt_bias,
           g_head_norm, w_proj_a, w_proj_b, w_out, g_post):
    batch, seq, _ = x_prompt.shape
    dec_batch = x_sample.shape[0]
    k_caches = (cache_k_w128, cache_k_w512, cache_k_w2048)
    v_caches = (cache_v_w128, cache_v_w512, cache_v_w2048)
    biases = _group_biases(rel_table)
    bias_tiles = _bias_tiles(biases) * LOG2_E
    lw = _layer_weights(w_in[0], conv_w[0], a_log[0], dt_bias[0], g_head_norm[0], w_proj_a[0],
                        w_proj_b[0], w_out[0])
    g_pre_l = g_pre[0][None, :]
    g_post_l = g_post[0][None, :]

    xp = x_prompt.reshape(batch * seq, D_MODEL)
    proj, gates, ab = _in_projection(xp, g_pre_l, lw["w_main"], lw["w_ab"], tm=INPROJ_ROWS, tn=INPROJ_COLS)
    h_a, p_gdn = _gdn_prompt(proj, gates, ab, lw["conv_wt"], lw["alog_pad"], lw["dtb_pad"], lw["ghn"],
                             lw["w_proj_a"], batch, seq)
    o_b = _attn_prompt(proj, bias_tiles, batch, seq)
    y_prompt = _output(o_b, proj, gates, h_a, xp, lw["w_proj_b"], lw["w_out"], g_post_l, tm=OUTPUT_ROWS)
    y_prompt = y_prompt.reshape(batch, seq, D_MODEL)
    proj3 = proj.reshape(batch, seq, WIDE_WIDTH)
    p_conv = proj3[:, seq - (CONV_WIDTH - 1):, :CONV_DIM]
    p_rows = []
    for gi, (win, _) in enumerate(DIL_GROUPS):
        keep = min(win, seq)
        ck = COL_QKV_B + gi * 3 * DIL_W + DIL_W
        p_rows += list(_window_rows(proj3, ck, keep))

    xs = x_sample.reshape(dec_batch, D_MODEL)
    proj_s, gates_s, ab_s = _in_projection(xs, g_pre_l, lw["w_main"], lw["w_ab"], tm=dec_batch,
                                           tn=INPROJ_COLS)
    sc_t = jnp.transpose(state_conv[0], (1, 0, 2))
    o_s, s_gdn = _gdn_sample(sc_t, proj_s, ab_s, lw["conv_wt"], lw["alog_pad"], lw["dtb_pad"],
                             state_gdn[0], nb=GDN_SAMPLE_SEQS)
    h_a_s = _gdn_sample_out(o_s, gates_s, lw["ghn"], lw["w_proj_a"])
    q_g, kn_g, vn_g = [], [], []
    for gi in range(N_GROUPS):
        cq = COL_QKV_B + gi * 3 * DIL_W
        q_g.append(proj_s[:, cq:cq + DIL_W].reshape(dec_batch, DIL_HEADS, HEAD_DIM))
        kn_g.append(proj_s[:, cq + DIL_W:cq + 2 * DIL_W].reshape(dec_batch, DIL_HEADS, HEAD_DIM))
        vn_g.append(proj_s[:, cq + 2 * DIL_W:cq + 3 * DIL_W].reshape(dec_batch, DIL_HEADS, HEAD_DIM))
    bias_all = jnp.stack(biases, axis=0)
    bias0 = bias_all[:, None, :, 0:1]
    bias_steps = jnp.transpose(bias_all[:, :, :0:-1], (2, 0, 1))[:, :, None, :, None]
    o_b_s = _attn_sample(q_g, kn_g, vn_g, k_caches, v_caches, bias0, bias_steps)
    y_sample = _output(o_b_s.reshape(dec_batch, DIL_W), proj_s, gates_s, h_a_s, xs, lw["w_proj_b"],
                       lw["w_out"], g_post_l, tm=dec_batch)
    y_sample = y_sample.reshape(dec_batch, 1, D_MODEL)
    s_conv = jnp.concatenate([state_conv[0][:, 1:, :], proj_s[:, None, :CONV_DIM]], axis=1)
    s_rows = []
    for gi in range(N_GROUPS):
        s_rows += [kn_g[gi][:, None], vn_g[gi][:, None]]

    return (y_prompt, y_sample, p_gdn[None], p_conv[None], *[r[None] for r in p_rows],
            s_gdn[None], s_conv[None], *[r[None] for r in s_rows])
```

```python
import functools
import math

import jax
import jax.numpy as jnp
from jax import lax
from jax.experimental import pallas as pl
from jax.experimental.pallas import tpu as pltpu

F32 = jnp.float32
BF16 = jnp.bfloat16

D_MODEL = 1024
GDN_HEADS = 8
HEAD_DIM = 128
GDN_QK = GDN_HEADS * HEAD_DIM
CONV_DIM = 3 * GDN_QK
CONV_WIDTH = 4
DIL_GROUPS = ((128, 1), (512, 4), (2048, 16))
N_GROUPS = 3
DIL_HEADS = 4
DIL_W = DIL_HEADS * HEAD_DIM
WIN_BLK = 128
REL_BUCKETS = 32
REL_MAX_DIST = 2048
RMS_EPS = 1e-6
NEG = -1e30
LOG2_E = math.log2(math.e)

COL_QKV_A = 0
COL_QKV_B = 3072
COL_Z_B = 7680
WIDE_WIDTH = 8192
COL_Z_A = 0
COL_GATE_A = 1024
COL_GATE_B = 2048
GATE_WIDTH = 3072
MAIN_WIDTH = WIDE_WIDTH + GATE_WIDTH
SRC_QKV_A, SRC_Z_A, SRC_AB, SRC_QKV_B, SRC_Z_B, SRC_GATE_A, SRC_GATE_B, SRC_END = (
    0, 3072, 4096, 4112, 8720, 9232, 10256, 11280)
LANES = 128

VMEM_LIMIT = 56 * 1024 * 1024
BIG_VMEM_LIMIT = 60 * 1024 * 1024

INPROJ_ROWS = 2048
INPROJ_COLS = 1024
OUTPUT_ROWS = 1024
GDN_CHUNK = 64
GDN_SEQS = 8
GDN_SAMPLE_SEQS = 32
ATTN_TILE = 2048
ATTN_UNITS = 16
MAX_ROW_STRIDE = 4


def _silu(x):
    half = 0.5 * x
    return half * jnp.tanh(half) + half


def _sigmoid(x):
    return 1.0 / (1.0 + jnp.exp(-x))


def _softplus(x):
    return jnp.maximum(x, 0.0) + jnp.log(1.0 + jnp.exp(-jnp.abs(x)))


def _dot(a, b):
    return jnp.dot(a.astype(BF16), b.astype(BF16), preferred_element_type=F32)


def _dot_tn(a, b):
    return lax.dot_general(a.astype(BF16), b.astype(BF16), (((0,), (0,)), ((), ())),
                           preferred_element_type=F32)


def _inproj_kernel(x_ref, g_ref, w_ref, wab_ref, wide_ref, gate_ref, ab_ref, xn_ref, *, wide_tiles):
    j = pl.program_id(1)

    @pl.when(j == 0)
    def _():
        x = x_ref[...]
        ms = jnp.mean(x * x, axis=-1, keepdims=True)
        xn = (x * lax.rsqrt(ms + RMS_EPS) * g_ref[...]).astype(BF16)
        xn_ref[...] = xn
        ab_ref[...] = jnp.dot(xn, wab_ref[...], preferred_element_type=F32)

    @pl.when(j < wide_tiles)
    def _():
        wide_ref[...] = jnp.dot(xn_ref[...], w_ref[...], preferred_element_type=F32)

    @pl.when(j >= wide_tiles)
    def _():
        gate_ref[...] = jnp.dot(xn_ref[...], w_ref[...], preferred_element_type=F32).astype(gate_ref.dtype)


def _in_projection(x2d, g_pre, w_main, w_ab, tm, tn):
    m = x2d.shape[0]
    wide_tiles = WIDE_WIDTH // tn
    return pl.pallas_call(
        functools.partial(_inproj_kernel, wide_tiles=wide_tiles),
        grid=(m // tm, MAIN_WIDTH // tn),
        in_specs=[
            pl.BlockSpec((tm, D_MODEL), lambda i, j: (i, 0)),
            pl.BlockSpec((1, D_MODEL), lambda i, j: (0, 0)),
            pl.BlockSpec((D_MODEL, tn), lambda i, j: (0, j)),
            pl.BlockSpec((D_MODEL, LANES), lambda i, j: (0, 0)),
        ],
        out_specs=[
            pl.BlockSpec((tm, tn), lambda i, j: (i, jnp.minimum(j, wide_tiles - 1))),
            pl.BlockSpec((tm, tn), lambda i, j: (i, jnp.maximum(j - wide_tiles, 0))),
            pl.BlockSpec((tm, LANES), lambda i, j: (i, 0)),
        ],
        out_shape=[
            jax.ShapeDtypeStruct((m, WIDE_WIDTH), F32),
            jax.ShapeDtypeStruct((m, GATE_WIDTH), BF16),
            jax.ShapeDtypeStruct((m, LANES), F32),
        ],
        scratch_shapes=[pltpu.VMEM((tm, D_MODEL), BF16)],
        compiler_params=pltpu.CompilerParams(
            dimension_semantics=("parallel", "arbitrary"), vmem_limit_bytes=BIG_VMEM_LIMIT),
        name="in_projection",
    )(x2d, g_pre, w_main, w_ab)


GROUP_HEADS = 4


def _block_diag_rhs(x4, half_masks):
    c = x4.shape[0]
    zeros = jnp.zeros((c, LANES), x4.dtype)
    row_blocks = []
    for h in range(GROUP_HEADS):
        piece = x4[:, (h // 2) * LANES:(h // 2 + 1) * LANES] * half_masks[h % 2]
        row_blocks.append(jnp.concatenate([piece if v == h // 2 else zeros for v in range(2)], axis=1))
    return jnp.concatenate(row_blocks, axis=0)


def _gdn_prompt_kernel(qkv_ref, z_ref, ab_ref, convw_ref, alog_ref, dtb_ref, ghn_ref,
                       ya_ref, sfin_ref, xpad_ref, qkvn_ref, o_ref, s_ref,
                       u_ref, w_ref, qd_ref, kd_ref, qk_ref):
    c = GDN_CHUNK
    tl = GDN_SEQS * c
    gw = GROUP_HEADS * c
    step = pl.program_id(1)

    def conv_stage(i):
        for blk in range(CONV_DIM // LANES):
            cols = slice(blk * LANES, (blk + 1) * LANES)
            xpad_ref[i, blk, 8:8 + c, :] = qkv_ref[i, :, cols]
            for phase in range(2):
                acc = None
                for w in range(CONV_WIDTH):
                    tap = xpad_ref[i, blk, pl.ds(5 + phase + w, c // 2, stride=2), :] * convw_ref[w:w + 1, cols]
                    acc = tap if acc is None else acc + tap
                y = _silu(acc)
                if blk < 2 * GDN_HEADS:
                    y = y * lax.rsqrt(jnp.sum(y * y, axis=-1, keepdims=True) + RMS_EPS)
                    if blk < GDN_HEADS:
                        y = y * (HEAD_DIM ** -0.5)
                qkvn_ref[blk, pl.ds(i * c + phase, c // 2, stride=2), :] = y
            xpad_ref[i, blk, 5:8, :] = xpad_ref[i, blk, c + 5:c + 8, :]

    @pl.when(step == 0)
    def _():
        xpad_ref[:, :, 0:8, :] = jnp.zeros((GDN_SEQS, CONV_DIM // LANES, 8, LANES), F32)
        s_ref[...] = jnp.zeros_like(s_ref)

    ab = ab_ref[...].reshape(tl, LANES)
    g_full = -jnp.exp(alog_ref[...]) * _softplus(ab + dtb_ref[...])
    beta_full = _sigmoid(ab)
    row = lax.broadcasted_iota(jnp.int32, (tl, tl), 0)
    col = lax.broadcasted_iota(jnp.int32, (tl, tl), 1)
    same_chunk = (row // c) == (col // c)
    tri = jnp.where(same_chunk & (col <= row), 1.0, 0.0).astype(BF16)
    g_hi = g_full.astype(BF16)
    g_rest = g_full - g_hi.astype(F32)
    g_mid = g_rest.astype(BF16)
    g_lo = (g_rest - g_mid.astype(F32)).astype(BF16)
    g_cum = jnp.dot(jnp.concatenate([tri, tri, tri], axis=1), jnp.concatenate([g_hi, g_mid, g_lo], axis=0),
                    preferred_element_type=F32)
    g_cum_t = g_cum.T
    g_last = jnp.concatenate(
        [jnp.broadcast_to(g_cum[i * c + c - 1:(i + 1) * c, :], (c, LANES)) for i in range(GDN_SEQS)], axis=0)
    exp_g = jnp.exp(g_cum)
    exp_rest = jnp.exp(g_last - g_cum)
    exp_last = jnp.exp(g_last)

    ci = lax.broadcasted_iota(jnp.int32, (c, gw), 0)
    cj = lax.broadcasted_iota(jnp.int32, (c, gw), 1) % c
    causal4 = cj <= ci
    strict4 = cj < ci
    half_lane = lax.broadcasted_iota(jnp.int32, (c, LANES), 1) // c
    half_masks = [jnp.where(half_lane == k, 1.0, 0.0).astype(BF16) for k in range(2)]

    def lanes_of(col_values, width):
        return jnp.concatenate([jnp.broadcast_to(v, (c, width)) for v in col_values], axis=1)

    n_groups = GDN_HEADS // GROUP_HEADS
    zero_s = jnp.zeros((HEAD_DIM, HEAD_DIM), BF16)
    zero_v = jnp.zeros((c, HEAD_DIM), BF16)

    def group_cols(hg):
        return slice(hg * GROUP_HEADS * LANES, (hg + 1) * GROUP_HEADS * LANES)

    def bd_dot(lhs, x4):
        return jnp.dot(lhs.astype(BF16), _block_diag_rhs(x4.astype(BF16), half_masks),
                       preferred_element_type=F32)

    def seq_stages(i):
        rows = slice(i * c, (i + 1) * c)
        st = {}

        def build():
            a4s = []
            for hg in range(n_groups):
                heads = range(hg * GROUP_HEADS, (hg + 1) * GROUP_HEADS)
                cols = group_cols(hg)

                def head_blocks(first_blk):
                    return jnp.concatenate([qkvn_ref[first_blk + h, rows, :] for h in heads], axis=1)

                q4 = head_blocks(0)
                k4 = head_blocks(GDN_HEADS)
                v4 = head_blocks(2 * GDN_HEADS)
                beta_e = lanes_of([beta_full[rows, GDN_HEADS + h:GDN_HEADS + h + 1] for h in heads], LANES)
                eg_e = lanes_of([exp_g[rows, h:h + 1] for h in heads], LANES)
                er_e = lanes_of([exp_rest[rows, h:h + 1] for h in heads], LANES)
                gi4 = lanes_of([g_cum[rows, h:h + 1] for h in heads], c)
                gj4 = jnp.concatenate([g_cum_t[h:h + 1, rows] for h in heads], axis=1)
                decay4 = jnp.exp(jnp.where(causal4, gi4 - gj4, NEG))
                kb4 = k4 * beta_e
                k4b = k4.astype(BF16)
                k_rhs = jnp.concatenate(
                    [jnp.concatenate([k4b[:, g * LANES:(g + 1) * LANES] if g == h else zero_v
                                      for g in range(GROUP_HEADS)], axis=1)
                     for h in range(GROUP_HEADS)], axis=0)
                kkqk = lax.dot_general(jnp.concatenate([kb4, q4], axis=0).astype(BF16), k_rhs,
                                       (((1,), (1,)), ((), ())), preferred_element_type=F32)
                a4s.append(jnp.where(strict4, kkqk[:c] * decay4, 0.0))
                qk_ref[rows, hg * gw:(hg + 1) * gw] = kkqk[c:] * decay4
                u_ref[rows, cols] = v4 * beta_e
                w_ref[rows, cols] = kb4 * eg_e
                qd_ref[rows, cols] = q4 * eg_e
                kd_ref[rows, cols] = k4 * er_e
            st["a"] = a4s

        def square():
            st["e"] = [-a4 for a4 in st["a"]]
            st["x"] = [bd_dot(a4, a4) for a4 in st["a"]]

        def double():
            boths = [bd_dot(jnp.concatenate([e, x], axis=0), x) for e, x in zip(st["e"], st["x"])]
            st["e"] = [e + x + both[:c] for e, x, both in zip(st["e"], st["x"], boths)]
            st["x"] = [both[c:] for both in boths]

        def last_factor():
            st["e"] = [e + x + bd_dot(e, x) for e, x in zip(st["e"], st["x"])]

        def solve():
            for hg, e4 in enumerate(st["e"]):
                cols = group_cols(hg)
                vb4 = u_ref[rows, cols]
                kbe4 = w_ref[rows, cols]
                us, ws = [], []
                for k in range(GROUP_HEADS):
                    hc = slice(k * LANES, (k + 1) * LANES)
                    rhs = jnp.concatenate([vb4[:, hc], kbe4[:, hc]], axis=1)
                    uw = rhs + _dot(e4[:, k * c:(k + 1) * c], rhs)
                    us.append(uw[:, :LANES])
                    ws.append(uw[:, LANES:])
                u_ref[rows, cols] = jnp.concatenate(us, axis=1)
                w_ref[rows, cols] = jnp.concatenate(ws, axis=1)

        def pair_cols(p):
            return slice(2 * p * LANES, (2 * p + 2) * LANES)

        def state_products():
            wsqs = []
            for p in range(GDN_HEADS // 2):
                cols = pair_cols(p)
                s_bd = jnp.concatenate(
                    [jnp.concatenate([s_ref[i, 2 * p].astype(BF16), zero_s], axis=1),
                     jnp.concatenate([zero_s, s_ref[i, 2 * p + 1].astype(BF16)], axis=1)], axis=0)
                wq = jnp.concatenate([w_ref[rows, cols], qd_ref[rows, cols]], axis=0).astype(BF16)
                wsqs.append(jnp.dot(wq, s_bd, preferred_element_type=F32))
            st["wsq"] = wsqs

        def outputs():
            vnbs = []
            for p, wsq in enumerate(st["wsq"]):
                cols = pair_cols(p)
                vnb = (u_ref[rows, cols] - wsq[:c]).astype(BF16)
                vnbs.append(vnb)
                vn_bd = jnp.concatenate([jnp.concatenate([vnb[:, :LANES], zero_v], axis=1),
                                         jnp.concatenate([zero_v, vnb[:, LANES:]], axis=1)], axis=0)
                qk2 = qk_ref[rows, 2 * p * c:(2 * p + 2) * c]
                o_ref[rows, cols] = wsq[c:] + jnp.dot(qk2.astype(BF16), vn_bd, preferred_element_type=F32)
            st["vn"] = vnbs

        def state_update():
            for p, vnb in enumerate(st["vn"]):
                kd2 = kd_ref[rows, pair_cols(p)]
                for k in range(2):
                    h = 2 * p + k
                    hc = slice(k * LANES, (k + 1) * LANES)
                    el = exp_last[i * c:i * c + 1, h:h + 1]
                    s_ref[i, h] = s_ref[i, h] * el + _dot_tn(kd2[:, hc], vnb[:, hc])

        doublings = int(math.log2(c)) - 2
        return ([functools.partial(conv_stage, i), build, square] + [double] * doublings
                + [last_factor, solve, state_products, outputs, state_update])

    for stages in zip(*[seq_stages(i) for i in range(GDN_SEQS)]):
        for stage in stages:
            stage()

    for h in range(GDN_HEADS):
        cols = slice(h * LANES, (h + 1) * LANES)
        o = o_ref[:, cols]
        o = o * lax.rsqrt(jnp.mean(o * o, axis=-1, keepdims=True) + RMS_EPS) * ghn_ref[...]
        y = o * _silu(z_ref[:, :, cols].astype(F32).reshape(tl, LANES))
        ya_ref[:, :, cols] = y.reshape(GDN_SEQS, c, LANES).astype(ya_ref.dtype)

    @pl.when(step == pl.num_programs(1) - 1)
    def _():
        sfin_ref[...] = s_ref[...]


def _gdn_prompt(proj, gates, ab, conv_wt, alog_pad, dtb_pad, ghn, batch, seq):
    c = GDN_CHUNK
    ns = GDN_SEQS
    tl = ns * c
    proj3 = proj.reshape(batch, seq, WIDE_WIDTH)
    gates3 = gates.reshape(batch, seq, GATE_WIDTH)
    ab3 = ab.reshape(batch, seq, LANES)
    y_a, s_fin = pl.pallas_call(
        _gdn_prompt_kernel,
        grid=(batch // ns, seq // c),
        in_specs=[
            pl.BlockSpec((ns, c, CONV_DIM), lambda b, l: (b, l, COL_QKV_A // CONV_DIM)),
            pl.BlockSpec((ns, c, GDN_QK), lambda b, l: (b, l, COL_Z_A // GDN_QK)),
            pl.BlockSpec((ns, c, LANES), lambda b, l: (b, l, 0)),
            pl.BlockSpec((CONV_WIDTH, CONV_DIM), lambda b, l: (0, 0)),
            pl.BlockSpec((1, LANES), lambda b, l: (0, 0)),
            pl.BlockSpec((1, LANES), lambda b, l: (0, 0)),
            pl.BlockSpec((1, LANES), lambda b, l: (0, 0)),
        ],
        out_specs=[
            pl.BlockSpec((ns, c, D_MODEL), lambda b, l: (b, l, 0)),
            pl.BlockSpec((ns, GDN_HEADS, HEAD_DIM, HEAD_DIM), lambda b, l: (b, 0, 0, 0)),
        ],
        out_shape=[
            jax.ShapeDtypeStruct((batch, seq, D_MODEL), BF16),
            jax.ShapeDtypeStruct((batch, GDN_HEADS, HEAD_DIM, HEAD_DIM), F32),
        ],
        scratch_shapes=[
            pltpu.VMEM((ns, CONV_DIM // LANES, c + 8, LANES), F32),
            pltpu.VMEM((CONV_DIM // LANES, tl, LANES), F32),
            pltpu.VMEM((tl, GDN_QK), F32),
            pltpu.VMEM((ns, GDN_HEADS, HEAD_DIM, HEAD_DIM), F32),
            pltpu.VMEM((tl, GDN_QK), F32),
            pltpu.VMEM((tl, GDN_QK), F32),
            pltpu.VMEM((tl, GDN_QK), F32),
            pltpu.VMEM((tl, GDN_QK), F32),
            pltpu.VMEM((tl, GDN_HEADS * GDN_CHUNK), F32),
        ],
        compiler_params=pltpu.CompilerParams(
            dimension_semantics=("parallel", "arbitrary"), vmem_limit_bytes=BIG_VMEM_LIMIT),
        name="gdn_prompt",
    )(proj3, gates3, ab3, conv_wt, alog_pad, dtb_pad, ghn)
    return y_a.reshape(batch * seq, GDN_QK), s_fin


def _gdn_sample_kernel(sc_q_ref, sc_k_ref, sc_v_ref, nq_ref, nk_ref, nv_ref, cwq_ref, cwk_ref, cwv_ref,
                       ab_ref, alog_ref, dtb_ref, s_ref, o_ref, snew_ref):
    h = pl.program_id(0)
    nb = nq_ref.shape[0]

    def conv_silu(sc_ref, new_ref, cw_ref):
        acc = new_ref[...] * cw_ref[3:4, :]
        for w in range(CONV_WIDTH - 1):
            acc = acc + sc_ref[w] * cw_ref[w:w + 1, :]
        return _silu(acc)

    q = conv_silu(sc_q_ref, nq_ref, cwq_ref)
    k = conv_silu(sc_k_ref, nk_ref, cwk_ref)
    v = conv_silu(sc_v_ref, nv_ref, cwv_ref)
    q = q * lax.rsqrt(jnp.sum(q * q, axis=-1, keepdims=True) + RMS_EPS) * (HEAD_DIM ** -0.5)
    k = k * lax.rsqrt(jnp.sum(k * k, axis=-1, keepdims=True) + RMS_EPS)

    ab = ab_ref[...]
    lane = lax.broadcasted_iota(jnp.int32, ab.shape, 1)
    g_full = -jnp.exp(alog_ref[...]) * _softplus(ab + dtb_ref[...])
    g = jnp.sum(jnp.where(lane == h, g_full, 0.0), axis=-1, keepdims=True)
    beta = jnp.sum(jnp.where(lane == h + GDN_HEADS, _sigmoid(ab), 0.0), axis=-1, keepdims=True)
    eg = jnp.exp(g)

    eye = (lax.broadcasted_iota(jnp.int32, (HEAD_DIM, HEAD_DIM), 0)
           == lax.broadcasted_iota(jnp.int32, (HEAD_DIM, HEAD_DIM), 1))

    def as_column(row):
        return jnp.sum(jnp.where(eye, row, 0.0), axis=1, keepdims=True)

    for b in range(nb):
        s = s_ref[b]
        k_col = as_column(k[b:b + 1, :])
        q_col = as_column(q[b:b + 1, :])
        e = eg[b:b + 1, :]
        ks = jnp.sum(k_col * s, axis=0, keepdims=True)
        v_new = beta[b:b + 1, :] * (v[b:b + 1, :] - e * ks)
        s_new = s * e + k_col * v_new
        snew_ref[b] = s_new
        o_ref[b:b + 1, :] = jnp.sum(q_col * s_new, axis=0, keepdims=True)


def _gdn_sample(state_conv_t, proj, ab, conv_wt, alog_pad, dtb_pad, state, nb):
    batch = proj.shape[0]
    hq = GDN_QK // LANES
    return pl.pallas_call(
        _gdn_sample_kernel,
        grid=(GDN_HEADS, batch // nb),
        in_specs=[
            pl.BlockSpec((CONV_WIDTH - 1, nb, LANES), lambda h, i: (0, i, h)),
            pl.BlockSpec((CONV_WIDTH - 1, nb, LANES), lambda h, i: (0, i, hq + h)),
            pl.BlockSpec((CONV_WIDTH - 1, nb, LANES), lambda h, i: (0, i, 2 * hq + h)),
            pl.BlockSpec((nb, LANES), lambda h, i: (i, h)),
            pl.BlockSpec((nb, LANES), lambda h, i: (i, hq + h)),
            pl.BlockSpec((nb, LANES), lambda h, i: (i, 2 * hq + h)),
            pl.BlockSpec((CONV_WIDTH, LANES), lambda h, i: (0, h)),
            pl.BlockSpec((CONV_WIDTH, LANES), lambda h, i: (0, hq + h)),
            pl.BlockSpec((CONV_WIDTH, LANES), lambda h, i: (0, 2 * hq + h)),
            pl.BlockSpec((nb, LANES), lambda h, i: (i, 0)),
            pl.BlockSpec((1, LANES), lambda h, i: (0, 0)),
            pl.BlockSpec((1, LANES), lambda h, i: (0, 0)),
            pl.BlockSpec((nb, None, HEAD_DIM, HEAD_DIM), lambda h, i: (i, h, 0, 0)),
        ],
        out_specs=[
            pl.BlockSpec((nb, LANES), lambda h, i: (i, h)),
            pl.BlockSpec((nb, None, HEAD_DIM, HEAD_DIM), lambda h, i: (i, h, 0, 0)),
        ],
        out_shape=[
            jax.ShapeDtypeStruct((batch, GDN_QK), F32),
            jax.ShapeDtypeStruct(state.shape, F32),
        ],
        compiler_params=pltpu.CompilerParams(
            dimension_semantics=("parallel", "parallel"), vmem_limit_bytes=VMEM_LIMIT),
        name="gdn_sample",
    )(state_conv_t, state_conv_t, state_conv_t, proj, proj, proj, conv_wt, conv_wt, conv_wt,
      ab, alog_pad, dtb_pad, state)


def _gdn_sample_out_kernel(o_ref, z_ref, ghn_ref, ya_ref):
    for h in range(GDN_HEADS):
        cols = slice(h * LANES, (h + 1) * LANES)
        o = o_ref[:, cols]
        o = o * lax.rsqrt(jnp.mean(o * o, axis=-1, keepdims=True) + RMS_EPS) * ghn_ref[...]
        ya_ref[:, cols] = (o * _silu(z_ref[:, cols].astype(F32))).astype(ya_ref.dtype)


def _gdn_sample_out(o, gates, ghn):
    batch = o.shape[0]
    return pl.pallas_call(
        _gdn_sample_out_kernel,
        grid=(1,),
        in_specs=[
            pl.BlockSpec((batch, GDN_QK), lambda i: (0, 0)),
            pl.BlockSpec((batch, GDN_QK), lambda i: (0, COL_Z_A // GDN_QK)),
            pl.BlockSpec((1, LANES), lambda i: (0, 0)),
        ],
        out_specs=pl.BlockSpec((batch, GDN_QK), lambda i: (0, 0)),
        out_shape=jax.ShapeDtypeStruct((batch, GDN_QK), BF16),
        compiler_params=pltpu.CompilerParams(vmem_limit_bytes=VMEM_LIMIT),
        name="gdn_sample_out",
    )(o, gates, ghn)


def _attn_prompt_kernel(*refs):
    ins = refs[:5 * N_GROUPS]
    bias_ref = refs[5 * N_GROUPS]
    o_ref = refs[5 * N_GROUPS + 1]
    acc_ref, m_ref, l_ref, bias_start_ref, split_ref, merged_ref = refs[5 * N_GROUPS + 2:]
    tq = ATTN_TILE
    first_tile = pl.program_id(1) == 0
    scale = (HEAD_DIM ** -0.5) * LOG2_E
    lane2 = lax.broadcasted_iota(jnp.int32, (1, 2 * WIN_BLK), 1)
    start_mask = jnp.where(first_tile & (lane2 < WIN_BLK), NEG, 0.0).astype(F32)
    for gi in range(N_GROUPS):
        bias_start_ref[gi] = bias_ref[gi] + start_mask

    def strided(start, n, stride):
        return pl.ds(start, n, stride=stride) if stride > 1 else pl.ds(start, n)

    for gi, (_, dil) in enumerate(DIL_GROUPS):
        srcs = ins[5 * gi:5 * gi + 5]
        n_blocks = tq // (dil * WIN_BLK)
        inner = min(dil, MAX_ROW_STRIDE)
        outer = dil // inner
        if outer > 1:
            for a, src in enumerate(srcs):
                n = src.shape[0]
                for p in range(outer):
                    split_ref[a, p * (n // outer):(p + 1) * (n // outer), :] = src[strided(p, n // outer, outer), :]
            srcs = [split_ref.at[a] for a in range(5)]
        q_ref, kp_ref, kc_ref, vp_ref, vc_ref = srcs

        def rows_of(ref_rows, r, first, count, inner=inner, outer=outer):
            if outer == 1:
                return strided(r + inner * first, count, inner)
            part = r % outer
            return strided(part * (ref_rows // outer) + r // outer + inner * first, count, inner)

        def run_units(units, q_ref=q_ref, kp_ref=kp_ref, kc_ref=kc_ref, vp_ref=vp_ref, vc_ref=vc_ref,
                      gi=gi, rows_of=rows_of):
            span = kp_ref.shape[0]
            n = len(units)
            st = [{} for _ in units]

            def logits(u):
                r, j = units[u]
                q = (q_ref[rows_of(tq, r, WIN_BLK * j, WIN_BLK), :] * scale).astype(BF16)
                if j == 0:
                    prev = rows_of(span, r, 0, WIN_BLK)
                    cur = rows_of(tq, r, 0, WIN_BLK)
                    kk = jnp.concatenate([kp_ref[prev, :], kc_ref[cur, :]], axis=0).astype(BF16)
                    st[u]["v"] = (vp_ref, prev, vc_ref, cur)
                else:
                    rows = rows_of(tq, r, WIN_BLK * (j - 1), 2 * WIN_BLK)
                    kk = kc_ref[rows, :].astype(BF16)
                    st[u]["v"] = (vc_ref, rows)
                st[u]["s"] = lax.dot_general(q, kk, (((1,), (1,)), ((), ())), preferred_element_type=F32)

            def softmax(u):
                r, j = units[u]
                s = st[u].pop("s") + (bias_start_ref[gi] if j == 0 else bias_ref[gi])
                m = jnp.max(s, axis=-1, keepdims=True)
                p = jnp.exp2(s - m)
                st[u]["m"] = m
                st[u]["l"] = jnp.sum(p, axis=-1, keepdims=True)
                st[u]["p"] = p.astype(BF16)

            def values(u):
                r, j = units[u]
                v = st[u].pop("v")
                if len(v) == 4:
                    vv = jnp.concatenate([v[0][v[1], :], v[2][v[3], :]], axis=0).astype(BF16)
                else:
                    vv = v[0][v[1], :].astype(BF16)
                acc = jnp.dot(st[u].pop("p"), vv, preferred_element_type=F32)
                q_rows = rows_of(tq, r, WIN_BLK * j, WIN_BLK)
                acc_ref[gi, q_rows, :] = acc
                m_ref[gi, q_rows, :] = jnp.broadcast_to(st[u].pop("m"), (WIN_BLK, LANES))
                l_ref[gi, q_rows, :] = jnp.broadcast_to(st[u].pop("l"), (WIN_BLK, LANES))

            for stage in (logits, softmax, values):
                for u in range(n):
                    stage(u)

        if dil == 1:
            for j0 in range(0, n_blocks, ATTN_UNITS):
                run_units([(0, j) for j in range(j0, min(j0 + ATTN_UNITS, n_blocks))])
        else:
            jb = min(n_blocks, ATTN_UNITS)
            rb = max(ATTN_UNITS // n_blocks, 1)
            for j0 in range(0, n_blocks, jb):
                def body(it, carry, j0=j0, jb=jb, rb=rb, run_units=run_units):
                    run_units([(it * rb + k, j) for k in range(rb) for j in range(j0, j0 + jb)])
                    return carry
                for it in range(dil // rb):
                    body(it, 0)

    outers = [dil // min(dil, MAX_ROW_STRIDE) for _, dil in DIL_GROUPS]
    parts = max(outers)
    cnt = 2 * WIN_BLK
    for p in range(parts):
        for i0 in range(0, tq // parts, cnt):
            def rows_in(gi):
                if outers[gi] == parts:
                    return pl.ds(p * (tq // parts) + i0, cnt)
                assert outers[gi] == 1
                return strided(p + parts * i0, cnt, parts)
            m_all = jnp.maximum(jnp.maximum(m_ref[0, rows_in(0), :], m_ref[1, rows_in(1), :]),
                                m_ref[2, rows_in(2), :])
            num = jnp.zeros((cnt, LANES), F32)
            den = jnp.zeros((cnt, LANES), F32)
            for gi in range(N_GROUPS):
                wgt = jnp.exp2(m_ref[gi, rows_in(gi), :] - m_all)
                num = num + wgt * acc_ref[gi, rows_in(gi), :]
                den = den + wgt * l_ref[gi, rows_in(gi), :]
            merged_ref[strided(p + parts * i0, cnt, parts), :] = num / den
    o_ref[...] = merged_ref[...].astype(o_ref.dtype)


def _attn_prompt(proj, bias_tiles, batch, seq):
    tq = ATTN_TILE
    nt = seq // tq
    in_specs = []
    args = []
    for gi, (_, dil) in enumerate(DIL_GROUPS):
        span = dil * WIN_BLK
        cq = (COL_QKV_B + gi * 3 * DIL_W) // LANES
        ck = cq + DIL_HEADS
        cv = ck + DIL_HEADS

        def cur(col):
            return pl.BlockSpec((tq, LANES), lambda b, i, h, col=col: (b * nt + i, col + h))

        def prev(col, span=span):
            per = tq // span
            return pl.BlockSpec(
                (span, LANES),
                lambda b, i, h, col=col, per=per: (jnp.maximum((b * nt + i) * per - 1, 0), col + h))

        in_specs += [cur(cq), prev(ck), cur(ck), prev(cv), cur(cv)]
        args += [proj] * 5
    in_specs.append(pl.BlockSpec((N_GROUPS, None, WIN_BLK, 2 * WIN_BLK), lambda b, i, h: (0, h, 0, 0)))
    args.append(bias_tiles)
    return pl.pallas_call(
        _attn_prompt_kernel,
        grid=(batch, nt, DIL_HEADS),
        in_specs=in_specs,
        out_specs=pl.BlockSpec((tq, LANES), lambda b, i, h: (b * nt + i, h)),
        out_shape=jax.ShapeDtypeStruct((batch * seq, DIL_W), BF16),
        scratch_shapes=[pltpu.VMEM((N_GROUPS, tq, LANES), F32)] * 3
        + [pltpu.VMEM((N_GROUPS, WIN_BLK, 2 * WIN_BLK), F32), pltpu.VMEM((5, tq, LANES), F32),
           pltpu.VMEM((tq, LANES), F32)],
        compiler_params=pltpu.CompilerParams(
            dimension_semantics=("parallel", "parallel", "parallel"), vmem_limit_bytes=VMEM_LIMIT),
        name="attn_prompt",
    )(*args)


def _attn_sample_kernel(*refs):
    q_refs = refs[0:N_GROUPS]
    kn_refs = refs[N_GROUPS:2 * N_GROUPS]
    vn_refs = refs[2 * N_GROUPS:3 * N_GROUPS]
    kc_refs = refs[3 * N_GROUPS:4 * N_GROUPS]
    vc_refs = refs[4 * N_GROUPS:5 * N_GROUPS]
    bias0_ref, bias_ref, o_ref = refs[5 * N_GROUPS:5 * N_GROUPS + 3]
    m_ref, l_ref, acc_ref = refs[5 * N_GROUPS + 3:]
    t = pl.program_id(0)
    scale = HEAD_DIM ** -0.5

    def logits(gi, k, bias):
        s = jnp.sum(q_refs[gi][...] * k, axis=-1, keepdims=True) * scale + bias
        return jnp.broadcast_to(s, k.shape)

    @pl.when(t == 0)
    def _():
        for gi in range(N_GROUPS):
            s = logits(gi, kn_refs[gi][...], bias0_ref[gi])
            m_ref[gi] = s
            l_ref[gi] = jnp.ones_like(s)
            acc_ref[gi] = vn_refs[gi][...]

    for gi in range(N_GROUPS):
        s = logits(gi, kc_refs[gi][...], bias_ref[gi])
        m_old = m_ref[gi]
        m_new = jnp.maximum(m_old, s)
        alpha = jnp.exp(m_old - m_new)
        p = jnp.exp(s - m_new)
        l_ref[gi] = alpha * l_ref[gi] + p
        acc_ref[gi] = alpha * acc_ref[gi] + p * vc_refs[gi][...]
        m_ref[gi] = m_new

    @pl.when(t == pl.num_programs(0) - 1)
    def _():
        m_all = jnp.maximum(jnp.maximum(m_ref[0], m_ref[1]), m_ref[2])
        num = jnp.zeros(acc_ref.shape[1:], F32)
        den = jnp.zeros(m_ref.shape[1:], F32)
        for gi in range(N_GROUPS):
            wgt = jnp.exp(m_ref[gi] - m_all)
            num = num + wgt * acc_ref[gi]
            den = den + wgt * l_ref[gi]
        o_ref[...] = num / den


def _attn_sample(q_g, kn_g, vn_g, k_caches, v_caches, bias0, bias_steps):
    batch = q_g[0].shape[0]
    small = pl.BlockSpec((batch, DIL_HEADS, HEAD_DIM), lambda t: (0, 0, 0))
    in_specs = [small] * (3 * N_GROUPS)
    for _ in range(2):
        for _, dil in DIL_GROUPS:
            in_specs.append(pl.BlockSpec((None, batch, None, DIL_HEADS, HEAD_DIM),
                                         lambda t, dil=dil: (0, 0, t * dil, 0, 0)))
    in_specs.append(pl.BlockSpec((N_GROUPS, 1, DIL_HEADS, 1), lambda t: (0, 0, 0, 0)))
    in_specs.append(pl.BlockSpec((None, N_GROUPS, 1, DIL_HEADS, 1), lambda t: (t, 0, 0, 0, 0)))
    return pl.pallas_call(
        _attn_sample_kernel,
        grid=(WIN_BLK,),
        in_specs=in_specs,
        out_specs=pl.BlockSpec((batch, DIL_HEADS, HEAD_DIM), lambda t: (0, 0, 0)),
        out_shape=jax.ShapeDtypeStruct((batch, DIL_HEADS, HEAD_DIM), F32),
        scratch_shapes=[
            pltpu.VMEM((N_GROUPS, batch, DIL_HEADS, HEAD_DIM), F32),
            pltpu.VMEM((N_GROUPS, batch, DIL_HEADS, HEAD_DIM), F32),
            pltpu.VMEM((N_GROUPS, batch, DIL_HEADS, HEAD_DIM), F32),
        ],
        compiler_params=pltpu.CompilerParams(
            dimension_semantics=("arbitrary",), vmem_limit_bytes=VMEM_LIMIT),
        name="attn_sample",
    )(*q_g, *kn_g, *vn_g, *k_caches, *v_caches, bias0, bias_steps)


def _output_kernel(ob_ref, zb_ref, gatea_ref, gateb_ref, ya_ref, x_ref, wpa_ref, wpb_ref, wout_ref, gpost_ref,
                   y_ref):
    p_a = jnp.dot(ya_ref[...], wpa_ref[...], preferred_element_type=F32)
    yb = (ob_ref[...].astype(F32) * _silu(zb_ref[...])).astype(BF16)
    p_b = jnp.dot(yb, wpb_ref[...], preferred_element_type=F32)
    hid = _sigmoid(gatea_ref[...].astype(F32)) * p_a + _sigmoid(gateb_ref[...].astype(F32)) * p_b
    out = jnp.dot(hid.astype(BF16), wout_ref[...], preferred_element_type=F32)
    ms = jnp.mean(out * out, axis=-1, keepdims=True)
    y_ref[...] = x_ref[...] + out * lax.rsqrt(ms + RMS_EPS) * gpost_ref[...]


def _output(o_b, proj, gates, y_a, x2d, w_proj_a, w_proj_b, w_out, g_post, tm):
    m = x2d.shape[0]
    return pl.pallas_call(
        _output_kernel,
        grid=(m // tm,),
        in_specs=[
            pl.BlockSpec((tm, DIL_W), lambda i: (i, 0)),
            pl.BlockSpec((tm, DIL_W), lambda i: (i, COL_Z_B // DIL_W)),
            pl.BlockSpec((tm, D_MODEL), lambda i: (i, COL_GATE_A // D_MODEL)),
            pl.BlockSpec((tm, D_MODEL), lambda i: (i, COL_GATE_B // D_MODEL)),
            pl.BlockSpec((tm, GDN_QK), lambda i: (i, 0)),
            pl.BlockSpec((tm, D_MODEL), lambda i: (i, 0)),
            pl.BlockSpec((GDN_QK, D_MODEL), lambda i: (0, 0)),
            pl.BlockSpec((DIL_W, D_MODEL), lambda i: (0, 0)),
            pl.BlockSpec((D_MODEL, D_MODEL), lambda i: (0, 0)),
            pl.BlockSpec((1, D_MODEL), lambda i: (0, 0)),
        ],
        out_specs=pl.BlockSpec((tm, D_MODEL), lambda i: (i, 0)),
        out_shape=jax.ShapeDtypeStruct((m, D_MODEL), F32),
        compiler_params=pltpu.CompilerParams(
            dimension_semantics=("parallel",), vmem_limit_bytes=VMEM_LIMIT),
        name="output",
    )(o_b, proj, gates, gates, y_a, x2d, w_proj_a, w_proj_b, w_out, g_post)


def _rel_bucket(dist):
    max_exact = REL_BUCKETS // 2
    d = jnp.maximum(dist, 1).astype(F32)
    large = max_exact + (jnp.log(d / max_exact) / math.log(REL_MAX_DIST / max_exact)
                         * (REL_BUCKETS - max_exact)).astype(jnp.int32)
    large = jnp.minimum(large, REL_BUCKETS - 1)
    return jnp.where(dist < max_exact, dist, large)


def _group_biases(rel_table):
    out = []
    for gi, (win, dil) in enumerate(DIL_GROUPS):
        dist = jnp.arange(win // dil + 1, dtype=jnp.int32) * dil
        b = rel_table[_rel_bucket(dist)][:, gi * DIL_HEADS:(gi + 1) * DIL_HEADS]
        out.append(b.T.astype(F32))
    return out


def _bias_tiles(biases):
    period = 3 * WIN_BLK
    tiles = []
    for b in biases:
        u = jnp.concatenate([b[:, ::-1], jnp.full((DIL_HEADS, period - WIN_BLK - 1), NEG, F32)], axis=1)
        flat = jnp.tile(u, (1, WIN_BLK))[:, :WIN_BLK * (period - 1)]
        tiles.append(flat.reshape(DIL_HEADS, WIN_BLK, period - 1)[:, :, :2 * WIN_BLK])
    return jnp.stack(tiles, axis=0).astype(F32)


WEIGHT_PREP_ROWS = 128


def _weight_prep_kernel(w_ref, main_ref, ab_ref):
    dst = 0
    for lo, hi in ((SRC_QKV_A, SRC_Z_A), (SRC_QKV_B, SRC_Z_B), (SRC_Z_B, SRC_GATE_A),
                   (SRC_Z_A, SRC_AB), (SRC_GATE_A, SRC_GATE_B), (SRC_GATE_B, SRC_END)):
        main_ref[:, dst:dst + hi - lo] = w_ref[:, lo:hi].astype(BF16)
        dst += hi - lo
    ab = w_ref[:, SRC_AB:SRC_QKV_B].astype(BF16)
    ab_ref[...] = jnp.concatenate([ab, jnp.zeros((ab.shape[0], LANES - ab.shape[1]), BF16)], axis=1)


def _weight_prep(w_in):
    rows = WEIGHT_PREP_ROWS
    return pl.pallas_call(
        _weight_prep_kernel,
        grid=(D_MODEL // rows,),
        in_specs=[pl.BlockSpec((rows, SRC_END), lambda i: (i, 0))],
        out_specs=[pl.BlockSpec((rows, MAIN_WIDTH), lambda i: (i, 0)),
                   pl.BlockSpec((rows, LANES), lambda i: (i, 0))],
        out_shape=[jax.ShapeDtypeStruct((D_MODEL, MAIN_WIDTH), BF16),
                   jax.ShapeDtypeStruct((D_MODEL, LANES), BF16)],
        compiler_params=pltpu.CompilerParams(
            dimension_semantics=("parallel",), vmem_limit_bytes=VMEM_LIMIT),
        name="weight_prep",
    )(w_in)


def _layer_weights(w_in, conv_w, a_log, dt_bias, g_head_norm, w_proj_a, w_proj_b, w_out):
    w_main, w_ab = _weight_prep(w_in)
    pad = (0, LANES - GDN_HEADS)
    return dict(
        w_main=w_main, w_ab=w_ab, conv_wt=conv_w.T.astype(F32),
        alog_pad=jnp.pad(a_log.astype(F32), pad)[None, :],
        dtb_pad=jnp.pad(dt_bias.astype(F32), pad)[None, :],
        ghn=g_head_norm.astype(F32)[None, :],
        w_proj_a=w_proj_a.astype(BF16), w_proj_b=w_proj_b.astype(BF16), w_out=w_out.astype(BF16))


def _window_rows_kernel(k_ref, v_ref, ko_ref, vo_ref):
    rows = k_ref.shape[0]
    for src, dst in ((k_ref, ko_ref), (v_ref, vo_ref)):
        for h in range(DIL_HEADS):
            dst[pl.ds(h, rows, stride=DIL_HEADS), :] = src[:, h * HEAD_DIM:(h + 1) * HEAD_DIM]


def _window_rows(proj3, col_k, keep):
    batch, seq, _ = proj3.shape
    rows = min(keep, 1024)
    first = (seq - keep) // rows
    out = jax.ShapeDtypeStruct((batch, keep * DIL_HEADS, HEAD_DIM), F32)
    k_out, v_out = pl.pallas_call(
        _window_rows_kernel,
        grid=(batch, keep // rows),
        in_specs=[
            pl.BlockSpec((None, rows, DIL_W), lambda b, r: (b, first + r, col_k // DIL_W)),
            pl.BlockSpec((None, rows, DIL_W), lambda b, r: (b, first + r, col_k // DIL_W + 1)),
        ],
        out_specs=[
            pl.BlockSpec((None, rows * DIL_HEADS, HEAD_DIM), lambda b, r: (b, r, 0)),
            pl.BlockSpec((None, rows * DIL_HEADS, HEAD_DIM), lambda b, r: (b, r, 0)),
        ],
        out_shape=[out, out],
        compiler_params=pltpu.CompilerParams(
            dimension_semantics=("parallel", "parallel"), vmem_limit_bytes=VMEM_LIMIT),
        name="window_rows",
    )(proj3, proj3)
    shape = (batch, keep, DIL_HEADS, HEAD_DIM)
    return k_out.reshape(shape), v_out.reshape(shape)


def kernel(x_prompt, x_sample, state_gdn, state_conv, cache_k_w128, cache_v_w128, cache_k_w512,
           cache_v_w512, cache_k_w2048, cache_v_w2048, rel_table, g_pre, w_in, conv_w, a_log, dt_bias,
           g_head_norm, w_proj_a, w_proj_b, w_out, g_post):
    batch, seq, _ = x_prompt.shape
    dec_batch = x_sample.shape[0]
    k_caches = (cache_k_w128, cache_k_w512, cache_k_w2048)
    v_caches = (cache_v_w128, cache_v_w512, cache_v_w2048)
    biases = _group_biases(rel_table)
    bias_tiles = _bias_tiles(biases) * LOG2_E
    lw = _layer_weights(w_in[0], conv_w[0], a_log[0], dt_bias[0], g_head_norm[0], w_proj_a[0],
                        w_proj_b[0], w_out[0])
    g_pre_l = g_pre[0][None, :]
    g_post_l = g_post[0][None, :]

    xp = x_prompt.reshape(batch * seq, D_MODEL)
    proj, gates, ab = _in_projection(xp, g_pre_l, lw["w_main"], lw["w_ab"], tm=INPROJ_ROWS, tn=INPROJ_COLS)
    y_a, p_gdn = _gdn_prompt(proj, gates, ab, lw["conv_wt"], lw["alog_pad"], lw["dtb_pad"], lw["ghn"],
                             batch, seq)
    o_b = _attn_prompt(proj, bias_tiles, batch, seq)
    y_prompt = _output(o_b, proj, gates, y_a, xp, lw["w_proj_a"], lw["w_proj_b"], lw["w_out"], g_post_l,
                       tm=OUTPUT_ROWS)
    y_prompt = y_prompt.reshape(batch, seq, D_MODEL)
    proj3 = proj.reshape(batch, seq, WIDE_WIDTH)
    p_conv = proj3[:, seq - (CONV_WIDTH - 1):, :CONV_DIM]
    p_rows = []
    for gi, (win, _) in enumerate(DIL_GROUPS):
        keep = min(win, seq)
        ck = COL_QKV_B + gi * 3 * DIL_W + DIL_W
        p_rows += list(_window_rows(proj3, ck, keep))

    xs = x_sample.reshape(dec_batch, D_MODEL)
    proj_s, gates_s, ab_s = _in_projection(xs, g_pre_l, lw["w_main"], lw["w_ab"], tm=dec_batch,
                                           tn=INPROJ_COLS)
    sc_t = jnp.transpose(state_conv[0], (1, 0, 2))
    o_s, s_gdn = _gdn_sample(sc_t, proj_s, ab_s, lw["conv_wt"], lw["alog_pad"], lw["dtb_pad"],
                             state_gdn[0], nb=GDN_SAMPLE_SEQS)
    y_a_s = _gdn_sample_out(o_s, gates_s, lw["ghn"])
    q_g, kn_g, vn_g = [], [], []
    for gi in range(N_GROUPS):
        cq = COL_QKV_B + gi * 3 * DIL_W
        q_g.append(proj_s[:, cq:cq + DIL_W].reshape(dec_batch, DIL_HEADS, HEAD_DIM))
        kn_g.append(proj_s[:, cq + DIL_W:cq + 2 * DIL_W].reshape(dec_batch, DIL_HEADS, HEAD_DIM))
        vn_g.append(proj_s[:, cq + 2 * DIL_W:cq + 3 * DIL_W].reshape(dec_batch, DIL_HEADS, HEAD_DIM))
    bias_all = jnp.stack(biases, axis=0)
    bias0 = bias_all[:, None, :, 0:1]
    bias_steps = jnp.transpose(bias_all[:, :, :0:-1], (2, 0, 1))[:, :, None, :, None]
    o_b_s = _attn_sample(q_g, kn_g, vn_g, k_caches, v_caches, bias0, bias_steps)
    y_sample = _output(o_b_s.reshape(dec_batch, DIL_W), proj_s, gates_s, y_a_s, xs, lw["w_proj_a"],
                       lw["w_proj_b"], lw["w_out"], g_post_l, tm=dec_batch)
    y_sample = y_sample.reshape(dec_batch, 1, D_MODEL)
    s_conv = jnp.concatenate([state_conv[0][:, 1:, :], proj_s[:, None, :CONV_DIM]], axis=1)
    s_rows = []
    for gi in range(N_GROUPS):
        s_rows += [kn_g[gi][:, None], vn_g[gi][:, None]]

    return (y_prompt, y_sample, p_gdn[None], p_conv[None], *[r[None] for r in p_rows],
            s_gdn[None], s_conv[None], *[r[None] for r in s_rows])
```

```python
import functools
import math

import jax
import jax.numpy as jnp
from jax import lax
from jax.experimental import pallas as pl
from jax.experimental.pallas import tpu as pltpu

F32 = jnp.float32
BF16 = jnp.bfloat16

D_MODEL = 1024
GDN_HEADS = 8
HEAD_DIM = 128
GDN_QK = GDN_HEADS * HEAD_DIM
CONV_DIM = 3 * GDN_QK
CONV_WIDTH = 4
DIL_GROUPS = ((128, 1), (512, 4), (2048, 16))
N_GROUPS = 3
DIL_HEADS = 4
DIL_W = DIL_HEADS * HEAD_DIM
WIN_BLK = 128
REL_BUCKETS = 32
REL_MAX_DIST = 2048
RMS_EPS = 1e-6
NEG = -1e30
LOG2_E = math.log2(math.e)

COL_QKV_A = 0
COL_QKV_B = 3072
COL_Z_B = 7680
WIDE_WIDTH = 8192
COL_Z_A = 0
COL_GATE_A = 1024
COL_GATE_B = 2048
GATE_WIDTH = 3072
MAIN_WIDTH = WIDE_WIDTH + GATE_WIDTH
SRC_QKV_A, SRC_Z_A, SRC_AB, SRC_QKV_B, SRC_Z_B, SRC_GATE_A, SRC_GATE_B, SRC_END = (
    0, 3072, 4096, 4112, 8720, 9232, 10256, 11280)
LANES = 128

VMEM_LIMIT = 56 * 1024 * 1024
BIG_VMEM_LIMIT = 60 * 1024 * 1024

INPROJ_ROWS = 2048
INPROJ_COLS = 1024
OUTPUT_ROWS = 1024
GDN_CHUNK = 64
GDN_SEQS = 8
GDN_SAMPLE_SEQS = 32
ATTN_TILE = 2048
ATTN_UNITS = 16
MAX_ROW_STRIDE = 4


def _silu(x):
    half = 0.5 * x
    return half * jnp.tanh(half) + half


def _sigmoid(x):
    return 1.0 / (1.0 + jnp.exp(-x))


def _softplus(x):
    return jnp.maximum(x, 0.0) + jnp.log(1.0 + jnp.exp(-jnp.abs(x)))


def _dot(a, b):
    return jnp.dot(a.astype(BF16), b.astype(BF16), preferred_element_type=F32)


def _dot_tn(a, b):
    return lax.dot_general(a.astype(BF16), b.astype(BF16), (((0,), (0,)), ((), ())),
                           preferred_element_type=F32)


def _inproj_kernel(x_ref, g_ref, w_ref, wab_ref, wide_ref, gate_ref, ab_ref, xn_ref, *, wide_tiles):
    j = pl.program_id(1)

    @pl.when(j == 0)
    def _():
        x = x_ref[...]
        ms = jnp.mean(x * x, axis=-1, keepdims=True)
        xn = (x * lax.rsqrt(ms + RMS_EPS) * g_ref[...]).astype(BF16)
        xn_ref[...] = xn
        ab_ref[...] = jnp.dot(xn, wab_ref[...], preferred_element_type=F32)

    @pl.when(j < wide_tiles)
    def _():
        wide_ref[...] = jnp.dot(xn_ref[...], w_ref[...], preferred_element_type=F32)

    @pl.when(j >= wide_tiles)
    def _():
        gate_ref[...] = jnp.dot(xn_ref[...], w_ref[...], preferred_element_type=F32).astype(gate_ref.dtype)


def _in_projection(x2d, g_pre, w_main, w_ab, tm, tn):
    m = x2d.shape[0]
    wide_tiles = WIDE_WIDTH // tn
    return pl.pallas_call(
        functools.partial(_inproj_kernel, wide_tiles=wide_tiles),
        grid=(m // tm, MAIN_WIDTH // tn),
        in_specs=[
            pl.BlockSpec((tm, D_MODEL), lambda i, j: (i, 0)),
            pl.BlockSpec((1, D_MODEL), lambda i, j: (0, 0)),
            pl.BlockSpec((D_MODEL, tn), lambda i, j: (0, j)),
            pl.BlockSpec((D_MODEL, LANES), lambda i, j: (0, 0)),
        ],
        out_specs=[
            pl.BlockSpec((tm, tn), lambda i, j: (i, jnp.minimum(j, wide_tiles - 1))),
            pl.BlockSpec((tm, tn), lambda i, j: (i, jnp.maximum(j - wide_tiles, 0))),
            pl.BlockSpec((tm, LANES), lambda i, j: (i, 0)),
        ],
        out_shape=[
            jax.ShapeDtypeStruct((m, WIDE_WIDTH), F32),
            jax.ShapeDtypeStruct((m, GATE_WIDTH), BF16),
            jax.ShapeDtypeStruct((m, LANES), F32),
        ],
        scratch_shapes=[pltpu.VMEM((tm, D_MODEL), BF16)],
        compiler_params=pltpu.CompilerParams(
            dimension_semantics=("parallel", "arbitrary"), vmem_limit_bytes=BIG_VMEM_LIMIT),
        name="in_projection",
    )(x2d, g_pre, w_main, w_ab)


GROUP_HEADS = 4


def _block_diag_rhs(x4, half_masks):
    c = x4.shape[0]
    zeros = jnp.zeros((c, LANES), x4.dtype)
    row_blocks = []
    for h in range(GROUP_HEADS):
        piece = x4[:, (h // 2) * LANES:(h // 2 + 1) * LANES] * half_masks[h % 2]
        row_blocks.append(jnp.concatenate([piece if v == h // 2 else zeros for v in range(2)], axis=1))
    return jnp.concatenate(row_blocks, axis=0)


def _gdn_prompt_kernel(qkv_ref, z_ref, ab_ref, convw_ref, alog_ref, dtb_ref, ghn_ref,
                       ya_ref, sfin_ref, xpad_ref, qkvn_ref, o_ref, s_ref,
                       u_ref, w_ref, qd_ref, kd_ref, qk_ref):
    c = GDN_CHUNK
    tl = GDN_SEQS * c
    gw = GROUP_HEADS * c
    step = pl.program_id(1)

    def conv_stage(i):
        for blk in range(CONV_DIM // LANES):
            cols = slice(blk * LANES, (blk + 1) * LANES)
            xpad_ref[i, blk, 8:8 + c, :] = qkv_ref[i, :, cols]
            for phase in range(2):
                acc = None
                for w in range(CONV_WIDTH):
                    tap = xpad_ref[i, blk, pl.ds(5 + phase + w, c // 2, stride=2), :] * convw_ref[w:w + 1, cols]
                    acc = tap if acc is None else acc + tap
                y = _silu(acc)
                if blk < 2 * GDN_HEADS:
                    y = y * lax.rsqrt(jnp.sum(y * y, axis=-1, keepdims=True) + RMS_EPS)
                    if blk < GDN_HEADS:
                        y = y * (HEAD_DIM ** -0.5)
                qkvn_ref[blk, pl.ds(i * c + phase, c // 2, stride=2), :] = y
            xpad_ref[i, blk, 5:8, :] = xpad_ref[i, blk, c + 5:c + 8, :]

    @pl.when(step == 0)
    def _():
        xpad_ref[:, :, 0:8, :] = jnp.zeros((GDN_SEQS, CONV_DIM // LANES, 8, LANES), F32)
        s_ref[...] = jnp.zeros_like(s_ref)

    ab = ab_ref[...].reshape(tl, LANES)
    g_full = -jnp.exp(alog_ref[...]) * _softplus(ab + dtb_ref[...])
    beta_full = _sigmoid(ab)
    row = lax.broadcasted_iota(jnp.int32, (tl, tl), 0)
    col = lax.broadcasted_iota(jnp.int32, (tl, tl), 1)
    same_chunk = (row // c) == (col // c)
    tri = jnp.where(same_chunk & (col <= row), 1.0, 0.0).astype(BF16)
    g_hi = g_full.astype(BF16)
    g_rest = g_full - g_hi.astype(F32)
    g_mid = g_rest.astype(BF16)
    g_lo = (g_rest - g_mid.astype(F32)).astype(BF16)
    g_cum = jnp.dot(jnp.concatenate([tri, tri, tri], axis=1), jnp.concatenate([g_hi, g_mid, g_lo], axis=0),
                    preferred_element_type=F32)
    g_cum_t = g_cum.T
    g_last = jnp.concatenate(
        [jnp.broadcast_to(g_cum[i * c + c - 1:(i + 1) * c, :], (c, LANES)) for i in range(GDN_SEQS)], axis=0)
    exp_g = jnp.exp(g_cum)
    exp_rest = jnp.exp(g_last - g_cum)
    exp_last = jnp.exp(g_last)

    ci = lax.broadcasted_iota(jnp.int32, (c, gw), 0)
    cj = lax.broadcasted_iota(jnp.int32, (c, gw), 1) % c
    causal4 = cj <= ci
    strict4 = cj < ci
    half_lane = lax.broadcasted_iota(jnp.int32, (c, LANES), 1) // c
    half_masks = [jnp.where(half_lane == k, 1.0, 0.0).astype(BF16) for k in range(2)]

    def lanes_of(col_values, width):
        return jnp.concatenate([jnp.broadcast_to(v, (c, width)) for v in col_values], axis=1)

    n_groups = GDN_HEADS // GROUP_HEADS
    zero_s = jnp.zeros((HEAD_DIM, HEAD_DIM), BF16)
    zero_v = jnp.zeros((c, HEAD_DIM), BF16)

    def group_cols(hg):
        return slice(hg * GROUP_HEADS * LANES, (hg + 1) * GROUP_HEADS * LANES)

    def bd_dot(lhs, x4):
        return jnp.dot(lhs.astype(BF16), _block_diag_rhs(x4.astype(BF16), half_masks),
                       preferred_element_type=F32)

    def seq_stages(i):
        rows = slice(i * c, (i + 1) * c)
        st = {}

        def build():
            a4s = []
            for hg in range(n_groups):
                heads = range(hg * GROUP_HEADS, (hg + 1) * GROUP_HEADS)
                kbs, qs, ks = [], [], []
                for h in heads:
                    hc = slice(h * LANES, (h + 1) * LANES)
                    q_h = qkvn_ref[h, rows, :]
                    k_h = qkvn_ref[GDN_HEADS + h, rows, :]
                    v_h = qkvn_ref[2 * GDN_HEADS + h, rows, :]
                    beta_h = beta_full[rows, GDN_HEADS + h:GDN_HEADS + h + 1]
                    eg_h = exp_g[rows, h:h + 1]
                    kb_h = k_h * beta_h
                    u_ref[rows, hc] = v_h * beta_h
                    w_ref[rows, hc] = kb_h * eg_h
                    qd_ref[rows, hc] = q_h * eg_h
                    kd_ref[rows, hc] = k_h * exp_rest[rows, h:h + 1]
                    kbs.append(kb_h.astype(BF16))
                    qs.append(q_h.astype(BF16))
                    ks.append(k_h.astype(BF16))
                gi4 = lanes_of([g_cum[rows, h:h + 1] for h in heads], c)
                gj4 = jnp.concatenate([g_cum_t[h:h + 1, rows] for h in heads], axis=1)
                decay4 = jnp.exp(jnp.where(causal4, gi4 - gj4, NEG))
                k_rhs = jnp.concatenate(
                    [jnp.concatenate([ks[g] if g == h else zero_v for g in range(GROUP_HEADS)], axis=1)
                     for h in range(GROUP_HEADS)], axis=0)
                lhs = jnp.concatenate([jnp.concatenate(kbs, axis=1), jnp.concatenate(qs, axis=1)], axis=0)
                kkqk = lax.dot_general(lhs, k_rhs, (((1,), (1,)), ((), ())),
                                       preferred_element_type=F32)
                a4s.append(jnp.where(strict4, kkqk[:c] * decay4, 0.0))
                qk_ref[rows, hg * gw:(hg + 1) * gw] = kkqk[c:] * decay4
            st["a"] = a4s

        def square():
            st["e"] = [-a4 for a4 in st["a"]]
            st["x"] = [bd_dot(a4, a4) for a4 in st["a"]]

        def double():
            boths = [bd_dot(jnp.concatenate([e, x], axis=0), x) for e, x in zip(st["e"], st["x"])]
            st["e"] = [e + x + both[:c] for e, x, both in zip(st["e"], st["x"], boths)]
            st["x"] = [both[c:] for both in boths]

        def last_factor():
            st["e"] = [e + x + bd_dot(e, x) for e, x in zip(st["e"], st["x"])]

        def solve():
            for hg, e4 in enumerate(st["e"]):
                cols = group_cols(hg)
                vb4 = u_ref[rows, cols]
                kbe4 = w_ref[rows, cols]
                us, ws = [], []
                for k in range(GROUP_HEADS):
                    hc = slice(k * LANES, (k + 1) * LANES)
                    rhs = jnp.concatenate([vb4[:, hc], kbe4[:, hc]], axis=1)
                    uw = rhs + _dot(e4[:, k * c:(k + 1) * c], rhs)
                    us.append(uw[:, :LANES])
                    ws.append(uw[:, LANES:])
                u_ref[rows, cols] = jnp.concatenate(us, axis=1)
                w_ref[rows, cols] = jnp.concatenate(ws, axis=1)

        def pair_cols(p):
            return slice(2 * p * LANES, (2 * p + 2) * LANES)

        def state_products():
            wsqs = []
            for p in range(GDN_HEADS // 2):
                cols = pair_cols(p)
                s_bd = jnp.concatenate(
                    [jnp.concatenate([s_ref[i, 2 * p].astype(BF16), zero_s], axis=1),
                     jnp.concatenate([zero_s, s_ref[i, 2 * p + 1].astype(BF16)], axis=1)], axis=0)
                wq = jnp.concatenate([w_ref[rows, cols], qd_ref[rows, cols]], axis=0).astype(BF16)
                wsqs.append(jnp.dot(wq, s_bd, preferred_element_type=F32))
            st["wsq"] = wsqs

        def outputs():
            vnbs = []
            for p, wsq in enumerate(st["wsq"]):
                cols = pair_cols(p)
                vnb = (u_ref[rows, cols] - wsq[:c]).astype(BF16)
                vnbs.append(vnb)
                vn_bd = jnp.concatenate([jnp.concatenate([vnb[:, :LANES], zero_v], axis=1),
                                         jnp.concatenate([zero_v, vnb[:, LANES:]], axis=1)], axis=0)
                qk2 = qk_ref[rows, 2 * p * c:(2 * p + 2) * c]
                o_ref[rows, cols] = wsq[c:] + jnp.dot(qk2.astype(BF16), vn_bd, preferred_element_type=F32)
            st["vn"] = vnbs

        def state_update():
            for p, vnb in enumerate(st["vn"]):
                kd2 = kd_ref[rows, pair_cols(p)]
                for k in range(2):
                    h = 2 * p + k
                    hc = slice(k * LANES, (k + 1) * LANES)
                    el = exp_last[i * c:i * c + 1, h:h + 1]
                    s_ref[i, h] = s_ref[i, h] * el + _dot_tn(kd2[:, hc], vnb[:, hc])

        doublings = int(math.log2(c)) - 2
        return ([functools.partial(conv_stage, i), build, square] + [double] * doublings
                + [last_factor, solve, state_products, outputs, state_update])

    for stages in zip(*[seq_stages(i) for i in range(GDN_SEQS)]):
        for stage in stages:
            stage()

    for h in range(GDN_HEADS):
        cols = slice(h * LANES, (h + 1) * LANES)
        o = o_ref[:, cols]
        o = o * lax.rsqrt(jnp.mean(o * o, axis=-1, keepdims=True) + RMS_EPS) * ghn_ref[...]
        y = o * _silu(z_ref[:, :, cols].astype(F32).reshape(tl, LANES))
        ya_ref[:, :, cols] = y.reshape(GDN_SEQS, c, LANES).astype(ya_ref.dtype)

    @pl.when(step == pl.num_programs(1) - 1)
    def _():
        sfin_ref[...] = s_ref[...]


def _gdn_prompt(proj, gates, ab, conv_wt, alog_pad, dtb_pad, ghn, batch, seq):
    c = GDN_CHUNK
    ns = GDN_SEQS
    tl = ns * c
    proj3 = proj.reshape(batch, seq, WIDE_WIDTH)
    gates3 = gates.reshape(batch, seq, GATE_WIDTH)
    ab3 = ab.reshape(batch, seq, LANES)
    y_a, s_fin = pl.pallas_call(
        _gdn_prompt_kernel,
        grid=(batch // ns, seq // c),
        in_specs=[
            pl.BlockSpec((ns, c, CONV_DIM), lambda b, l: (b, l, COL_QKV_A // CONV_DIM)),
            pl.BlockSpec((ns, c, GDN_QK), lambda b, l: (b, l, COL_Z_A // GDN_QK)),
            pl.BlockSpec((ns, c, LANES), lambda b, l: (b, l, 0)),
            pl.BlockSpec((CONV_WIDTH, CONV_DIM), lambda b, l: (0, 0)),
            pl.BlockSpec((1, LANES), lambda b, l: (0, 0)),
            pl.BlockSpec((1, LANES), lambda b, l: (0, 0)),
            pl.BlockSpec((1, LANES), lambda b, l: (0, 0)),
        ],
        out_specs=[
            pl.BlockSpec((ns, c, D_MODEL), lambda b, l: (b, l, 0)),
            pl.BlockSpec((ns, GDN_HEADS, HEAD_DIM, HEAD_DIM), lambda b, l: (b, 0, 0, 0)),
        ],
        out_shape=[
            jax.ShapeDtypeStruct((batch, seq, D_MODEL), BF16),
            jax.ShapeDtypeStruct((batch, GDN_HEADS, HEAD_DIM, HEAD_DIM), F32),
        ],
        scratch_shapes=[
            pltpu.VMEM((ns, CONV_DIM // LANES, c + 8, LANES), F32),
            pltpu.VMEM((CONV_DIM // LANES, tl, LANES), F32),
            pltpu.VMEM((tl, GDN_QK), F32),
            pltpu.VMEM((ns, GDN_HEADS, HEAD_DIM, HEAD_DIM), F32),
            pltpu.VMEM((tl, GDN_QK), F32),
            pltpu.VMEM((tl, GDN_QK), F32),
            pltpu.VMEM((tl, GDN_QK), F32),
            pltpu.VMEM((tl, GDN_QK), F32),
            pltpu.VMEM((tl, GDN_HEADS * GDN_CHUNK), F32),
        ],
        compiler_params=pltpu.CompilerParams(
            dimension_semantics=("parallel", "arbitrary"), vmem_limit_bytes=BIG_VMEM_LIMIT),
        name="gdn_prompt",
    )(proj3, gates3, ab3, conv_wt, alog_pad, dtb_pad, ghn)
    return y_a.reshape(batch * seq, GDN_QK), s_fin


def _gdn_sample_kernel(sc_q_ref, sc_k_ref, sc_v_ref, nq_ref, nk_ref, nv_ref, cwq_ref, cwk_ref, cwv_ref,
                       ab_ref, alog_ref, dtb_ref, s_ref, o_ref, snew_ref):
    h = pl.program_id(0)
    nb = nq_ref.shape[0]

    def conv_silu(sc_ref, new_ref, cw_ref):
        acc = new_ref[...] * cw_ref[3:4, :]
        for w in range(CONV_WIDTH - 1):
            acc = acc + sc_ref[w] * cw_ref[w:w + 1, :]
        return _silu(acc)

    q = conv_silu(sc_q_ref, nq_ref, cwq_ref)
    k = conv_silu(sc_k_ref, nk_ref, cwk_ref)
    v = conv_silu(sc_v_ref, nv_ref, cwv_ref)
    q = q * lax.rsqrt(jnp.sum(q * q, axis=-1, keepdims=True) + RMS_EPS) * (HEAD_DIM ** -0.5)
    k = k * lax.rsqrt(jnp.sum(k * k, axis=-1, keepdims=True) + RMS_EPS)

    ab = ab_ref[...]
    lane = lax.broadcasted_iota(jnp.int32, ab.shape, 1)
    g_full = -jnp.exp(alog_ref[...]) * _softplus(ab + dtb_ref[...])
    g = jnp.sum(jnp.where(lane == h, g_full, 0.0), axis=-1, keepdims=True)
    beta = jnp.sum(jnp.where(lane == h + GDN_HEADS, _sigmoid(ab), 0.0), axis=-1, keepdims=True)
    eg = jnp.exp(g)

    eye = (lax.broadcasted_iota(jnp.int32, (HEAD_DIM, HEAD_DIM), 0)
           == lax.broadcasted_iota(jnp.int32, (HEAD_DIM, HEAD_DIM), 1))

    def as_column(row):
        return jnp.sum(jnp.where(eye, row, 0.0), axis=1, keepdims=True)

    for b in range(nb):
        s = s_ref[b]
        k_col = as_column(k[b:b + 1, :])
        q_col = as_column(q[b:b + 1, :])
        e = eg[b:b + 1, :]
        ks = jnp.sum(k_col * s, axis=0, keepdims=True)
        v_new = beta[b:b + 1, :] * (v[b:b + 1, :] - e * ks)
        s_new = s * e + k_col * v_new
        snew_ref[b] = s_new
        o_ref[b:b + 1, :] = jnp.sum(q_col * s_new, axis=0, keepdims=True)


def _gdn_sample(state_conv_t, proj, ab, conv_wt, alog_pad, dtb_pad, state, nb):
    batch = proj.shape[0]
    hq = GDN_QK // LANES
    return pl.pallas_call(
        _gdn_sample_kernel,
        grid=(GDN_HEADS, batch // nb),
        in_specs=[
            pl.BlockSpec((CONV_WIDTH - 1, nb, LANES), lambda h, i: (0, i, h)),
            pl.BlockSpec((CONV_WIDTH - 1, nb, LANES), lambda h, i: (0, i, hq + h)),
            pl.BlockSpec((CONV_WIDTH - 1, nb, LANES), lambda h, i: (0, i, 2 * hq + h)),
            pl.BlockSpec((nb, LANES), lambda h, i: (i, h)),
            pl.BlockSpec((nb, LANES), lambda h, i: (i, hq + h)),
            pl.BlockSpec((nb, LANES), lambda h, i: (i, 2 * hq + h)),
            pl.BlockSpec((CONV_WIDTH, LANES), lambda h, i: (0, h)),
            pl.BlockSpec((CONV_WIDTH, LANES), lambda h, i: (0, hq + h)),
            pl.BlockSpec((CONV_WIDTH, LANES), lambda h, i: (0, 2 * hq + h)),
            pl.BlockSpec((nb, LANES), lambda h, i: (i, 0)),
            pl.BlockSpec((1, LANES), lambda h, i: (0, 0)),
            pl.BlockSpec((1, LANES), lambda h, i: (0, 0)),
            pl.BlockSpec((nb, None, HEAD_DIM, HEAD_DIM), lambda h, i: (i, h, 0, 0)),
        ],
        out_specs=[
            pl.BlockSpec((nb, LANES), lambda h, i: (i, h)),
            pl.BlockSpec((nb, None, HEAD_DIM, HEAD_DIM), lambda h, i: (i, h, 0, 0)),
        ],
        out_shape=[
            jax.ShapeDtypeStruct((batch, GDN_QK), F32),
            jax.ShapeDtypeStruct(state.shape, F32),
        ],
        compiler_params=pltpu.CompilerParams(
            dimension_semantics=("parallel", "parallel"), vmem_limit_bytes=VMEM_LIMIT),
        name="gdn_sample",
    )(state_conv_t, state_conv_t, state_conv_t, proj, proj, proj, conv_wt, conv_wt, conv_wt,
      ab, alog_pad, dtb_pad, state)


def _gdn_sample_out_kernel(o_ref, z_ref, ghn_ref, ya_ref):
    for h in range(GDN_HEADS):
        cols = slice(h * LANES, (h + 1) * LANES)
        o = o_ref[:, cols]
        o = o * lax.rsqrt(jnp.mean(o * o, axis=-1, keepdims=True) + RMS_EPS) * ghn_ref[...]
        ya_ref[:, cols] = (o * _silu(z_ref[:, cols].astype(F32))).astype(ya_ref.dtype)


def _gdn_sample_out(o, gates, ghn):
    batch = o.shape[0]
    return pl.pallas_call(
        _gdn_sample_out_kernel,
        grid=(1,),
        in_specs=[
            pl.BlockSpec((batch, GDN_QK), lambda i: (0, 0)),
            pl.BlockSpec((batch, GDN_QK), lambda i: (0, COL_Z_A // GDN_QK)),
            pl.BlockSpec((1, LANES), lambda i: (0, 0)),
        ],
        out_specs=pl.BlockSpec((batch, GDN_QK), lambda i: (0, 0)),
        out_shape=jax.ShapeDtypeStruct((batch, GDN_QK), BF16),
        compiler_params=pltpu.CompilerParams(vmem_limit_bytes=VMEM_LIMIT),
        name="gdn_sample_out",
    )(o, gates, ghn)


def _attn_prompt_kernel(*refs):
    ins = refs[:5 * N_GROUPS]
    bias_ref = refs[5 * N_GROUPS]
    o_ref = refs[5 * N_GROUPS + 1]
    acc_ref, m_ref, l_ref, bias_start_ref, split_ref, merged_ref = refs[5 * N_GROUPS + 2:]
    tq = ATTN_TILE
    first_tile = pl.program_id(1) == 0
    scale = (HEAD_DIM ** -0.5) * LOG2_E
    lane2 = lax.broadcasted_iota(jnp.int32, (1, 2 * WIN_BLK), 1)
    start_mask = jnp.where(first_tile & (lane2 < WIN_BLK), NEG, 0.0).astype(F32)
    for gi in range(N_GROUPS):
        bias_start_ref[gi] = bias_ref[gi] + start_mask

    def strided(start, n, stride):
        return pl.ds(start, n, stride=stride) if stride > 1 else pl.ds(start, n)

    for gi, (_, dil) in enumerate(DIL_GROUPS):
        srcs = ins[5 * gi:5 * gi + 5]
        n_blocks = tq // (dil * WIN_BLK)
        inner = min(dil, MAX_ROW_STRIDE)
        outer = dil // inner
        if outer > 1:
            for a, src in enumerate(srcs):
                n = src.shape[0]
                for p in range(outer):
                    split_ref[a, p * (n // outer):(p + 1) * (n // outer), :] = src[strided(p, n // outer, outer), :]
            srcs = [split_ref.at[a] for a in range(5)]
        q_ref, kp_ref, kc_ref, vp_ref, vc_ref = srcs

        def rows_of(ref_rows, r, first, count, inner=inner, outer=outer):
            if outer == 1:
                return strided(r + inner * first, count, inner)
            part = r % outer
            return strided(part * (ref_rows // outer) + r // outer + inner * first, count, inner)

        def run_units(units, q_ref=q_ref, kp_ref=kp_ref, kc_ref=kc_ref, vp_ref=vp_ref, vc_ref=vc_ref,
                      gi=gi, rows_of=rows_of):
            span = kp_ref.shape[0]
            n = len(units)
            st = [{} for _ in units]

            def logits(u):
                r, j = units[u]
                q = (q_ref[rows_of(tq, r, WIN_BLK * j, WIN_BLK), :] * scale).astype(BF16)
                if j == 0:
                    prev = rows_of(span, r, 0, WIN_BLK)
                    cur = rows_of(tq, r, 0, WIN_BLK)
                    kk = jnp.concatenate([kp_ref[prev, :], kc_ref[cur, :]], axis=0).astype(BF16)
                    st[u]["v"] = (vp_ref, prev, vc_ref, cur)
                else:
                    rows = rows_of(tq, r, WIN_BLK * (j - 1), 2 * WIN_BLK)
                    kk = kc_ref[rows, :].astype(BF16)
                    st[u]["v"] = (vc_ref, rows)
                st[u]["s"] = lax.dot_general(q, kk, (((1,), (1,)), ((), ())), preferred_element_type=F32)

            def softmax(u):
                r, j = units[u]
                s = st[u].pop("s") + (bias_start_ref[gi] if j == 0 else bias_ref[gi])
                m = jnp.max(s, axis=-1, keepdims=True)
                p = jnp.exp2(s - m)
                st[u]["m"] = m
                st[u]["l"] = jnp.sum(p, axis=-1, keepdims=True)
                st[u]["p"] = p.astype(BF16)

            def values(u):
                r, j = units[u]
                v = st[u].pop("v")
                if len(v) == 4:
                    vv = jnp.concatenate([v[0][v[1], :], v[2][v[3], :]], axis=0).astype(BF16)
                else:
                    vv = v[0][v[1], :].astype(BF16)
                acc = jnp.dot(st[u].pop("p"), vv, preferred_element_type=F32)
                q_rows = rows_of(tq, r, WIN_BLK * j, WIN_BLK)
                acc_ref[gi, q_rows, :] = acc
                m_ref[gi, q_rows, :] = jnp.broadcast_to(st[u].pop("m"), (WIN_BLK, LANES))
                l_ref[gi, q_rows, :] = jnp.broadcast_to(st[u].pop("l"), (WIN_BLK, LANES))

            for stage in (logits, softmax, values):
                for u in range(n):
                    stage(u)

        if dil == 1:
            for j0 in range(0, n_blocks, ATTN_UNITS):
                run_units([(0, j) for j in range(j0, min(j0 + ATTN_UNITS, n_blocks))])
        else:
            jb = min(n_blocks, ATTN_UNITS)
            rb = max(ATTN_UNITS // n_blocks, 1)
            for j0 in range(0, n_blocks, jb):
                def body(it, carry, j0=j0, jb=jb, rb=rb, run_units=run_units):
                    run_units([(it * rb + k, j) for k in range(rb) for j in range(j0, j0 + jb)])
                    return carry
                for it in range(dil // rb):
                    body(it, 0)

    outers = [dil // min(dil, MAX_ROW_STRIDE) for _, dil in DIL_GROUPS]
    parts = max(outers)
    cnt = 2 * WIN_BLK
    for p in range(parts):
        for i0 in range(0, tq // parts, cnt):
            def rows_in(gi):
                if outers[gi] == parts:
                    return pl.ds(p * (tq // parts) + i0, cnt)
                assert outers[gi] == 1
                return strided(p + parts * i0, cnt, parts)
            m_all = jnp.maximum(jnp.maximum(m_ref[0, rows_in(0), :], m_ref[1, rows_in(1), :]),
                                m_ref[2, rows_in(2), :])
            num = jnp.zeros((cnt, LANES), F32)
            den = jnp.zeros((cnt, LANES), F32)
            for gi in range(N_GROUPS):
                wgt = jnp.exp2(m_ref[gi, rows_in(gi), :] - m_all)
                num = num + wgt * acc_ref[gi, rows_in(gi), :]
                den = den + wgt * l_ref[gi, rows_in(gi), :]
            merged_ref[strided(p + parts * i0, cnt, parts), :] = num / den
    o_ref[...] = merged_ref[...].astype(o_ref.dtype)


def _attn_prompt(proj, bias_tiles, batch, seq):
    tq = ATTN_TILE
    nt = seq // tq
    in_specs = []
    args = []
    for gi, (_, dil) in enumerate(DIL_GROUPS):
        span = dil * WIN_BLK
        cq = (COL_QKV_B + gi * 3 * DIL_W) // LANES
        ck = cq + DIL_HEADS
        cv = ck + DIL_HEADS

        def cur(col):
            return pl.BlockSpec((tq, LANES), lambda b, i, h, col=col: (b * nt + i, col + h))

        def prev(col, span=span):
            per = tq // span
            return pl.BlockSpec(
                (span, LANES),
                lambda b, i, h, col=col, per=per: (jnp.maximum((b * nt + i) * per - 1, 0), col + h))

        in_specs += [cur(cq), prev(ck), cur(ck), prev(cv), cur(cv)]
        args += [proj] * 5
    in_specs.append(pl.BlockSpec((N_GROUPS, None, WIN_BLK, 2 * WIN_BLK), lambda b, i, h: (0, h, 0, 0)))
    args.append(bias_tiles)
    return pl.pallas_call(
        _attn_prompt_kernel,
        grid=(batch, nt, DIL_HEADS),
        in_specs=in_specs,
        out_specs=pl.BlockSpec((tq, LANES), lambda b, i, h: (b * nt + i, h)),
        out_shape=jax.ShapeDtypeStruct((batch * seq, DIL_W), BF16),
        scratch_shapes=[pltpu.VMEM((N_GROUPS, tq, LANES), F32)] * 3
        + [pltpu.VMEM((N_GROUPS, WIN_BLK, 2 * WIN_BLK), F32), pltpu.VMEM((5, tq, LANES), F32),
           pltpu.VMEM((tq, LANES), F32)],
        compiler_params=pltpu.CompilerParams(
            dimension_semantics=("parallel", "parallel", "parallel"), vmem_limit_bytes=VMEM_LIMIT),
        name="attn_prompt",
    )(*args)


def _attn_sample_kernel(*refs):
    q_refs = refs[0:N_GROUPS]
    kn_refs = refs[N_GROUPS:2 * N_GROUPS]
    vn_refs = refs[2 * N_GROUPS:3 * N_GROUPS]
    kc_refs = refs[3 * N_GROUPS:4 * N_GROUPS]
    vc_refs = refs[4 * N_GROUPS:5 * N_GROUPS]
    bias0_ref, bias_ref, o_ref = refs[5 * N_GROUPS:5 * N_GROUPS + 3]
    m_ref, l_ref, acc_ref = refs[5 * N_GROUPS + 3:]
    t = pl.program_id(0)
    scale = HEAD_DIM ** -0.5

    def logits(gi, k, bias):
        s = jnp.sum(q_refs[gi][...] * k, axis=-1, keepdims=True) * scale + bias
        return jnp.broadcast_to(s, k.shape)

    @pl.when(t == 0)
    def _():
        for gi in range(N_GROUPS):
            s = logits(gi, kn_refs[gi][...], bias0_ref[gi])
            m_ref[gi] = s
            l_ref[gi] = jnp.ones_like(s)
            acc_ref[gi] = vn_refs[gi][...]

    for gi in range(N_GROUPS):
        s = logits(gi, kc_refs[gi][...], bias_ref[gi])
        m_old = m_ref[gi]
        m_new = jnp.maximum(m_old, s)
        alpha = jnp.exp(m_old - m_new)
        p = jnp.exp(s - m_new)
        l_ref[gi] = alpha * l_ref[gi] + p
        acc_ref[gi] = alpha * acc_ref[gi] + p * vc_refs[gi][...]
        m_ref[gi] = m_new

    @pl.when(t == pl.num_programs(0) - 1)
    def _():
        m_all = jnp.maximum(jnp.maximum(m_ref[0], m_ref[1]), m_ref[2])
        num = jnp.zeros(acc_ref.shape[1:], F32)
        den = jnp.zeros(m_ref.shape[1:], F32)
        for gi in range(N_GROUPS):
            wgt = jnp.exp(m_ref[gi] - m_all)
            num = num + wgt * acc_ref[gi]
            den = den + wgt * l_ref[gi]
        o_ref[...] = num / den


def _attn_sample(q_g, kn_g, vn_g, k_caches, v_caches, bias0, bias_steps):
    batch = q_g[0].shape[0]
    small = pl.BlockSpec((batch, DIL_HEADS, HEAD_DIM), lambda t: (0, 0, 0))
    in_specs = [small] * (3 * N_GROUPS)
    for _ in range(2):
        for _, dil in DIL_GROUPS:
            in_specs.append(pl.BlockSpec((None, batch, None, DIL_HEADS, HEAD_DIM),
                                         lambda t, dil=dil: (0, 0, t * dil, 0, 0)))
    in_specs.append(pl.BlockSpec((N_GROUPS, 1, DIL_HEADS, 1), lambda t: (0, 0, 0, 0)))
    in_specs.append(pl.BlockSpec((None, N_GROUPS, 1, DIL_HEADS, 1), lambda t: (t, 0, 0, 0, 0)))
    return pl.pallas_call(
        _attn_sample_kernel,
        grid=(WIN_BLK,),
        in_specs=in_specs,
        out_specs=pl.BlockSpec((batch, DIL_HEADS, HEAD_DIM), lambda t: (0, 0, 0)),
        out_shape=jax.ShapeDtypeStruct((batch, DIL_HEADS, HEAD_DIM), F32),
        scratch_shapes=[
            pltpu.VMEM((N_GROUPS, batch, DIL_HEADS, HEAD_DIM), F32),
            pltpu.VMEM((N_GROUPS, batch, DIL_HEADS, HEAD_DIM), F32),
            pltpu.VMEM((N_GROUPS, batch, DIL_HEADS, HEAD_DIM), F32),
        ],
        compiler_params=pltpu.CompilerParams(
            dimension_semantics=("arbitrary",), vmem_limit_bytes=VMEM_LIMIT),
        name="attn_sample",
    )(*q_g, *kn_g, *vn_g, *k_caches, *v_caches, bias0, bias_steps)


def _output_kernel(ob_ref, zb_ref, gatea_ref, gateb_ref, ya_ref, x_ref, wpa_ref, wpb_ref, wout_ref, gpost_ref,
                   y_ref):
    p_a = jnp.dot(ya_ref[...], wpa_ref[...], preferred_element_type=F32)
    yb = (ob_ref[...].astype(F32) * _silu(zb_ref[...])).astype(BF16)
    p_b = jnp.dot(yb, wpb_ref[...], preferred_element_type=F32)
    hid = _sigmoid(gatea_ref[...].astype(F32)) * p_a + _sigmoid(gateb_ref[...].astype(F32)) * p_b
    out = jnp.dot(hid.astype(BF16), wout_ref[...], preferred_element_type=F32)
    ms = jnp.mean(out * out, axis=-1, keepdims=True)
    y_ref[...] = x_ref[...] + out * lax.rsqrt(ms + RMS_EPS) * gpost_ref[...]


def _output(o_b, proj, gates, y_a, x2d, w_proj_a, w_proj_b, w_out, g_post, tm):
    m = x2d.shape[0]
    return pl.pallas_call(
        _output_kernel,
        grid=(m // tm,),
        in_specs=[
            pl.BlockSpec((tm, DIL_W), lambda i: (i, 0)),
            pl.BlockSpec((tm, DIL_W), lambda i: (i, COL_Z_B // DIL_W)),
            pl.BlockSpec((tm, D_MODEL), lambda i: (i, COL_GATE_A // D_MODEL)),
            pl.BlockSpec((tm, D_MODEL), lambda i: (i, COL_GATE_B // D_MODEL)),
            pl.BlockSpec((tm, GDN_QK), lambda i: (i, 0)),
            pl.BlockSpec((tm, D_MODEL), lambda i: (i, 0)),
            pl.BlockSpec((GDN_QK, D_MODEL), lambda i: (0, 0)),
            pl.BlockSpec((DIL_W, D_MODEL), lambda i: (0, 0)),
            pl.BlockSpec((D_MODEL, D_MODEL), lambda i: (0, 0)),
            pl.BlockSpec((1, D_MODEL), lambda i: (0, 0)),
        ],
        out_specs=pl.BlockSpec((tm, D_MODEL), lambda i: (i, 0)),
        out_shape=jax.ShapeDtypeStruct((m, D_MODEL), F32),
        compiler_params=pltpu.CompilerParams(
            dimension_semantics=("parallel",), vmem_limit_bytes=VMEM_LIMIT),
        name="output",
    )(o_b, proj, gates, gates, y_a, x2d, w_proj_a, w_proj_b, w_out, g_post)


def _rel_bucket(dist):
    max_exact = REL_BUCKETS // 2
    d = jnp.maximum(dist, 1).astype(F32)
    large = max_exact + (jnp.log(d / max_exact) / math.log(REL_MAX_DIST / max_exact)
                         * (REL_BUCKETS - max_exact)).astype(jnp.int32)
    large = jnp.minimum(large, REL_BUCKETS - 1)
    return jnp.where(dist < max_exact, dist, large)


def _group_biases(rel_table):
    out = []
    for gi, (win, dil) in enumerate(DIL_GROUPS):
        dist = jnp.arange(win // dil + 1, dtype=jnp.int32) * dil
        b = rel_table[_rel_bucket(dist)][:, gi * DIL_HEADS:(gi + 1) * DIL_HEADS]
        out.append(b.T.astype(F32))
    return out


def _bias_tiles(biases):
    period = 3 * WIN_BLK
    tiles = []
    for b in biases:
        u = jnp.concatenate([b[:, ::-1], jnp.full((DIL_HEADS, period - WIN_BLK - 1), NEG, F32)], axis=1)
        flat = jnp.tile(u, (1, WIN_BLK))[:, :WIN_BLK * (period - 1)]
        tiles.append(flat.reshape(DIL_HEADS, WIN_BLK, period - 1)[:, :, :2 * WIN_BLK])
    return jnp.stack(tiles, axis=0).astype(F32)


WEIGHT_PREP_ROWS = 128


def _weight_prep_kernel(w_ref, main_ref, ab_ref):
    dst = 0
    for lo, hi in ((SRC_QKV_A, SRC_Z_A), (SRC_QKV_B, SRC_Z_B), (SRC_Z_B, SRC_GATE_A),
                   (SRC_Z_A, SRC_AB), (SRC_GATE_A, SRC_GATE_B), (SRC_GATE_B, SRC_END)):
        main_ref[:, dst:dst + hi - lo] = w_ref[:, lo:hi].astype(BF16)
        dst += hi - lo
    ab = w_ref[:, SRC_AB:SRC_QKV_B].astype(BF16)
    ab_ref[...] = jnp.concatenate([ab, jnp.zeros((ab.shape[0], LANES - ab.shape[1]), BF16)], axis=1)


def _weight_prep(w_in):
    rows = WEIGHT_PREP_ROWS
    return pl.pallas_call(
        _weight_prep_kernel,
        grid=(D_MODEL // rows,),
        in_specs=[pl.BlockSpec((rows, SRC_END), lambda i: (i, 0))],
        out_specs=[pl.BlockSpec((rows, MAIN_WIDTH), lambda i: (i, 0)),
                   pl.BlockSpec((rows, LANES), lambda i: (i, 0))],
        out_shape=[jax.ShapeDtypeStruct((D_MODEL, MAIN_WIDTH), BF16),
                   jax.ShapeDtypeStruct((D_MODEL, LANES), BF16)],
        compiler_params=pltpu.CompilerParams(
            dimension_semantics=("parallel",), vmem_limit_bytes=VMEM_LIMIT),
        name="weight_prep",
    )(w_in)


def _layer_weights(w_in, conv_w, a_log, dt_bias, g_head_norm, w_proj_a, w_proj_b, w_out):
    w_main, w_ab = _weight_prep(w_in)
    pad = (0, LANES - GDN_HEADS)
    return dict(
        w_main=w_main, w_ab=w_ab, conv_wt=conv_w.T.astype(F32),
        alog_pad=jnp.pad(a_log.astype(F32), pad)[None, :],
        dtb_pad=jnp.pad(dt_bias.astype(F32), pad)[None, :],
        ghn=g_head_norm.astype(F32)[None, :],
        w_proj_a=w_proj_a.astype(BF16), w_proj_b=w_proj_b.astype(BF16), w_out=w_out.astype(BF16))


def _window_rows_kernel(k_ref, v_ref, ko_ref, vo_ref):
    rows = k_ref.shape[0]
    for src, dst in ((k_ref, ko_ref), (v_ref, vo_ref)):
        for h in range(DIL_HEADS):
            dst[pl.ds(h, rows, stride=DIL_HEADS), :] = src[:, h * HEAD_DIM:(h + 1) * HEAD_DIM]


def _window_rows(proj3, col_k, keep):
    batch, seq, _ = proj3.shape
    rows = min(keep, 1024)
    first = (seq - keep) // rows
    out = jax.ShapeDtypeStruct((batch, keep * DIL_HEADS, HEAD_DIM), F32)
    k_out, v_out = pl.pallas_call(
        _window_rows_kernel,
        grid=(batch, keep // rows),
        in_specs=[
            pl.BlockSpec((None, rows, DIL_W), lambda b, r: (b, first + r, col_k // DIL_W)),
            pl.BlockSpec((None, rows, DIL_W), lambda b, r: (b, first + r, col_k // DIL_W + 1)),
        ],
        out_specs=[
            pl.BlockSpec((None, rows * DIL_HEADS, HEAD_DIM), lambda b, r: (b, r, 0)),
            pl.BlockSpec((None, rows * DIL_HEADS, HEAD_DIM), lambda b, r: (b, r, 0)),
        ],
        out_shape=[out, out],
        compiler_params=pltpu.CompilerParams(
            dimension_semantics=("parallel", "parallel"), vmem_limit_bytes=VMEM_LIMIT),
        name="window_rows",
    )(proj3, proj3)
    shape = (batch, keep, DIL_HEADS, HEAD_DIM)
    return k_out.reshape(shape), v_out.reshape(shape)


def kernel(x_prompt, x_sample, state_gdn, state_conv, cache_k_w128, cache_v_w128, cache_k_w512,
           cache_v_w512, cache_k_w2048, cache_v_w2048, rel_table, g_pre, w_in, conv_w, a_log, dt_bias,
           g_head_norm, w_proj_a, w_proj_b, w_out, g_post):
    batch, seq, _ = x_prompt.shape
    dec_batch = x_sample.shape[0]
    k_caches = (cache_k_w128, cache_k_w512, cache_k_w2048)
    v_caches = (cache_v_w128, cache_v_w512, cache_v_w2048)
    biases = _group_biases(rel_table)
    bias_tiles = _bias_tiles(biases) * LOG2_E
    lw = _layer_weights(w_in[0], conv_w[0], a_log[0], dt_bias[0], g_head_norm[0], w_proj_a[0],
                        w_proj_b[0], w_out[0])
    g_pre_l = g_pre[0][None, :]
    g_post_l = g_post[0][None, :]

    xp = x_prompt.reshape(batch * seq, D_MODEL)
    proj, gates, ab = _in_projection(xp, g_pre_l, lw["w_main"], lw["w_ab"], tm=INPROJ_ROWS, tn=INPROJ_COLS)
    y_a, p_gdn = _gdn_prompt(proj, gates, ab, lw["conv_wt"], lw["alog_pad"], lw["dtb_pad"], lw["ghn"],
                             batch, seq)
    o_b = _attn_prompt(proj, bias_tiles, batch, seq)
    y_prompt = _output(o_b, proj, gates, y_a, xp, lw["w_proj_a"], lw["w_proj_b"], lw["w_out"], g_post_l,
                       tm=OUTPUT_ROWS)
    y_prompt = y_prompt.reshape(batch, seq, D_MODEL)
    proj3 = proj.reshape(batch, seq, WIDE_WIDTH)
    p_conv = proj3[:, seq - (CONV_WIDTH - 1):, :CONV_DIM]
    p_rows = []
    for gi, (win, _) in enumerate(DIL_GROUPS):
        keep = min(win, seq)
        ck = COL_QKV_B + gi * 3 * DIL_W + DIL_W
        p_rows += list(_window_rows(proj3, ck, keep))

    xs = x_sample.reshape(dec_batch, D_MODEL)
    proj_s, gates_s, ab_s = _in_projection(xs, g_pre_l, lw["w_main"], lw["w_ab"], tm=dec_batch,
                                           tn=INPROJ_COLS)
    sc_t = jnp.transpose(state_conv[0], (1, 0, 2))
    o_s, s_gdn = _gdn_sample(sc_t, proj_s, ab_s, lw["conv_wt"], lw["alog_pad"], lw["dtb_pad"],
                             state_gdn[0], nb=GDN_SAMPLE_SEQS)
    y_a_s = _gdn_sample_out(o_s, gates_s, lw["ghn"])
    q_g, kn_g, vn_g = [], [], []
    for gi in range(N_GROUPS):
        cq = COL_QKV_B + gi * 3 * DIL_W
        q_g.append(proj_s[:, cq:cq + DIL_W].reshape(dec_batch, DIL_HEADS, HEAD_DIM))
        kn_g.append(proj_s[:, cq + DIL_W:cq + 2 * DIL_W].reshape(dec_batch, DIL_HEADS, HEAD_DIM))
        vn_g.append(proj_s[:, cq + 2 * DIL_W:cq + 3 * DIL_W].reshape(dec_batch, DIL_HEADS, HEAD_DIM))
    bias_all = jnp.stack(biases, axis=0)
    bias0 = bias_all[:, None, :, 0:1]
    bias_steps = jnp.transpose(bias_all[:, :, :0:-1], (2, 0, 1))[:, :, None, :, None]
    o_b_s = _attn_sample(q_g, kn_g, vn_g, k_caches, v_caches, bias0, bias_steps)
    y_sample = _output(o_b_s.reshape(dec_batch, DIL_W), proj_s, gates_s, y_a_s, xs, lw["w_proj_a"],
                       lw["w_proj_b"], lw["w_out"], g_post_l, tm=dec_batch)
    y_sample = y_sample.reshape(dec_batch, 1, D_MODEL)
    s_conv = jnp.concatenate([state_conv[0][:, 1:, :], proj_s[:, None, :CONV_DIM]], axis=1)
    s_rows = []
    for gi in range(N_GROUPS):
        s_rows += [kn_g[gi][:, None], vn_g[gi][:, None]]

    return (y_prompt, y_sample, p_gdn[None], p_conv[None], *[r[None] for r in p_rows],
            s_gdn[None], s_conv[None], *[r[None] for r in s_rows])
```
